```python
import jax
import jax.numpy as jnp
from jax import lax
import numpy as np

D_MODEL = 1024
BATCH = 8
SEQ = 2048
DEPTH = 4

GRID_W = 64
CTX_LEN = 256
EPS = 1e-6
ROPE_THETA = 10000.0
Q_BLOCK = 128

A_HEADS = 8
A_NOPE = 64
A_ROPE = 32
A_V = 64
A_QK = A_NOPE + A_ROPE
KV_LORA = 256
Q_LORA = 768
A_WIDTH = A_HEADS * A_V

B_HEADS = 4
B_DK = 64
B_DV = 128
B_KW = B_HEADS * B_DK
B_WIDTH = B_HEADS * B_DV
GATE_RANK = 16
GATE_TAU = 16.0
CHUNK = 64

C_HEADS = 16
C_HD = 64
C_WIDTH = C_HEADS * C_HD
WIN_ROWS = 8
WIN_COLS = 16

EVEN_SPLITS = (Q_LORA, KV_LORA, A_ROPE, A_WIDTH, B_KW, B_KW, B_WIDTH, 2 * GATE_RANK, B_WIDTH)
EVEN_IN = sum(EVEN_SPLITS)
ODD_IN = 4 * C_WIDTH
N_EVEN = (DEPTH + 1) // 2
N_ODD = DEPTH // 2

kernel_name = "hybrid_mla_gla_natten_prefix_dit"


def rms_norm(x, gain):
    xf = x.astype(jnp.float32)
    y = xf * lax.rsqrt(jnp.mean(xf * xf, axis=-1, keepdims=True) + EPS)
    return (y * gain.astype(jnp.float32)).astype(x.dtype)


def split_cols(u, sizes):
    idx = np.cumsum(sizes)[:-1].tolist()
    return jnp.split(u, idx, axis=-1)


def modulation(cond, w, b):
    m = jnp.dot(jax.nn.silu(cond), w) + b
    return jnp.split(m[..., None, :], 3, axis=-1)


def axial_rope_tables(n):
    t = jnp.arange(n)
    row = (t // GRID_W).astype(jnp.float32)
    col = (t % GRID_W).astype(jnp.float32)
    d_ax = A_ROPE // 2
    inv = ROPE_THETA ** (-jnp.arange(0, d_ax, 2, dtype=jnp.float32) / d_ax)
    ang = jnp.stack([row[:, None] * inv, col[:, None] * inv], axis=1)
    return jnp.cos(ang), jnp.sin(ang)


def apply_axial_rope(x, cos, sin):
    shp = x.shape
    xr = x.astype(jnp.float32).reshape(shp[:-1] + (2, 2, shp[-1] // 4))
    x1, x2 = xr[..., 0, :], xr[..., 1, :]
    cs, sn = cos[None, :, None], sin[None, :, None]
    out = jnp.stack([x1 * cs - x2 * sn, x2 * cs + x1 * sn], axis=-2)
    return out.reshape(shp).astype(x.dtype)


def block_attention(q, k, v, scale):
    bsz, n, h, dq = q.shape
    nb = n // Q_BLOCK
    qb = q.reshape(bsz, nb, Q_BLOCK, h, dq).transpose(1, 0, 2, 3, 4)

    def one(qi):
        s = jnp.einsum('bqhd,bkhd->bhqk', qi, k, preferred_element_type=jnp.float32) * scale
        p = jax.nn.softmax(s, axis=-1).astype(v.dtype)
        return jnp.einsum('bhqk,bkhd->bqhd', p, v)

    o = lax.map(one, qb)
    return o.transpose(1, 0, 2, 3, 4).reshape(bsz, n, h, v.shape[-1])


def mla_heads(q_lat, kv_lat, k_rope, q_norm, w_uq, kv_norm, w_ukv, q_gain, k_gain, rope):
    bsz, n, _ = q_lat.shape
    q = jnp.dot(rms_norm(q_lat, q_norm), w_uq).reshape(bsz, n, A_HEADS, A_QK)
    kv = jnp.dot(rms_norm(kv_lat, kv_norm), w_ukv).reshape(bsz, n, A_HEADS, A_NOPE + A_V)
    k = jnp.concatenate(
        [kv[..., :A_NOPE], jnp.broadcast_to(k_rope[:, :, None, :], (bsz, n, A_HEADS, A_ROPE))], axis=-1)
    q = rms_norm(q, q_gain)
    k = rms_norm(k, k_gain)
    if rope is not None:
        cos, sin = rope
        q = jnp.concatenate([q[..., :A_NOPE], apply_axial_rope(q[..., A_NOPE:], cos, sin)], axis=-1)
        k = jnp.concatenate([k[..., :A_NOPE], apply_axial_rope(k[..., A_NOPE:], cos, sin)], axis=-1)
    return q, k, kv[..., A_NOPE:]


def gla_inputs(gq, gk, gv, g_lr, gate_w, gate_b):
    bsz, n, _ = gq.shape
    q = gq.reshape(bsz, n, B_HEADS, B_DK) * (B_DK ** -0.5)
    k = gk.reshape(bsz, n, B_HEADS, B_DK)
    v = gv.reshape(bsz, n, B_HEADS, B_DV)
    r = g_lr.reshape(bsz, n, 2, GATE_RANK)
    logits = jnp.einsum('bnzr,zrk->bnzk', r, gate_w) + gate_b
    g = (jax.nn.log_sigmoid(logits.astype(jnp.float32)) / GATE_TAU).reshape(bsz, n, 2, B_HEADS, B_DK)
    return q, k, v, g[:, :, 0], g[:, :, 1]


def gla_scan(q, k, v, g, s0):
    bsz, n, h, dk = q.shape
    dv = v.shape[-1]
    nc = n // CHUNK

    def to_chunks(t):
        return t.astype(jnp.float32).reshape(bsz, nc, CHUNK, h, t.shape[-1]).transpose(1, 0, 3, 2, 4)

    mask = jnp.tril(jnp.ones((CHUNK, CHUNK), dtype=bool))

    def step(state, inp):
        qi, ki, vi, gi = inp
        b = jnp.cumsum(gi, axis=-2)
        b_last = b[..., -1:, :]
        q_t = qi * jnp.exp(b)
        k_t = ki * jnp.exp(-b)
        att = jnp.where(mask, jnp.einsum('bhld,bhmd->bhlm', q_t, k_t), 0.0)
        o = jnp.einsum('bhld,bhde->bhle', q_t, state) + jnp.einsum('bhlm,bhme->bhle', att, vi)
        k_dec = ki * jnp.exp(b_last - b)
        state = state * jnp.exp(b_last)[..., 0, :, None] + jnp.einsum('bhld,bhle->bhde', k_dec, vi)
        return state, o

    s_final, o = lax.scan(step, s0, (to_chunks(q), to_chunks(k), to_chunks(v), to_chunks(g)))
    return o.transpose(1, 0, 3, 2, 4).reshape(bsz, n, h, dv), s_final


def gla_bidirectional(lat, ctx):
    ql, kl, vl, gfl, gbl = lat
    qc, kc, vc, gfc, gbc = ctx
    s0 = jnp.zeros((ql.shape[0], B_HEADS, B_DK, B_DV), jnp.float32)
    flip = lambda t: jnp.flip(t, axis=1)
    oc_f, sc_f = gla_scan(qc, kc, vc, gfc, s0)
    oc_b, sc_b = gla_scan(flip(qc), flip(kc), flip(vc), flip(gbc), s0)
    ol_f, _ = gla_scan(ql, kl, vl, gfl, sc_f)
    ol_b, _ = gla_scan(flip(ql), flip(kl), flip(vl), flip(gbl), sc_b)
    return ol_f + flip(ol_b), oc_f + flip(oc_b)


def even_mixer(h, hc, rope, w_in, q_norm, w_uq, kv_norm, w_ukv, q_gain, k_gain,
               gate_w, gate_b, gla_norm, w_out, update_ctx):
    lat = split_cols(jnp.dot(h, w_in), EVEN_SPLITS)
    cp = split_cols(jnp.dot(hc, w_in), EVEN_SPLITS)
    mla_p = (q_norm, w_uq, kv_norm, w_ukv, q_gain, k_gain)
    q_x, k_x, v_x = mla_heads(lat[0], lat[1], lat[2], *mla_p, rope)
    q_c, k_c, v_c = mla_heads(cp[0], cp[1], cp[2], *mla_p, None)
    scale = A_QK ** -0.5
    a_x = block_attention(q_x, jnp.concatenate([k_x, k_c], axis=1), jnp.concatenate([v_x, v_c], axis=1), scale)
    b_x, b_c = gla_bidirectional(gla_inputs(lat[4], lat[5], lat[6], lat[7], gate_w, gate_b),
                                 gla_inputs(cp[4], cp[5], cp[6], cp[7], gate_w, gate_b))
    gla_gain = gla_norm.reshape(B_HEADS, B_DV)

    def readout(a, b, za, zb):
        bsz, n = a.shape[:2]
        a = a.reshape(bsz, n, A_WIDTH) * jax.nn.silu(za)
        b = rms_norm(b, gla_gain).reshape(bsz, n, B_WIDTH).astype(zb.dtype) * jax.nn.silu(zb)
        return jnp.dot(jnp.concatenate([a, b], axis=-1), w_out)

    y = readout(a_x, b_x, lat[3], lat[8])
    yc = None
    if update_ctx:
        a_c = block_attention(q_c, k_c, v_c, scale)
        yc = readout(a_c, b_c, cp[3], cp[8])
    return y, yc


def neighborhood_attention(q, k, v, k_ctx, v_ctx, rpb):
    bsz, n, h, d = q.shape
    rows = n // GRID_W
    kr = min(WIN_ROWS, rows)
    kc = WIN_COLS
    scale = d ** -0.5
    kg = k.reshape(bsz, rows, GRID_W, h, d)
    vg = v.reshape(bsz, rows, GRID_W, h, d)
    qg = q.reshape(bsz, rows, GRID_W, h, d).transpose(1, 0, 2, 3, 4)
    col = jnp.arange(GRID_W)
    col_start = jnp.clip(col - kc // 2, 0, GRID_W - kc)
    col_mask = (col[None, :] >= col_start[:, None]) & (col[None, :] < col_start[:, None] + kc)
    col_idx = jnp.clip(col[None, :] - col[:, None] + WIN_COLS - 1, 0, 2 * WIN_COLS - 2)

    def one_row(args):
        qr, r = args
        r_start = jnp.clip(r - kr // 2, 0, rows - kr)
        kb = lax.dynamic_slice_in_dim(kg, r_start, kr, axis=1)
        vb = lax.dynamic_slice_in_dim(vg, r_start, kr, axis=1)
        row_idx = r_start + jnp.arange(kr) - r + WIN_ROWS - 1
        bias = rpb[:, row_idx][:, :, col_idx].transpose(0, 2, 1, 3)
        s_lat = jnp.einsum('bqhd,brkhd->bhqrk', qr, kb, preferred_element_type=jnp.float32) * scale
        s_lat = s_lat + bias.astype(jnp.float32)[None]
        s_lat = jnp.where(col_mask[None, None, :, None, :], s_lat, -jnp.inf).reshape(bsz, h, GRID_W, kr * GRID_W)
        s_ctx = jnp.einsum('bqhd,bchd->bhqc', qr, k_ctx, preferred_element_type=jnp.float32) * scale
        p = jax.nn.softmax(jnp.concatenate([s_lat, s_ctx], axis=-1), axis=-1).astype(v.dtype)
        o = jnp.einsum('bhqk,bkhd->bqhd', p[..., :kr * GRID_W], vb.reshape(bsz, kr * GRID_W, h, d))
        return o + jnp.einsum('bhqc,bchd->bqhd', p[..., kr * GRID_W:], v_ctx)

    o = lax.map(one_row, (qg, jnp.arange(rows)))
    return o.transpose(1, 0, 2, 3, 4).reshape(bsz, n, h, d)


def odd_mixer(h, hc, w_in, q_gain, k_gain, rpb, w_out, update_ctx):
    def heads(u):
        bsz, n, _ = u.shape
        q, k, v, z = jnp.split(u, 4, axis=-1)
        sh = lambda t: t.reshape(bsz, n, C_HEADS, C_HD)
        return rms_norm(sh(q), q_gain), rms_norm(sh(k), k_gain), sh(v), z

    q_x, k_x, v_x, z_x = heads(jnp.dot(h, w_in))
    q_c, k_c, v_c, z_c = heads(jnp.dot(hc, w_in))
    o = neighborhood_attention(q_x, k_x, v_x, k_c, v_c, rpb)
    y = jnp.dot(o.reshape(h.shape[0], h.shape[1], C_WIDTH) * jax.nn.silu(z_x), w_out)
    yc = None
    if update_ctx:
        o_c = block_attention(q_c, k_c, v_c, C_HD ** -0.5)
        yc = jnp.dot(o_c.reshape(hc.shape[0], hc.shape[1], C_WIDTH) * jax.nn.silu(z_c), w_out)
    return y, yc


def setup_inputs(seed: int = 0) -> dict:
    key = jax.random.key(seed)
    ks = jax.random.split(key, 23)
    D = D_MODEL

    def normal(i, shape, s=1.0):
        return s * jax.random.normal(ks[i], shape, jnp.float32)

    def gain(i, shape):
        return 1.0 + normal(i, shape, 0.05)

    return {
        "x": normal(0, (BATCH, SEQ, D)),
        "c": normal(1, (BATCH, D)),
        "ctx": normal(2, (BATCH, CTX_LEN, D)),
        "c_ctx": normal(3, (D,)),
        "norm_g": gain(4, (DEPTH, D)),
        "ada_w": normal(5, (DEPTH, D, 3 * D), 0.5 * D ** -0.5),
        "ada_b": normal(6, (DEPTH, 3 * D), 0.02),
        "ev_w_in": normal(7, (N_EVEN, D, EVEN_IN), D ** -0.5),
        "ev_q_norm": gain(8, (N_EVEN, Q_LORA)),
        "ev_w_uq": normal(9, (N_EVEN, Q_LORA, A_HEADS * A_QK), Q_LORA ** -0.5),
        "ev_kv_norm": gain(10, (N_EVEN, KV_LORA)),
        "ev_w_ukv": normal(11, (N_EVEN, KV_LORA, A_HEADS * (A_NOPE + A_V)), KV_LORA ** -0.5),
        "ev_q_gain": gain(12, (N_EVEN, A_QK)),
        "ev_k_gain": gain(13, (N_EVEN, A_QK)),
        "ev_gate_w": normal(14, (N_EVEN, 2, GATE_RANK, B_KW), GATE_RANK ** -0.5),
        "ev_gate_b": normal(15, (N_EVEN, 2, B_KW), 0.1),
        "ev_gla_norm": gain(16, (N_EVEN, B_WIDTH)),
        "ev_w_out": normal(17, (N_EVEN, A_WIDTH + B_WIDTH, D), (A_WIDTH + B_WIDTH) ** -0.5),
        "od_w_in": normal(18, (N_ODD, D, ODD_IN), D ** -0.5),
        "od_q_gain": gain(19, (N_ODD, C_HD)),
        "od_k_gain": gain(20, (N_ODD, C_HD)),
        "od_rpb": normal(21, (N_ODD, C_HEADS, 2 * WIN_ROWS - 1, 2 * WIN_COLS - 1), 0.1),
        "od_w_out": normal(22, (N_ODD, C_WIDTH, D), C_WIDTH ** -0.5),
    }


def reference(x, c, ctx, c_ctx, norm_g, ada_w, ada_b, ev_w_in, ev_q_norm, ev_w_uq, ev_kv_norm, ev_w_ukv,
              ev_q_gain, ev_k_gain, ev_gate_w, ev_gate_b, ev_gla_norm, ev_w_out,
              od_w_in, od_q_gain, od_k_gain, od_rpb, od_w_out):
    rope = axial_rope_tables(x.shape[1])
    cx = ctx
    for i in range(DEPTH):
        update_ctx = i < DEPTH - 1
        shift, scale, gate = modulation(c, ada_w[i], ada_b[i])
        c_shift, c_scale, c_gate = modulation(c_ctx, ada_w[i], ada_b[i])
        h = rms_norm(x, norm_g[i]) * (1 + scale) + shift
        hc = rms_norm(cx, norm_g[i]) * (1 + c_scale) + c_shift
        j = i // 2
        if i % 2 == 0:
            y, yc = even_mixer(h, hc, rope, ev_w_in[j], ev_q_norm[j], ev_w_uq[j], ev_kv_norm[j], ev_w_ukv[j],
                               ev_q_gain[j], ev_k_gain[j], ev_gate_w[j], ev_gate_b[j], ev_gla_norm[j],
                               ev_w_out[j], update_ctx)
        else:
            y, yc = odd_mixer(h, hc, od_w_in[j], od_q_gain[j], od_k_gain[j], od_rpb[j], od_w_out[j], update_ctx)
        x = x + gate * y
        if update_ctx:
            cx = cx + c_gate * yc
    return x
```

```python
import functools

import jax
import jax.numpy as jnp
from jax import lax
from jax.experimental import pallas as pl
from jax.experimental.pallas import tpu as pltpu

D = 1024
BATCH = 8
SEQ = 2048
DEPTH = 4
GRID_W = 64
GRID_ROWS = SEQ // GRID_W
CTX = 256
T = SEQ + CTX
EPS = 1e-6
ROPE_THETA = 10000.0

A_HEADS = 8
A_NOPE = 64
A_ROPE = 32
A_V = 64
A_QK = A_NOPE + A_ROPE
KV_LORA = 256
Q_LORA = 768
A_WIDTH = A_HEADS * A_V

B_HEADS = 4
B_DK = 64
B_DV = 128
B_KW = B_HEADS * B_DK
B_WIDTH = B_HEADS * B_DV
GATE_RANK = 16
GATE_TAU = 16.0
CHUNK = 64

C_HEADS = 16
C_HD = 64
C_WIDTH = C_HEADS * C_HD
WIN_ROWS = 8
WIN_COLS = 16

LANES = 128
TILE = 256
N_TILES = T // TILE
CTX_TILE = N_TILES - 1
MOD_ROWS = 16
EVEN_COLS = 3200
NEG = -1e30
VMEM_LIMIT = 56 * 1024 * 1024

F32 = jnp.float32
BF16 = jnp.bfloat16


def _dot(a, b):
    return lax.dot_general(a, b, (((1,), (0,)), ((), ())), preferred_element_type=F32)


def _dot_nt(a, b):
    return lax.dot_general(a, b, (((1,), (1,)), ((), ())), preferred_element_type=F32)


def _dot_tn(a, b):
    return lax.dot_general(a, b, (((0,), (0,)), ((), ())), preferred_element_type=F32)


def _silu(x):
    return x / (1.0 + jnp.exp(-x))


def _rms(x, n):
    return x * lax.rsqrt(jnp.sum(x * x, axis=-1, keepdims=True) * (1.0 / n) + EPS)


def _split3(x):
    hi = x.astype(BF16)
    r1 = x - hi.astype(F32)
    mid = r1.astype(BF16)
    lo = (r1 - mid.astype(F32)).astype(BF16)
    return hi, mid, lo


def _params(sem):
    return pltpu.CompilerParams(dimension_semantics=sem, vmem_limit_bytes=VMEM_LIMIT)


def _const_spec(shape):
    nd = len(shape)
    return pl.BlockSpec(shape, lambda *_: (0,) * nd)


def _mod_index(b, t):
    return (jnp.where(t == CTX_TILE, BATCH, b), 0, 0)


def _mod_kernel(c_ref, w_ref, b_ref, o_ref):
    a = _silu(c_ref[...]).astype(BF16)
    o_ref[0] = _dot(a, w_ref[0].astype(BF16)) + b_ref[0]


def _modulation(c16, ada_w, ada_b):
    return pl.pallas_call(
        _mod_kernel,
        out_shape=jax.ShapeDtypeStruct((DEPTH, MOD_ROWS, 3 * D), F32),
        grid=(DEPTH, 3),
        in_specs=[
            pl.BlockSpec((MOD_ROWS, D), lambda i, j: (0, 0)),
            pl.BlockSpec((1, D, D), lambda i, j: (i, 0, j)),
            pl.BlockSpec((1, 1, D), lambda i, j: (i, 0, j)),
        ],
        out_specs=pl.BlockSpec((1, MOD_ROWS, D), lambda i, j: (i, 0, j)),
        compiler_params=_params(("parallel", "parallel")),
        name="modulation",
    )(c16, ada_w, ada_b.reshape(DEPTH, 1, 3 * D))


def _modulated_norm(x, mod_ref, ng_ref):
    return _rms(x, D) * ng_ref[...] * (1.0 + mod_ref[0, 1:2, :]) + mod_ref[0, 0:1, :]


def _even_in_kernel(x_ref, mod_ref, ng_ref, win_ref, qn_ref, wuq_ref, kvn_ref, wukv_ref, qg_ref, kg_ref,
                    wg_ref, gbias_ref, cos_ref, sina_ref, sinb_ref,
                    q_ref, k_ref, v_ref, za_ref, gq_ref, gk_ref, gv_ref, gf_ref, gb_ref, zb_ref):
    h = _modulated_norm(x_ref[0], mod_ref, ng_ref)
    lat = _dot(h.astype(BF16), win_ref[...])
    q_lat = lat[:, 0:768]
    kv_lat = lat[:, 768:1024]
    za_ref[0] = lat[:, 1024:1536]
    gq_ref[0] = lat[:, 1536:1792] * (B_DK ** -0.5)
    gk_ref[0] = lat[:, 1792:2048]
    gv_ref[0] = lat[:, 2048:2560].astype(BF16)
    zb_ref[0] = lat[:, 2560:3072]
    misc = lat[:, 3072:3200]

    cos = cos_ref[...]
    sina = sina_ref[...]
    sinb = sinb_ref[...]

    def rope(u):
        return u * cos + pltpu.roll(u, LANES - 8, 1) * sina + pltpu.roll(u, 8, 1) * sinb

    qf = _dot((_rms(q_lat, Q_LORA) * qn_ref[...]).astype(BF16), wuq_ref[...])
    kvf = _dot((_rms(kv_lat, KV_LORA) * kvn_ref[...]).astype(BF16), wukv_ref[...])
    lane = lax.broadcasted_iota(jnp.int32, (1, LANES), 1)
    k_rope = jnp.where((lane >= A_NOPE) & (lane < A_QK), misc, 0.0)
    for hh in range(A_HEADS):
        sl = slice(hh * LANES, (hh + 1) * LANES)
        qh = rope(_rms(qf[:, sl], A_QK) * qg_ref[...])
        q_ref[0, hh] = (qh * (A_QK ** -0.5)).astype(BF16)
        kh = rope(_rms(kvf[:, sl] + k_rope, A_QK) * kg_ref[...])
        k_ref[0, hh] = kh.astype(BF16)
    v_ref[0] = kvf[:, A_HEADS * LANES:].astype(BF16)

    logits = _dot(misc.astype(BF16), wg_ref[...]) + gbias_ref[...]
    g = (jnp.minimum(logits, 0.0) - jnp.log1p(jnp.exp(-jnp.abs(logits)))) * (1.0 / GATE_TAU)
    gf_ref[0] = g[:, :B_KW]
    gb_ref[0] = g[:, B_KW:]


def _even_in(xu, mod, ng, w):
    tok = lambda width: pl.BlockSpec((1, TILE, width), lambda b, t: (b, t, 0))
    head = pl.BlockSpec((1, A_HEADS, TILE, LANES), lambda b, t: (b, 0, t, 0))
    rope_spec = pl.BlockSpec((TILE, LANES), lambda b, t: (t, 0))
    sd = jax.ShapeDtypeStruct
    return pl.pallas_call(
        _even_in_kernel,
        out_shape=(
            sd((BATCH, A_HEADS, T, LANES), BF16),
            sd((BATCH, A_HEADS, T, LANES), BF16),
            sd((BATCH, T, A_WIDTH), BF16),
            sd((BATCH, T, A_WIDTH), F32),
            sd((BATCH, T, B_KW), F32),
            sd((BATCH, T, B_KW), F32),
            sd((BATCH, T, B_WIDTH), BF16),
            sd((BATCH, T, B_KW), F32),
            sd((BATCH, T, B_KW), F32),
            sd((BATCH, T, B_WIDTH), F32),
        ),
        grid=(BATCH, N_TILES),
        in_specs=[
            tok(D),
            pl.BlockSpec((1, 3, D), _mod_index),
            _const_spec((1, D)),
            _const_spec((D, EVEN_COLS)),
            _const_spec((1, Q_LORA)),
            _const_spec((Q_LORA, A_HEADS * LANES)),
            _const_spec((1, KV_LORA)),
            _const_spec((KV_LORA, A_HEADS * LANES + A_WIDTH)),
            _const_spec((1, LANES)),
            _const_spec((1, LANES)),
            _const_spec((LANES, 2 * B_KW)),
            _const_spec((1, 2 * B_KW)),
            rope_spec, rope_spec, rope_spec,
        ],
        out_specs=(head, head, tok(A_WIDTH), tok(A_WIDTH), tok(B_KW), tok(B_KW), tok(B_WIDTH),
                   tok(B_KW), tok(B_KW), tok(B_WIDTH)),
        compiler_params=_params(("parallel", "parallel")),
        name="even_in",
    )(xu, mod, ng, w["w_in"], w["q_norm"], w["w_uq"], w["kv_norm"], w["w_ukv"], w["q_gain"], w["k_gain"],
      w["w_gate"], w["gate_b"], w["cos"], w["sina"], w["sinb"])


def _softmax_pv(s_list, v_list):
    m = functools.reduce(jnp.maximum, [jnp.max(s, axis=-1, keepdims=True) for s in s_list])
    p_list = [jnp.exp(s - m) for s in s_list]
    l = functools.reduce(jnp.add, [jnp.sum(p, axis=-1, keepdims=True) for p in p_list])
    o = functools.reduce(jnp.add, [_dot(p.astype(BF16), v) for p, v in zip(p_list, v_list)])
    return o / l


def _mla_attn_kernel(q_ref, k_ref, v_ref, z_ref, o_ref):
    t = pl.program_id(2)
    first_head = lax.broadcasted_iota(jnp.int32, (1, LANES), 1) < A_V

    def attend(keys):
        v2 = v_ref[0, keys, :]
        o0 = _softmax_pv([_dot_nt(q_ref[0, 0], k_ref[0, 0, keys, :])], [v2])
        o1 = _softmax_pv([_dot_nt(q_ref[0, 1], k_ref[0, 1, keys, :])], [v2])
        o_ref[0] = (jnp.where(first_head, o0, o1) * _silu(z_ref[0])).astype(BF16)

    @pl.when(t < CTX_TILE)
    def _():
        attend(slice(0, T))

    @pl.when(t == CTX_TILE)
    def _():
        attend(slice(SEQ, T))


def _mla_attn(q, k, v, za):
    return pl.pallas_call(
        _mla_attn_kernel,
        out_shape=jax.ShapeDtypeStruct((BATCH, T, A_WIDTH), BF16),
        grid=(BATCH, A_HEADS // 2, N_TILES),
        in_specs=[
            pl.BlockSpec((1, 2, TILE, LANES), lambda b, p, t: (b, p, t, 0)),
            pl.BlockSpec((1, 2, T, LANES), lambda b, p, t: (b, p, 0, 0)),
            pl.BlockSpec((1, T, LANES), lambda b, p, t: (b, 0, p)),
            pl.BlockSpec((1, TILE, LANES), lambda b, p, t: (b, t, p)),
        ],
        out_specs=pl.BlockSpec((1, TILE, LANES), lambda b, p, t: (b, t, p)),
        compiler_params=_params(("parallel", "parallel", "parallel")),
        name="mla_attn",
    )(q, k, v, za)


def _gla_direction(q_ref, k_ref, v_ref, g_ref, o_ref, s_ref, forward):
    q = q_ref[0]
    k = k_ref[0]
    g = g_ref[0]
    row = lax.broadcasted_iota(jnp.int32, (TILE, TILE), 0)
    col = lax.broadcasted_iota(jnp.int32, (TILE, TILE), 1)
    same_chunk = (row // CHUNK) == (col // CHUNK)
    tri = (same_chunk & ((col <= row) if forward else (col >= row))).astype(BF16)
    b = functools.reduce(jnp.add, [_dot(tri, part) for part in _split3(g)])

    lane = lax.broadcasted_iota(jnp.int32, (1, LANES), 1)
    head_mask = (lane < B_DK, lane >= B_DK)
    crow = lax.broadcasted_iota(jnp.int32, (CHUNK, CHUNK), 0)
    ccol = lax.broadcasted_iota(jnp.int32, (CHUNK, CHUNK), 1)
    causal = (ccol <= crow) if forward else (ccol >= crow)

    n_chunks = TILE // CHUNK
    for c in (range(n_chunks) if forward else reversed(range(n_chunks))):
        sl = slice(c * CHUNK, (c + 1) * CHUNK)
        bc = b[sl]
        b_last = bc[CHUNK - 1:CHUNK] if forward else bc[0:1]
        q_t = q[sl] * jnp.exp(bc)
        k_t = (k[sl] * jnp.exp(-bc)).astype(BF16)
        k_dec = (k[sl] * jnp.exp(b_last - bc)).astype(BF16)
        vc = v_ref[0, sl, :]
        state = s_ref[...]
        state_b = state.astype(BF16)
        outs = []
        for hh in range(2):
            qm = jnp.where(head_mask[hh], q_t, 0.0).astype(BF16)
            att = jnp.where(causal, _dot_nt(qm, k_t), 0.0)
            outs.append(_dot_nt(qm, state_b) + _dot(att.astype(BF16), vc[:, hh * B_DV:(hh + 1) * B_DV]))
        o_ref[0, sl, :] = jnp.concatenate(outs, axis=1)
        upd = _dot_tn(vc, k_dec)
        s_ref[...] = state * jnp.exp(b_last) + jnp.where(head_mask[0], upd[:B_DV], upd[B_DV:])


def _gla_kernel(qf_ref, kf_ref, vf_ref, gf_ref, qb_ref, kb_ref, vb_ref, gb_ref, of_ref, ob_ref, sf_ref, sb_ref):
    @pl.when(pl.program_id(2) == 0)
    def _():
        sf_ref[...] = jnp.zeros_like(sf_ref)
        sb_ref[...] = jnp.zeros_like(sb_ref)

    _gla_direction(qf_ref, kf_ref, vf_ref, gf_ref, of_ref, sf_ref, True)
    _gla_direction(qb_ref, kb_ref, vb_ref, gb_ref, ob_ref, sb_ref, False)


def _gla(gq, gk, gv, gf, gb):
    fwd_tile = lambda s: (s + CTX_TILE) % N_TILES
    bwd_tile = lambda s: CTX_TILE - s
    kspec = lambda tile: pl.BlockSpec((1, TILE, LANES), lambda b, p, s: (b, tile(s), p))
    vspec = lambda tile: pl.BlockSpec((1, TILE, 2 * B_DV), lambda b, p, s: (b, tile(s), p))
    out = jax.ShapeDtypeStruct((BATCH, T, B_WIDTH), F32)
    return pl.pallas_call(
        _gla_kernel,
        out_shape=(out, out),
        grid=(BATCH, B_HEADS // 2, N_TILES),
        in_specs=[kspec(fwd_tile), kspec(fwd_tile), vspec(fwd_tile), kspec(fwd_tile),
                  kspec(bwd_tile), kspec(bwd_tile), vspec(bwd_tile), kspec(bwd_tile)],
        out_specs=(vspec(fwd_tile), vspec(bwd_tile)),
        scratch_shapes=[pltpu.VMEM((B_DV, LANES), F32), pltpu.VMEM((B_DV, LANES), F32)],
        compiler_params=_params(("parallel", "parallel", "arbitrary")),
        name="gla_scan",
    )(gq, gk, gv, gf, gq, gk, gv, gb)


def _even_out_kernel(a_ref, of_ref, ob_ref, zb_ref, gn_ref, wa_ref, wb_ref, x_ref, mod_ref, o_ref):
    bsum = of_ref[0] + ob_ref[0]
    zb = zb_ref[0]
    parts = []
    for hh in range(B_HEADS):
        sl = slice(hh * B_DV, (hh + 1) * B_DV)
        parts.append((_rms(bsum[:, sl], B_DV) * gn_ref[:, sl] * _silu(zb[:, sl])).astype(BF16))
    y = _dot(a_ref[0], wa_ref[...]) + _dot(jnp.concatenate(parts, axis=1), wb_ref[...])
    o_ref[0] = x_ref[0] + mod_ref[0, 2:3, :] * y


def _even_out(a, of, ob, zb, gn, wa, wb, xu, mod):
    tok = lambda width: pl.BlockSpec((1, TILE, width), lambda b, t: (b, t, 0))
    return pl.pallas_call(
        _even_out_kernel,
        out_shape=jax.ShapeDtypeStruct((BATCH, T, D), F32),
        grid=(BATCH, N_TILES),
        in_specs=[tok(A_WIDTH), tok(B_WIDTH), tok(B_WIDTH), tok(B_WIDTH), _const_spec((1, B_WIDTH)),
                  _const_spec((A_WIDTH, D)), _const_spec((B_WIDTH, D)), tok(D),
                  pl.BlockSpec((1, 3, D), _mod_index)],
        out_specs=tok(D),
        compiler_params=_params(("parallel", "parallel")),
        name="even_out",
    )(a, of, ob, zb, gn, wa, wb, xu, mod)


def _odd_in_kernel(x_ref, mod_ref, ng_ref, win_ref, qg_ref, kg_ref, q_ref, k_ref, v_ref, z_ref):
    h = _modulated_norm(x_ref[0], mod_ref, ng_ref)
    u = _dot(h.astype(BF16), win_ref[...])
    grow = lax.broadcasted_iota(jnp.int32, (LANES, LANES), 0) // C_HD
    gcol = lax.broadcasted_iota(jnp.int32, (LANES, LANES), 1) // C_HD
    group = (grow == gcol).astype(BF16)

    def head_norm(xb, gain):
        ss = functools.reduce(jnp.add, [_dot(part, group) for part in _split3(xb * xb)])
        return xb * lax.rsqrt(ss * (1.0 / C_HD) + EPS) * gain

    for j in range(C_WIDTH // LANES):
        sl = slice(j * LANES, (j + 1) * LANES)
        q_ref[0, :, sl] = (head_norm(u[:, sl], qg_ref[...]) * (C_HD ** -0.5)).astype(BF16)
        ks = slice(C_WIDTH + j * LANES, C_WIDTH + (j + 1) * LANES)
        k_ref[0, :, sl] = head_norm(u[:, ks], kg_ref[...]).astype(BF16)
    v_ref[0] = u[:, 2 * C_WIDTH:3 * C_WIDTH].astype(BF16)
    z_ref[0] = u[:, 3 * C_WIDTH:]


def _odd_in(xu, mod, ng, win, qg, kg):
    tok = lambda width: pl.BlockSpec((1, TILE, width), lambda b, t: (b, t, 0))
    sd = jax.ShapeDtypeStruct
    return pl.pallas_call(
        _odd_in_kernel,
        out_shape=(sd((BATCH, T, C_WIDTH), BF16), sd((BATCH, T, C_WIDTH), BF16), sd((BATCH, T, C_WIDTH), BF16),
                   sd((BATCH, T, C_WIDTH), F32)),
        grid=(BATCH, N_TILES),
        in_specs=[tok(D), pl.BlockSpec((1, 3, D), _mod_index), _const_spec((1, D)), _const_spec((D, 4 * C_WIDTH)),
                  _const_spec((1, LANES)), _const_spec((1, LANES))],
        out_specs=(tok(C_WIDTH), tok(C_WIDTH), tok(C_WIDTH), tok(C_WIDTH)),
        compiler_params=_params(("parallel", "parallel")),
        name="odd_in",
    )(xu, mod, ng, win, qg, kg)


ROWS_PER_TILE = TILE // GRID_W
WIN_KEYS = WIN_ROWS * GRID_W


def _nbr_attn_kernel(q_ref, k_ref, v_ref, z_ref, bias_ref, o_ref):
    t = pl.program_id(2)
    lane = lax.broadcasted_iota(jnp.int32, (1, LANES), 1)
    head_mask = (lane < C_HD, lane >= C_HD)
    ctx = slice(SEQ, T)

    @pl.when(t < CTX_TILE)
    def _():
        k_ctx = k_ref[0, ctx, :]
        v_ctx = v_ref[0, ctx, :]
        for rr in range(ROWS_PER_TILE):
            r = t * ROWS_PER_TILE + rr
            r_start = jnp.clip(r - WIN_ROWS // 2, 0, GRID_ROWS - WIN_ROWS)
            win = pl.ds(pl.multiple_of(r_start * GRID_W, GRID_W), WIN_KEYS)
            k_win = k_ref[0, win, :]
            v_win = v_ref[0, win, :]
            rel = r_start - r + WIN_ROWS - 1
            rows = slice(rr * GRID_W, (rr + 1) * GRID_W)
            q2 = q_ref[0, rows, :]
            outs = []
            for hh in range(2):
                qm = jnp.where(head_mask[hh], q2, jnp.zeros_like(q2))
                bias = jnp.concatenate([bias_ref[0, hh, rel + j] for j in range(0, WIN_ROWS, 2)], axis=1)
                s_lat = _dot_nt(qm, k_win) + bias
                s_ctx = _dot_nt(qm, k_ctx)
                outs.append(_softmax_pv([s_lat, s_ctx], [v_win, v_ctx]))
            o = jnp.where(head_mask[0], outs[0], outs[1])
            o_ref[0, rows, :] = (o * _silu(z_ref[0, rows, :])).astype(BF16)

    @pl.when(t == CTX_TILE)
    def _():
        q2 = q_ref[0]
        k_ctx = k_ref[0, ctx, :]
        v_ctx = v_ref[0, ctx, :]
        outs = []
        for hh in range(2):
            qm = jnp.where(head_mask[hh], q2, jnp.zeros_like(q2))
            outs.append(_softmax_pv([_dot_nt(qm, k_ctx)], [v_ctx]))
        o = jnp.where(head_mask[0], outs[0], outs[1])
        o_ref[0] = (o * _silu(z_ref[0])).astype(BF16)


def _nbr_attn(q, k, v, z, bias):
    pair = lambda rows: pl.BlockSpec((1, rows, LANES), lambda b, p, t: (b, 0 if rows == T else t, p))
    return pl.pallas_call(
        _nbr_attn_kernel,
        out_shape=jax.ShapeDtypeStruct((BATCH, T, C_WIDTH), BF16),
        grid=(BATCH, C_HEADS // 2, N_TILES),
        in_specs=[pair(TILE), pair(T), pair(T), pair(TILE),
                  pl.BlockSpec((1, 2, 2 * WIN_ROWS - 2, GRID_W, LANES), lambda b, p, t: (p, 0, 0, 0, 0))],
        out_specs=pair(TILE),
        compiler_params=_params(("parallel", "parallel", "parallel")),
        name="nbr_attn",
    )(q, k, v, z, bias)


def _odd_out_kernel(o_ref, w_ref, x_ref, mod_ref, y_ref):
    y_ref[0] = x_ref[0] + mod_ref[0, 2:3, :] * _dot(o_ref[0], w_ref[...])


def _odd_out(o, w, xu, mod, n_tiles):
    tok = lambda width: pl.BlockSpec((1, TILE, width), lambda b, t: (b, t, 0))
    return pl.pallas_call(
        _odd_out_kernel,
        out_shape=jax.ShapeDtypeStruct((BATCH, n_tiles * TILE, D), F32),
        grid=(BATCH, n_tiles),
        in_specs=[tok(C_WIDTH), _const_spec((C_WIDTH, D)), tok(D), pl.BlockSpec((1, 3, D), _mod_index)],
        out_specs=tok(D),
        compiler_params=_params(("parallel", "parallel")),
        name="odd_out",
    )(o, w, xu, mod)


def _rope_tables():
    n = jnp.arange(SEQ)
    pos = jnp.stack([(n // GRID_W).astype(F32), (n % GRID_W).astype(F32)], axis=1)
    d_ax = A_ROPE // 2
    inv = ROPE_THETA ** (-jnp.arange(0, d_ax, 2, dtype=F32) / d_ax)
    ang = pos[:, :, None] * inv
    cos, sin, zero = jnp.cos(ang), jnp.sin(ang), jnp.zeros_like(ang)
    lay = lambda first, second: jnp.stack([first, second], axis=2).reshape(SEQ, A_ROPE)

    def place(tab, fill):
        full = jnp.full((T, LANES), fill, F32)
        return full.at[:SEQ, A_NOPE:A_QK].set(tab)

    return place(lay(cos, cos), 1.0), place(lay(-sin, zero), 0.0), place(lay(zero, sin), 0.0)


def _pad_lanes(v, width=LANES):
    return jnp.pad(v, (0, width - v.shape[0])).reshape(1, width)


def _even_weights(j, p, rope):
    w_in = p["ev_w_in"][j]
    q_l, kv_l, k_r, z_a, q_b, k_b, v_b, lr, z_b = jnp.split(
        w_in, [768, 1024, 1056, 1568, 1824, 2080, 2592, 2624], axis=1)
    zeros = jnp.zeros((D, 32), F32)
    w_in2 = jnp.concatenate([q_l, kv_l, z_a, q_b, k_b, v_b, z_b, lr, zeros, k_r, zeros], axis=1).astype(BF16)
    w_uq = jnp.pad(p["ev_w_uq"][j].reshape(Q_LORA, A_HEADS, A_QK), ((0, 0), (0, 0), (0, LANES - A_QK)))
    w_ukv = p["ev_w_ukv"][j].reshape(KV_LORA, A_HEADS, A_NOPE + A_V)
    w_uk = jnp.pad(w_ukv[:, :, :A_NOPE], ((0, 0), (0, 0), (0, LANES - A_NOPE))).reshape(KV_LORA, A_HEADS * LANES)
    w_uv = w_ukv[:, :, A_NOPE:].reshape(KV_LORA, A_WIDTH)
    gate_w = p["ev_gate_w"][j]
    w_gate = jnp.zeros((LANES, 2 * B_KW), F32)
    w_gate = w_gate.at[:GATE_RANK, :B_KW].set(gate_w[0]).at[GATE_RANK:2 * GATE_RANK, B_KW:].set(gate_w[1])
    w_out = p["ev_w_out"][j].astype(BF16)
    return {
        "w_in": w_in2,
        "q_norm": p["ev_q_norm"][j].reshape(1, Q_LORA),
        "w_uq": w_uq.reshape(Q_LORA, A_HEADS * LANES).astype(BF16),
        "kv_norm": p["ev_kv_norm"][j].reshape(1, KV_LORA),
        "w_ukv": jnp.concatenate([w_uk, w_uv], axis=1).astype(BF16),
        "q_gain": _pad_lanes(p["ev_q_gain"][j]),
        "k_gain": _pad_lanes(p["ev_k_gain"][j]),
        "w_gate": w_gate.astype(BF16),
        "gate_b": p["ev_gate_b"][j].reshape(1, 2 * B_KW),
        "cos": rope[0], "sina": rope[1], "sinb": rope[2],
        "gla_norm": p["ev_gla_norm"][j].reshape(1, B_WIDTH),
        "w_out_a": w_out[:A_WIDTH],
        "w_out_b": w_out[A_WIDTH:],
    }


def _nbr_bias_table(rpb):
    col = jnp.arange(GRID_W)
    col_start = jnp.clip(col - WIN_COLS // 2, 0, GRID_W - WIN_COLS)
    col_mask = (col[None, :] >= col_start[:, None]) & (col[None, :] < col_start[:, None] + WIN_COLS)
    col_idx = jnp.clip(col[None, :] - col[:, None] + WIN_COLS - 1, 0, 2 * WIN_COLS - 2)
    tab = jnp.where(col_mask[None, None], rpb[:, :, col_idx], NEG)
    pairs = jnp.concatenate([tab[:, :-1], tab[:, 1:]], axis=-1)
    return pairs.reshape(C_HEADS // 2, 2, 2 * WIN_ROWS - 2, GRID_W, LANES)


def kernel(x, c, ctx, c_ctx, norm_g, ada_w, ada_b, ev_w_in, ev_q_norm, ev_w_uq, ev_kv_norm, ev_w_ukv, ev_q_gain,
           ev_k_gain, ev_gate_w, ev_gate_b, ev_gla_norm, ev_w_out, od_w_in, od_q_gain, od_k_gain, od_rpb, od_w_out):
    p = dict(ev_w_in=ev_w_in, ev_q_norm=ev_q_norm, ev_w_uq=ev_w_uq, ev_kv_norm=ev_kv_norm, ev_w_ukv=ev_w_ukv,
             ev_q_gain=ev_q_gain, ev_k_gain=ev_k_gain, ev_gate_w=ev_gate_w, ev_gate_b=ev_gate_b,
             ev_gla_norm=ev_gla_norm, ev_w_out=ev_w_out)
    c16 = jnp.zeros((MOD_ROWS, D), F32).at[:BATCH].set(c).at[BATCH].set(c_ctx)
    mod_all = _modulation(c16, ada_w, ada_b).reshape(DEPTH, MOD_ROWS, 3, D)
    rope = _rope_tables()
    xu = jnp.concatenate([x, ctx], axis=1)
    for i in range(DEPTH):
        j = i // 2
        mod = mod_all[i]
        ng = norm_g[i].reshape(1, D)
        if i % 2 == 0:
            w = _even_weights(j, p, rope)
            q, k, v, za, gq, gk, gv, gf, gb, zb = _even_in(xu, mod, ng, w)
            a = _mla_attn(q, k, v, za)
            of, ob = _gla(gq, gk, gv, gf, gb)
            xu = _even_out(a, of, ob, zb, w["gla_norm"], w["w_out_a"], w["w_out_b"], xu, mod)
        else:
            q, k, v, z = _odd_in(xu, mod, ng, od_w_in[j].astype(BF16), _pad_lanes(jnp.tile(od_q_gain[j], 2)),
                                 _pad_lanes(jnp.tile(od_k_gain[j], 2)))
            o = _nbr_attn(q, k, v, z, _nbr_bias_table(od_rpb[j]))
            n_tiles = N_TILES if i < DEPTH - 1 else CTX_TILE
            xu = _odd_out(o, od_w_out[j].astype(BF16), xu, mod, n_tiles)
    return xu
```

```python
import functools

import jax
import jax.numpy as jnp
from jax import lax
from jax.experimental import pallas as pl
from jax.experimental.pallas import tpu as pltpu

D = 1024
BATCH = 8
SEQ = 2048
DEPTH = 4
GRID_W = 64
GRID_ROWS = SEQ // GRID_W
CTX = 256
T = SEQ + CTX
EPS = 1e-6
ROPE_THETA = 10000.0

A_HEADS = 8
A_NOPE = 64
A_ROPE = 32
A_V = 64
A_QK = A_NOPE + A_ROPE
KV_LORA = 256
Q_LORA = 768
A_WIDTH = A_HEADS * A_V

B_HEADS = 4
B_DK = 64
B_DV = 128
B_KW = B_HEADS * B_DK
B_WIDTH = B_HEADS * B_DV
GATE_RANK = 16
GATE_TAU = 16.0
CHUNK = 64

C_HEADS = 16
C_HD = 64
C_WIDTH = C_HEADS * C_HD
WIN_ROWS = 8
WIN_COLS = 16

LANES = 128
TILE = 256
N_TILES = T // TILE
CTX_TILE = N_TILES - 1
MOD_ROWS = 16
EVEN_COLS = 3200
NEG = -1e30
VMEM_LIMIT = 56 * 1024 * 1024

F32 = jnp.float32
BF16 = jnp.bfloat16


def _dot(a, b):
    return lax.dot_general(a, b, (((1,), (0,)), ((), ())), preferred_element_type=F32)


def _dot_nt(a, b):
    return lax.dot_general(a, b, (((1,), (1,)), ((), ())), preferred_element_type=F32)


def _dot_tn(a, b):
    return lax.dot_general(a, b, (((0,), (0,)), ((), ())), preferred_element_type=F32)


def _silu(x):
    return x / (1.0 + jnp.exp(-x))


def _rms(x, n):
    return x * lax.rsqrt(jnp.sum(x * x, axis=-1, keepdims=True) * (1.0 / n) + EPS)


def _split3(x):
    hi = x.astype(BF16)
    r1 = x - hi.astype(F32)
    mid = r1.astype(BF16)
    lo = (r1 - mid.astype(F32)).astype(BF16)
    return hi, mid, lo


def _params(sem):
    return pltpu.CompilerParams(dimension_semantics=sem, vmem_limit_bytes=VMEM_LIMIT)


def _const_spec(shape):
    nd = len(shape)
    return pl.BlockSpec(shape, lambda *_: (0,) * nd)


def _mod_index(b, t):
    return (jnp.where(t == CTX_TILE, BATCH, b), 0, 0)


def _mod_kernel(c_ref, w_ref, b_ref, o_ref):
    a = _silu(c_ref[...]).astype(BF16)
    o_ref[0] = _dot(a, w_ref[0].astype(BF16)) + b_ref[0]


def _modulation(c16, ada_w, ada_b):
    return pl.pallas_call(
        _mod_kernel,
        out_shape=jax.ShapeDtypeStruct((DEPTH, MOD_ROWS, 3 * D), F32),
        grid=(DEPTH, 3),
        in_specs=[
            pl.BlockSpec((MOD_ROWS, D), lambda i, j: (0, 0)),
            pl.BlockSpec((1, D, D), lambda i, j: (i, 0, j)),
            pl.BlockSpec((1, 1, D), lambda i, j: (i, 0, j)),
        ],
        out_specs=pl.BlockSpec((1, MOD_ROWS, D), lambda i, j: (i, 0, j)),
        compiler_params=_params(("parallel", "parallel")),
        name="modulation",
    )(c16, ada_w, ada_b.reshape(DEPTH, 1, 3 * D))


def _modulated_norm(x, mod_ref, ng_ref):
    return _rms(x, D) * ng_ref[...] * (1.0 + mod_ref[0, 1:2, :]) + mod_ref[0, 0:1, :]


def _even_in_kernel(x_ref, mod_ref, ng_ref, win_ref, qn_ref, wuq_ref, kvn_ref, wukv_ref, qg_ref, kg_ref,
                    wg_ref, gbias_ref, cos_ref, sina_ref, sinb_ref,
                    q_ref, k_ref, v_ref, za_ref, gq_ref, gk_ref, gv_ref, gf_ref, gb_ref, zb_ref):
    h = _modulated_norm(x_ref[0], mod_ref, ng_ref)
    lat = _dot(h.astype(BF16), win_ref[...])
    q_lat = lat[:, 0:768]
    kv_lat = lat[:, 768:1024]
    za_ref[0] = lat[:, 1024:1536]
    gq_ref[0] = lat[:, 1536:1792] * (B_DK ** -0.5)
    gk_ref[0] = lat[:, 1792:2048]
    gv_ref[0] = lat[:, 2048:2560].astype(BF16)
    zb_ref[0] = lat[:, 2560:3072]
    misc = lat[:, 3072:3200]

    cos = cos_ref[...]
    sina = sina_ref[...]
    sinb = sinb_ref[...]

    def rope(u):
        return u * cos + pltpu.roll(u, LANES - 8, 1) * sina + pltpu.roll(u, 8, 1) * sinb

    qf = _dot((_rms(q_lat, Q_LORA) * qn_ref[...]).astype(BF16), wuq_ref[...])
    kvf = _dot((_rms(kv_lat, KV_LORA) * kvn_ref[...]).astype(BF16), wukv_ref[...])
    lane = lax.broadcasted_iota(jnp.int32, (1, LANES), 1)
    k_rope = jnp.where((lane >= A_NOPE) & (lane < A_QK), misc, 0.0)
    for hh in range(A_HEADS):
        sl = slice(hh * LANES, (hh + 1) * LANES)
        qh = rope(_rms(qf[:, sl], A_QK) * qg_ref[...])
        q_ref[0, hh] = (qh * (A_QK ** -0.5)).astype(BF16)
        kh = rope(_rms(kvf[:, sl] + k_rope, A_QK) * kg_ref[...])
        k_ref[0, hh] = kh.astype(BF16)
    v_ref[0] = kvf[:, A_HEADS * LANES:].astype(BF16)

    logits = _dot(misc.astype(BF16), wg_ref[...]) + gbias_ref[...]
    g = (jnp.minimum(logits, 0.0) - jnp.log1p(jnp.exp(-jnp.abs(logits)))) * (1.0 / GATE_TAU)
    gf_ref[0] = g[:, :B_KW]
    gb_ref[0] = g[:, B_KW:]


def _even_in(xu, mod, ng, w):
    tok = lambda width: pl.BlockSpec((1, TILE, width), lambda b, t: (b, t, 0))
    head = pl.BlockSpec((1, A_HEADS, TILE, LANES), lambda b, t: (b, 0, t, 0))
    rope_spec = pl.BlockSpec((TILE, LANES), lambda b, t: (t, 0))
    sd = jax.ShapeDtypeStruct
    return pl.pallas_call(
        _even_in_kernel,
        out_shape=(
            sd((BATCH, A_HEADS, T, LANES), BF16),
            sd((BATCH, A_HEADS, T, LANES), BF16),
            sd((BATCH, T, A_WIDTH), BF16),
            sd((BATCH, T, A_WIDTH), F32),
            sd((BATCH, T, B_KW), F32),
            sd((BATCH, T, B_KW), F32),
            sd((BATCH, T, B_WIDTH), BF16),
            sd((BATCH, T, B_KW), F32),
            sd((BATCH, T, B_KW), F32),
            sd((BATCH, T, B_WIDTH), F32),
        ),
        grid=(BATCH, N_TILES),
        in_specs=[
            tok(D),
            pl.BlockSpec((1, 3, D), _mod_index),
            _const_spec((1, D)),
            _const_spec((D, EVEN_COLS)),
            _const_spec((1, Q_LORA)),
            _const_spec((Q_LORA, A_HEADS * LANES)),
            _const_spec((1, KV_LORA)),
            _const_spec((KV_LORA, A_HEADS * LANES + A_WIDTH)),
            _const_spec((1, LANES)),
            _const_spec((1, LANES)),
            _const_spec((LANES, 2 * B_KW)),
            _const_spec((1, 2 * B_KW)),
            rope_spec, rope_spec, rope_spec,
        ],
        out_specs=(head, head, tok(A_WIDTH), tok(A_WIDTH), tok(B_KW), tok(B_KW), tok(B_WIDTH),
                   tok(B_KW), tok(B_KW), tok(B_WIDTH)),
        compiler_params=_params(("parallel", "parallel")),
        name="even_in",
    )(xu, mod, ng, w["w_in"], w["q_norm"], w["w_uq"], w["kv_norm"], w["w_ukv"], w["q_gain"], w["k_gain"],
      w["w_gate"], w["gate_b"], w["cos"], w["sina"], w["sinb"])


def _softmax_pv(s_list, v_list):
    m = functools.reduce(jnp.maximum, [jnp.max(s, axis=-1, keepdims=True) for s in s_list])
    p_list = [jnp.exp(s - m) for s in s_list]
    l = functools.reduce(jnp.add, [jnp.sum(p, axis=-1, keepdims=True) for p in p_list])
    o = functools.reduce(jnp.add, [_dot(p.astype(BF16), v) for p, v in zip(p_list, v_list)])
    return o / l


def _mla_attn_kernel(q_ref, k_ref, v_ref, z_ref, o_ref):
    t = pl.program_id(2)
    first_head = lax.broadcasted_iota(jnp.int32, (1, LANES), 1) < A_V

    def attend(keys):
        v2 = v_ref[0, keys, :]
        o0 = _softmax_pv([_dot_nt(q_ref[0, 0], k_ref[0, 0, keys, :])], [v2])
        o1 = _softmax_pv([_dot_nt(q_ref[0, 1], k_ref[0, 1, keys, :])], [v2])
        o_ref[0] = (jnp.where(first_head, o0, o1) * _silu(z_ref[0])).astype(BF16)

    @pl.when(t < CTX_TILE)
    def _():
        attend(slice(0, T))

    @pl.when(t == CTX_TILE)
    def _():
        attend(slice(SEQ, T))


def _mla_attn(q, k, v, za):
    return pl.pallas_call(
        _mla_attn_kernel,
        out_shape=jax.ShapeDtypeStruct((BATCH, T, A_WIDTH), BF16),
        grid=(BATCH, A_HEADS // 2, N_TILES),
        in_specs=[
            pl.BlockSpec((1, 2, TILE, LANES), lambda b, p, t: (b, p, t, 0)),
            pl.BlockSpec((1, 2, T, LANES), lambda b, p, t: (b, p, 0, 0)),
            pl.BlockSpec((1, T, LANES), lambda b, p, t: (b, 0, p)),
            pl.BlockSpec((1, TILE, LANES), lambda b, p, t: (b, t, p)),
        ],
        out_specs=pl.BlockSpec((1, TILE, LANES), lambda b, p, t: (b, t, p)),
        compiler_params=_params(("parallel", "parallel", "parallel")),
        name="mla_attn",
    )(q, k, v, za)


def _gla_direction(q_ref, k_ref, v_ref, g_ref, o_ref, s_ref, forward):
    q = q_ref[0]
    k = k_ref[0]
    g = g_ref[0]
    row = lax.broadcasted_iota(jnp.int32, (TILE, TILE), 0)
    col = lax.broadcasted_iota(jnp.int32, (TILE, TILE), 1)
    same_chunk = (row // CHUNK) == (col // CHUNK)
    tri = (same_chunk & ((col <= row) if forward else (col >= row))).astype(BF16)
    b = functools.reduce(jnp.add, [_dot(tri, part) for part in _split3(g)])

    lane = lax.broadcasted_iota(jnp.int32, (1, LANES), 1)
    head_mask = (lane < B_DK, lane >= B_DK)
    crow = lax.broadcasted_iota(jnp.int32, (CHUNK, CHUNK), 0)
    ccol = lax.broadcasted_iota(jnp.int32, (CHUNK, CHUNK), 1)
    causal = (ccol <= crow) if forward else (ccol >= crow)

    n_chunks = TILE // CHUNK
    for c in (range(n_chunks) if forward else reversed(range(n_chunks))):
        sl = slice(c * CHUNK, (c + 1) * CHUNK)
        bc = b[sl]
        b_last = bc[CHUNK - 1:CHUNK] if forward else bc[0:1]
        q_t = q[sl] * jnp.exp(bc)
        k_t = (k[sl] * jnp.exp(-bc)).astype(BF16)
        k_dec = (k[sl] * jnp.exp(b_last - bc)).astype(BF16)
        vc = v_ref[0, sl, :]
        state = s_ref[...]
        state_b = state.astype(BF16)
        outs = []
        for hh in range(2):
            qm = jnp.where(head_mask[hh], q_t, 0.0).astype(BF16)
            att = jnp.where(causal, _dot_nt(qm, k_t), 0.0)
            outs.append(_dot_nt(qm, state_b) + _dot(att.astype(BF16), vc[:, hh * B_DV:(hh + 1) * B_DV]))
        o_ref[0, sl, :] = jnp.concatenate(outs, axis=1)
        upd = _dot_tn(vc, k_dec)
        s_ref[...] = state * jnp.exp(b_last) + jnp.where(head_mask[0], upd[:B_DV], upd[B_DV:])


def _gla_kernel(qf_ref, kf_ref, vf_ref, gf_ref, qb_ref, kb_ref, vb_ref, gb_ref, of_ref, ob_ref, sf_ref, sb_ref):
    @pl.when(pl.program_id(2) == 0)
    def _():
        sf_ref[...] = jnp.zeros_like(sf_ref)
        sb_ref[...] = jnp.zeros_like(sb_ref)

    _gla_direction(qf_ref, kf_ref, vf_ref, gf_ref, of_ref, sf_ref, True)
    _gla_direction(qb_ref, kb_ref, vb_ref, gb_ref, ob_ref, sb_ref, False)


def _gla(gq, gk, gv, gf, gb):
    fwd_tile = lambda s: (s + CTX_TILE) % N_TILES
    bwd_tile = lambda s: CTX_TILE - s
    kspec = lambda tile: pl.BlockSpec((1, TILE, LANES), lambda b, p, s: (b, tile(s), p))
    vspec = lambda tile: pl.BlockSpec((1, TILE, 2 * B_DV), lambda b, p, s: (b, tile(s), p))
    out = jax.ShapeDtypeStruct((BATCH, T, B_WIDTH), F32)
    return pl.pallas_call(
        _gla_kernel,
        out_shape=(out, out),
        grid=(BATCH, B_HEADS // 2, N_TILES),
        in_specs=[kspec(fwd_tile), kspec(fwd_tile), vspec(fwd_tile), kspec(fwd_tile),
                  kspec(bwd_tile), kspec(bwd_tile), vspec(bwd_tile), kspec(bwd_tile)],
        out_specs=(vspec(fwd_tile), vspec(bwd_tile)),
        scratch_shapes=[pltpu.VMEM((B_DV, LANES), F32), pltpu.VMEM((B_DV, LANES), F32)],
        compiler_params=_params(("parallel", "parallel", "arbitrary")),
        name="gla_scan",
    )(gq, gk, gv, gf, gq, gk, gv, gb)


def _even_out_kernel(a_ref, of_ref, ob_ref, zb_ref, gn_ref, wa_ref, wb_ref, x_ref, mod_ref, o_ref):
    bsum = of_ref[0] + ob_ref[0]
    zb = zb_ref[0]
    parts = []
    for hh in range(B_HEADS):
        sl = slice(hh * B_DV, (hh + 1) * B_DV)
        parts.append((_rms(bsum[:, sl], B_DV) * gn_ref[:, sl] * _silu(zb[:, sl])).astype(BF16))
    y = _dot(a_ref[0], wa_ref[...]) + _dot(jnp.concatenate(parts, axis=1), wb_ref[...])
    o_ref[0] = x_ref[0] + mod_ref[0, 2:3, :] * y


def _even_out(a, of, ob, zb, gn, wa, wb, xu, mod):
    tok = lambda width: pl.BlockSpec((1, TILE, width), lambda b, t: (b, t, 0))
    return pl.pallas_call(
        _even_out_kernel,
        out_shape=jax.ShapeDtypeStruct((BATCH, T, D), F32),
        grid=(BATCH, N_TILES),
        in_specs=[tok(A_WIDTH), tok(B_WIDTH), tok(B_WIDTH), tok(B_WIDTH), _const_spec((1, B_WIDTH)),
                  _const_spec((A_WIDTH, D)), _const_spec((B_WIDTH, D)), tok(D),
                  pl.BlockSpec((1, 3, D), _mod_index)],
        out_specs=tok(D),
        compiler_params=_params(("parallel", "parallel")),
        name="even_out",
    )(a, of, ob, zb, gn, wa, wb, xu, mod)


def _odd_in_kernel(x_ref, mod_ref, ng_ref, win_ref, qg_ref, kg_ref, q_ref, k_ref, v_ref, z_ref):
    h = _modulated_norm(x_ref[0], mod_ref, ng_ref)
    u = _dot(h.astype(BF16), win_ref[...])
    grow = lax.broadcasted_iota(jnp.int32, (LANES, LANES), 0) // C_HD
    gcol = lax.broadcasted_iota(jnp.int32, (LANES, LANES), 1) // C_HD
    group = (grow == gcol).astype(BF16)

    def head_norm(xb, gain):
        ss = functools.reduce(jnp.add, [_dot(part, group) for part in _split3(xb * xb)])
        return xb * lax.rsqrt(ss * (1.0 / C_HD) + EPS) * gain

    for j in range(C_WIDTH // LANES):
        sl = slice(j * LANES, (j + 1) * LANES)
        q_ref[0, :, sl] = (head_norm(u[:, sl], qg_ref[...]) * (C_HD ** -0.5)).astype(BF16)
        ks = slice(C_WIDTH + j * LANES, C_WIDTH + (j + 1) * LANES)
        k_ref[0, :, sl] = head_norm(u[:, ks], kg_ref[...]).astype(BF16)
    v_ref[0] = u[:, 2 * C_WIDTH:3 * C_WIDTH].astype(BF16)
    z_ref[0] = u[:, 3 * C_WIDTH:]


def _odd_in(xu, mod, ng, win, qg, kg):
    tok = lambda width: pl.BlockSpec((1, TILE, width), lambda b, t: (b, t, 0))
    sd = jax.ShapeDtypeStruct
    return pl.pallas_call(
        _odd_in_kernel,
        out_shape=(sd((BATCH, T, C_WIDTH), BF16), sd((BATCH, T, C_WIDTH), BF16), sd((BATCH, T, C_WIDTH), BF16),
                   sd((BATCH, T, C_WIDTH), F32)),
        grid=(BATCH, N_TILES),
        in_specs=[tok(D), pl.BlockSpec((1, 3, D), _mod_index), _const_spec((1, D)), _const_spec((D, 4 * C_WIDTH)),
                  _const_spec((1, LANES)), _const_spec((1, LANES))],
        out_specs=(tok(C_WIDTH), tok(C_WIDTH), tok(C_WIDTH), tok(C_WIDTH)),
        compiler_params=_params(("parallel", "parallel")),
        name="odd_in",
    )(xu, mod, ng, win, qg, kg)


ROWS_PER_TILE = TILE // GRID_W
ROWS_PER_CHAIN = 2
UNION_ROWS = WIN_ROWS + ROWS_PER_CHAIN
UNION_KEYS = UNION_ROWS * GRID_W
BIAS_SLOTS = 2 * WIN_ROWS


def _nbr_attn_kernel(q_ref, k_ref, v_ref, z_ref, bias_ref, o_ref):
    t = pl.program_id(2)
    lane = lax.broadcasted_iota(jnp.int32, (1, LANES), 1)
    first = lane < C_HD
    head_mask = (first, lane >= C_HD)
    ctx = slice(SEQ, T)

    def stack_heads(blocks):
        zero = jnp.zeros(blocks[0].shape, BF16)
        return jnp.concatenate([jnp.where(head_mask[hh], blk, zero) for blk in blocks for hh in range(2)], axis=0)

    def finish(s, v_all, n_blocks, rows, out_rows):
        m = jnp.max(s, axis=-1, keepdims=True)
        p = jnp.exp(s - m)
        o = _dot(p.astype(BF16), v_all) / jnp.sum(p, axis=-1, keepdims=True)
        o = jnp.concatenate(
            [jnp.where(first, o[(2 * i) * rows:(2 * i + 1) * rows], o[(2 * i + 1) * rows:(2 * i + 2) * rows])
             for i in range(n_blocks)], axis=0)
        o_ref[0, out_rows, :] = (o * _silu(z_ref[0, out_rows, :])).astype(BF16)

    @pl.when(t < CTX_TILE)
    def _():
        k_ctx = k_ref[0, ctx, :]
        v_ctx = v_ref[0, ctx, :]
        for chain in range(ROWS_PER_TILE // ROWS_PER_CHAIN):
            r0 = t * ROWS_PER_TILE + chain * ROWS_PER_CHAIN
            u0 = jnp.clip(r0 - WIN_ROWS // 2, 0, GRID_ROWS - UNION_ROWS)
            win = pl.ds(pl.multiple_of(u0 * GRID_W, 2 * GRID_W), UNION_KEYS)
            k_all = jnp.concatenate([k_ref[0, win, :], k_ctx], axis=0)
            v_all = jnp.concatenate([v_ref[0, win, :], v_ctx], axis=0)
            q_rows = [q_ref[0, (chain * ROWS_PER_CHAIN + rr) * GRID_W:(chain * ROWS_PER_CHAIN + rr + 1) * GRID_W, :]
                      for rr in range(ROWS_PER_CHAIN)]
            s = _dot_nt(stack_heads(q_rows), k_all)
            blocks = []
            for rr in range(ROWS_PER_CHAIN):
                r = r0 + rr
                r_start = jnp.clip(r - WIN_ROWS // 2, 0, GRID_ROWS - WIN_ROWS)
                for hh in range(2):
                    blk = s[(2 * rr + hh) * GRID_W:(2 * rr + hh + 1) * GRID_W]
                    pieces = []
                    for mm in range(UNION_ROWS // 2):
                        u = u0 + 2 * mm
                        ok_l = ((u >= r_start) & (u < r_start + WIN_ROWS)).astype(jnp.int32)
                        ok_r = ((u + 1 >= r_start) & (u + 1 < r_start + WIN_ROWS)).astype(jnp.int32)
                        slot = jnp.clip(u - r + WIN_ROWS, 0, BIAS_SLOTS - 1)
                        in_window = jnp.where(first, ok_l, ok_r) > 0
                        scores = blk[:, mm * LANES:(mm + 1) * LANES] + bias_ref[0, hh, slot]
                        pieces.append(jnp.where(in_window, scores, NEG))
                    pieces.append(blk[:, UNION_KEYS:])
                    blocks.append(jnp.concatenate(pieces, axis=1))
            chain_rows = slice(chain * ROWS_PER_CHAIN * GRID_W, (chain + 1) * ROWS_PER_CHAIN * GRID_W)
            finish(jnp.concatenate(blocks, axis=0), v_all, ROWS_PER_CHAIN, GRID_W, chain_rows)

    @pl.when(t == CTX_TILE)
    def _():
        finish(_dot_nt(stack_heads([q_ref[0]]), k_ref[0, ctx, :]), v_ref[0, ctx, :], 1, TILE, slice(0, TILE))


def _nbr_attn(q, k, v, z, bias):
    pair = lambda rows: pl.BlockSpec((1, rows, LANES), lambda b, p, t: (b, 0 if rows == T else t, p))
    return pl.pallas_call(
        _nbr_attn_kernel,
        out_shape=jax.ShapeDtypeStruct((BATCH, T, C_WIDTH), BF16),
        grid=(BATCH, C_HEADS // 2, N_TILES),
        in_specs=[pair(TILE), pair(T), pair(T), pair(TILE),
                  pl.BlockSpec((1, 2, BIAS_SLOTS, GRID_W, LANES), lambda b, p, t: (p, 0, 0, 0, 0))],
        out_specs=pair(TILE),
        compiler_params=_params(("parallel", "parallel", "parallel")),
        name="nbr_attn",
    )(q, k, v, z, bias)


def _odd_out_kernel(o_ref, w_ref, x_ref, mod_ref, y_ref):
    y_ref[0] = x_ref[0] + mod_ref[0, 2:3, :] * _dot(o_ref[0], w_ref[...])


def _odd_out(o, w, xu, mod, n_tiles):
    tok = lambda width: pl.BlockSpec((1, TILE, width), lambda b, t: (b, t, 0))
    return pl.pallas_call(
        _odd_out_kernel,
        out_shape=jax.ShapeDtypeStruct((BATCH, n_tiles * TILE, D), F32),
        grid=(BATCH, n_tiles),
        in_specs=[tok(C_WIDTH), _const_spec((C_WIDTH, D)), tok(D), pl.BlockSpec((1, 3, D), _mod_index)],
        out_specs=tok(D),
        compiler_params=_params(("parallel", "parallel")),
        name="odd_out",
    )(o, w, xu, mod)


def _rope_tables():
    n = jnp.arange(SEQ)
    pos = jnp.stack([(n // GRID_W).astype(F32), (n % GRID_W).astype(F32)], axis=1)
    d_ax = A_ROPE // 2
    inv = ROPE_THETA ** (-jnp.arange(0, d_ax, 2, dtype=F32) / d_ax)
    ang = pos[:, :, None] * inv
    cos, sin, zero = jnp.cos(ang), jnp.sin(ang), jnp.zeros_like(ang)
    lay = lambda first, second: jnp.stack([first, second], axis=2).reshape(SEQ, A_ROPE)

    def place(tab, fill):
        full = jnp.full((T, LANES), fill, F32)
        return full.at[:SEQ, A_NOPE:A_QK].set(tab)

    return place(lay(cos, cos), 1.0), place(lay(-sin, zero), 0.0), place(lay(zero, sin), 0.0)


def _pad_lanes(v, width=LANES):
    return jnp.pad(v, (0, width - v.shape[0])).reshape(1, width)


def _even_weights(j, p, rope):
    w_in = p["ev_w_in"][j]
    q_l, kv_l, k_r, z_a, q_b, k_b, v_b, lr, z_b = jnp.split(
        w_in, [768, 1024, 1056, 1568, 1824, 2080, 2592, 2624], axis=1)
    zeros = jnp.zeros((D, 32), F32)
    w_in2 = jnp.concatenate([q_l, kv_l, z_a, q_b, k_b, v_b, z_b, lr, zeros, k_r, zeros], axis=1).astype(BF16)
    w_uq = jnp.pad(p["ev_w_uq"][j].reshape(Q_LORA, A_HEADS, A_QK), ((0, 0), (0, 0), (0, LANES - A_QK)))
    w_ukv = p["ev_w_ukv"][j].reshape(KV_LORA, A_HEADS, A_NOPE + A_V)
    w_uk = jnp.pad(w_ukv[:, :, :A_NOPE], ((0, 0), (0, 0), (0, LANES - A_NOPE))).reshape(KV_LORA, A_HEADS * LANES)
    w_uv = w_ukv[:, :, A_NOPE:].reshape(KV_LORA, A_WIDTH)
    gate_w = p["ev_gate_w"][j]
    w_gate = jnp.zeros((LANES, 2 * B_KW), F32)
    w_gate = w_gate.at[:GATE_RANK, :B_KW].set(gate_w[0]).at[GATE_RANK:2 * GATE_RANK, B_KW:].set(gate_w[1])
    w_out = p["ev_w_out"][j].astype(BF16)
    return {
        "w_in": w_in2,
        "q_norm": p["ev_q_norm"][j].reshape(1, Q_LORA),
        "w_uq": w_uq.reshape(Q_LORA, A_HEADS * LANES).astype(BF16),
        "kv_norm": p["ev_kv_norm"][j].reshape(1, KV_LORA),
        "w_ukv": jnp.concatenate([w_uk, w_uv], axis=1).astype(BF16),
        "q_gain": _pad_lanes(p["ev_q_gain"][j]),
        "k_gain": _pad_lanes(p["ev_k_gain"][j]),
        "w_gate": w_gate.astype(BF16),
        "gate_b": p["ev_gate_b"][j].reshape(1, 2 * B_KW),
        "cos": rope[0], "sina": rope[1], "sinb": rope[2],
        "gla_norm": p["ev_gla_norm"][j].reshape(1, B_WIDTH),
        "w_out_a": w_out[:A_WIDTH],
        "w_out_b": w_out[A_WIDTH:],
    }


def _nbr_bias_table(rpb):
    col = jnp.arange(GRID_W)
    col_start = jnp.clip(col - WIN_COLS // 2, 0, GRID_W - WIN_COLS)
    col_mask = (col[None, :] >= col_start[:, None]) & (col[None, :] < col_start[:, None] + WIN_COLS)
    col_idx = jnp.clip(col[None, :] - col[:, None] + WIN_COLS - 1, 0, 2 * WIN_COLS - 2)
    tab = jnp.where(col_mask[None, None], rpb[:, :, col_idx], NEG)
    masked = jnp.full((C_HEADS, 1, GRID_W, GRID_W), NEG, F32)
    ext = jnp.concatenate([masked, tab, masked], axis=1)
    pairs = jnp.concatenate([ext[:, :-1], ext[:, 1:]], axis=-1)
    return pairs.reshape(C_HEADS // 2, 2, BIAS_SLOTS, GRID_W, LANES)


def kernel(x, c, ctx, c_ctx, norm_g, ada_w, ada_b, ev_w_in, ev_q_norm, ev_w_uq, ev_kv_norm, ev_w_ukv, ev_q_gain,
           ev_k_gain, ev_gate_w, ev_gate_b, ev_gla_norm, ev_w_out, od_w_in, od_q_gain, od_k_gain, od_rpb, od_w_out):
    p = dict(ev_w_in=ev_w_in, ev_q_norm=ev_q_norm, ev_w_uq=ev_w_uq, ev_kv_norm=ev_kv_norm, ev_w_ukv=ev_w_ukv,
             ev_q_gain=ev_q_gain, ev_k_gain=ev_k_gain, ev_gate_w=ev_gate_w, ev_gate_b=ev_gate_b,
             ev_gla_norm=ev_gla_norm, ev_w_out=ev_w_out)
    c16 = jnp.zeros((MOD_ROWS, D), F32).at[:BATCH].set(c).at[BATCH].set(c_ctx)
    mod_all = _modulation(c16, ada_w, ada_b).reshape(DEPTH, MOD_ROWS, 3, D)
    rope = _rope_tables()
    xu = jnp.concatenate([x, ctx], axis=1)
    for i in range(DEPTH):
        j = i // 2
        mod = mod_all[i]
        ng = norm_g[i].reshape(1, D)
        if i % 2 == 0:
            w = _even_weights(j, p, rope)
            q, k, v, za, gq, gk, gv, gf, gb, zb = _even_in(xu, mod, ng, w)
            a = _mla_attn(q, k, v, za)
            of, ob = _gla(gq, gk, gv, gf, gb)
            xu = _even_out(a, of, ob, zb, w["gla_norm"], w["w_out_a"], w["w_out_b"], xu, mod)
        else:
            q, k, v, z = _odd_in(xu, mod, ng, od_w_in[j].astype(BF16), _pad_lanes(jnp.tile(od_q_gain[j], 2)),
                                 _pad_lanes(jnp.tile(od_k_gain[j], 2)))
            o = _nbr_attn(q, k, v, z, _nbr_bias_table(od_rpb[j]))
            n_tiles = N_TILES if i < DEPTH - 1 else CTX_TILE
            xu = _odd_out(o, od_w_out[j].astype(BF16), xu, mod, n_tiles)
    return xu
```

```python
import functools

import jax
import jax.numpy as jnp
from jax import lax
from jax.experimental import pallas as pl
from jax.experimental.pallas import tpu as pltpu

D = 1024
BATCH = 8
SEQ = 2048
DEPTH = 4
GRID_W = 64
GRID_ROWS = SEQ // GRID_W
CTX = 256
T = SEQ + CTX
EPS = 1e-6
ROPE_THETA = 10000.0

A_HEADS = 8
A_NOPE = 64
A_ROPE = 32
A_V = 64
A_QK = A_NOPE + A_ROPE
KV_LORA = 256
Q_LORA = 768
A_WIDTH = A_HEADS * A_V

B_HEADS = 4
B_DK = 64
B_DV = 128
B_KW = B_HEADS * B_DK
B_WIDTH = B_HEADS * B_DV
GATE_RANK = 16
GATE_TAU = 16.0
CHUNK = 64

C_HEADS = 16
C_HD = 64
C_WIDTH = C_HEADS * C_HD
WIN_ROWS = 8
WIN_COLS = 16

LANES = 128
TILE = 256
N_TILES = T // TILE
CTX_TILE = N_TILES - 1
MOD_ROWS = 16
EVEN_COLS = 3200
NEG = -1e30
LOG2E = 1.4426950408889634
VMEM_LIMIT = 56 * 1024 * 1024

F32 = jnp.float32
BF16 = jnp.bfloat16


def _dot(a, b):
    return lax.dot_general(a, b, (((1,), (0,)), ((), ())), preferred_element_type=F32)


def _dot_nt(a, b):
    return lax.dot_general(a, b, (((1,), (1,)), ((), ())), preferred_element_type=F32)


def _dot_tn(a, b):
    return lax.dot_general(a, b, (((0,), (0,)), ((), ())), preferred_element_type=F32)


def _silu(x):
    return x / (1.0 + jnp.exp(-x))


def _rms(x, n):
    return x * lax.rsqrt(jnp.sum(x * x, axis=-1, keepdims=True) * (1.0 / n) + EPS)


def _split3(x):
    hi = x.astype(BF16)
    r1 = x - hi.astype(F32)
    mid = r1.astype(BF16)
    lo = (r1 - mid.astype(F32)).astype(BF16)
    return hi, mid, lo


def _params(sem):
    return pltpu.CompilerParams(dimension_semantics=sem, vmem_limit_bytes=VMEM_LIMIT)


def _const_spec(shape):
    nd = len(shape)
    return pl.BlockSpec(shape, lambda *_: (0,) * nd)


def _mod_index(b, t):
    return (jnp.where(t == CTX_TILE, BATCH, b), 0, 0)


def _mod_kernel(c_ref, w_ref, b_ref, o_ref):
    a = _silu(c_ref[...]).astype(BF16)
    o_ref[0] = _dot(a, w_ref[0].astype(BF16)) + b_ref[0]


def _modulation(c16, ada_w, ada_b):
    return pl.pallas_call(
        _mod_kernel,
        out_shape=jax.ShapeDtypeStruct((DEPTH, MOD_ROWS, 3 * D), F32),
        grid=(DEPTH, 3),
        in_specs=[
            pl.BlockSpec((MOD_ROWS, D), lambda i, j: (0, 0)),
            pl.BlockSpec((1, D, D), lambda i, j: (i, 0, j)),
            pl.BlockSpec((1, 1, D), lambda i, j: (i, 0, j)),
        ],
        out_specs=pl.BlockSpec((1, MOD_ROWS, D), lambda i, j: (i, 0, j)),
        compiler_params=_params(("parallel", "parallel")),
        name="modulation",
    )(c16, ada_w, ada_b.reshape(DEPTH, 1, 3 * D))


def _modulated_norm(x, mod_ref, ng_ref):
    return _rms(x, D) * ng_ref[...] * (1.0 + mod_ref[0, 1:2, :]) + mod_ref[0, 0:1, :]


def _even_in_kernel(x_ref, mod_ref, ng_ref, win_ref, qn_ref, wuq_ref, kvn_ref, wukv_ref, qg_ref, kg_ref,
                    wg_ref, gbias_ref, cos_ref, sina_ref, sinb_ref,
                    q_ref, k_ref, v_ref, za_ref, gq_ref, gk_ref, gv_ref, gf_ref, gb_ref, zb_ref):
    h = _modulated_norm(x_ref[0], mod_ref, ng_ref)
    lat = _dot(h.astype(BF16), win_ref[...])
    q_lat = lat[:, 0:768]
    kv_lat = lat[:, 768:1024]
    za_ref[0] = lat[:, 1024:1536]
    gq_ref[0] = lat[:, 1536:1792] * (B_DK ** -0.5)
    gk_ref[0] = lat[:, 1792:2048]
    gv_ref[0] = lat[:, 2048:2560].astype(BF16)
    zb_ref[0] = lat[:, 2560:3072]
    misc = lat[:, 3072:3200]

    cos = cos_ref[...]
    sina = sina_ref[...]
    sinb = sinb_ref[...]

    def rope(u):
        return u * cos + pltpu.roll(u, LANES - 8, 1) * sina + pltpu.roll(u, 8, 1) * sinb

    qf = _dot((_rms(q_lat, Q_LORA) * qn_ref[...]).astype(BF16), wuq_ref[...])
    kvf = _dot((_rms(kv_lat, KV_LORA) * kvn_ref[...]).astype(BF16), wukv_ref[...])
    lane = lax.broadcasted_iota(jnp.int32, (1, LANES), 1)
    k_rope = jnp.where((lane >= A_NOPE) & (lane < A_QK), misc, 0.0)
    sumsq = lambda u: jnp.sum(u * u, axis=-1, keepdims=True)
    heads = [slice(hh * LANES, (hh + 1) * LANES) for hh in range(A_HEADS)]
    q_roped = [rope(qf[:, sl] * qg_ref[...]) for sl in heads]
    kr_roped = rope(k_rope * kg_ref[...])
    kr_ss = sumsq(k_rope)
    q_scale = [lax.rsqrt(sumsq(qf[:, sl]) * (1.0 / A_QK) + EPS) * (A_QK ** -0.5 * LOG2E) for sl in heads]
    k_scale = [lax.rsqrt((sumsq(kvf[:, sl]) + kr_ss) * (1.0 / A_QK) + EPS) for sl in heads]
    for hh, sl in enumerate(heads):
        q_ref[0, hh] = (q_roped[hh] * q_scale[hh]).astype(BF16)
        k_ref[0, hh] = ((kvf[:, sl] * kg_ref[...] + kr_roped) * k_scale[hh]).astype(BF16)
    v_ref[0] = kvf[:, A_HEADS * LANES:].astype(BF16)

    logits = _dot(misc.astype(BF16), wg_ref[...]) + gbias_ref[...]
    g = (jnp.minimum(logits, 0.0) - jnp.log1p(jnp.exp(-jnp.abs(logits)))) * (1.0 / GATE_TAU)
    gf_ref[0] = g[:, :B_KW]
    gb_ref[0] = g[:, B_KW:]


def _even_in(xu, mod, ng, w):
    tok = lambda width: pl.BlockSpec((1, TILE, width), lambda b, t: (b, t, 0))
    head = pl.BlockSpec((1, A_HEADS, TILE, LANES), lambda b, t: (b, 0, t, 0))
    rope_spec = pl.BlockSpec((TILE, LANES), lambda b, t: (t, 0))
    sd = jax.ShapeDtypeStruct
    return pl.pallas_call(
        _even_in_kernel,
        out_shape=(
            sd((BATCH, A_HEADS, T, LANES), BF16),
            sd((BATCH, A_HEADS, T, LANES), BF16),
            sd((BATCH, T, A_WIDTH), BF16),
            sd((BATCH, T, A_WIDTH), F32),
            sd((BATCH, T, B_KW), F32),
            sd((BATCH, T, B_KW), F32),
            sd((BATCH, T, B_WIDTH), BF16),
            sd((BATCH, T, B_KW), F32),
            sd((BATCH, T, B_KW), F32),
            sd((BATCH, T, B_WIDTH), F32),
        ),
        grid=(BATCH, N_TILES),
        in_specs=[
            tok(D),
            pl.BlockSpec((1, 3, D), _mod_index),
            _const_spec((1, D)),
            _const_spec((D, EVEN_COLS)),
            _const_spec((1, Q_LORA)),
            _const_spec((Q_LORA, A_HEADS * LANES)),
            _const_spec((1, KV_LORA)),
            _const_spec((KV_LORA, A_HEADS * LANES + A_WIDTH)),
            _const_spec((1, LANES)),
            _const_spec((1, LANES)),
            _const_spec((LANES, 2 * B_KW)),
            _const_spec((1, 2 * B_KW)),
            rope_spec, rope_spec, rope_spec,
        ],
        out_specs=(head, head, tok(A_WIDTH), tok(A_WIDTH), tok(B_KW), tok(B_KW), tok(B_WIDTH),
                   tok(B_KW), tok(B_KW), tok(B_WIDTH)),
        compiler_params=_params(("parallel", "parallel")),
        name="even_in",
    )(xu, mod, ng, w["w_in"], w["q_norm"], w["w_uq"], w["kv_norm"], w["w_ukv"], w["q_gain"], w["k_gain"],
      w["w_gate"], w["gate_b"], w["cos"], w["sina"], w["sinb"])


def _values_with_ones(v):
    return jnp.concatenate([v, jnp.ones(v.shape, BF16)], axis=1)


def _normalise(acc):
    return acc[:, :LANES] / acc[:, LANES:]


def _mla_attn_kernel(q_ref, k_ref, v_ref, z_ref, o_ref):
    t = pl.program_id(2)
    first_head = lax.broadcasted_iota(jnp.int32, (1, LANES), 1) < A_V

    def attend(key_groups):
        rowmax = lambda s: functools.reduce(jnp.maximum, [jnp.max(x, axis=-1, keepdims=True) for x in s])
        s0 = [_dot_nt(q_ref[0, 0], k_ref[0, 0, g, :]) for g in key_groups]
        m0 = rowmax(s0)
        s1, p0 = [], []
        for i, g in enumerate(key_groups):
            s1.append(_dot_nt(q_ref[0, 1], k_ref[0, 1, g, :]))
            p0.append(jnp.exp2(s0[i] - m0).astype(BF16))
        m1 = rowmax(s1)
        acc0, p1 = [], []
        for i, g in enumerate(key_groups):
            acc0.append(_dot(p0[i], _values_with_ones(v_ref[0, g, :])))
            p1.append(jnp.exp2(s1[i] - m1).astype(BF16))
        acc1 = [_dot(p1[i], _values_with_ones(v_ref[0, g, :])) for i, g in enumerate(key_groups)]
        o0 = _normalise(functools.reduce(jnp.add, acc0))
        o1 = _normalise(functools.reduce(jnp.add, acc1))
        o_ref[0] = (jnp.where(first_head, o0, o1) * _silu(z_ref[0])).astype(BF16)

    @pl.when(t < CTX_TILE)
    def _():
        attend([slice(0, 768), slice(768, 1536), slice(1536, T)])

    @pl.when(t == CTX_TILE)
    def _():
        attend([slice(SEQ, T)])


def _mla_attn(q, k, v, za):
    return pl.pallas_call(
        _mla_attn_kernel,
        out_shape=jax.ShapeDtypeStruct((BATCH, T, A_WIDTH), BF16),
        grid=(BATCH, A_HEADS // 2, N_TILES),
        in_specs=[
            pl.BlockSpec((1, 2, TILE, LANES), lambda b, p, t: (b, p, t, 0)),
            pl.BlockSpec((1, 2, T, LANES), lambda b, p, t: (b, p, 0, 0)),
            pl.BlockSpec((1, T, LANES), lambda b, p, t: (b, 0, p)),
            pl.BlockSpec((1, TILE, LANES), lambda b, p, t: (b, t, p)),
        ],
        out_specs=pl.BlockSpec((1, TILE, LANES), lambda b, p, t: (b, t, p)),
        compiler_params=_params(("parallel", "parallel", "parallel")),
        name="mla_attn",
    )(q, k, v, za)


def _gla_direction(q_ref, k_ref, v_ref, g_ref, o_ref, s_ref, forward):
    q = q_ref[0]
    k = k_ref[0]
    g = g_ref[0]
    v = v_ref[0]
    row = lax.broadcasted_iota(jnp.int32, (TILE, TILE), 0)
    col = lax.broadcasted_iota(jnp.int32, (TILE, TILE), 1)
    in_chunk_causal = ((row // CHUNK) == (col // CHUNK)) & ((col <= row) if forward else (col >= row))
    tri = in_chunk_causal.astype(BF16)
    g_hi = g.astype(BF16)
    g_lo = (g - g_hi.astype(F32)).astype(BF16)
    b = _dot(tri, g_hi) + _dot(tri, g_lo)

    n_chunks = TILE // CHUNK
    chunks = [slice(c * CHUNK, (c + 1) * CHUNK) for c in range(n_chunks)]
    last = CHUNK - 1 if forward else 0
    b_last = [b[c * CHUNK + last:c * CHUNK + last + 1] for c in range(n_chunks)]
    b_last_rows = jnp.concatenate([jnp.broadcast_to(bl, (CHUNK, LANES)) for bl in b_last], axis=0)
    q_t = q * jnp.exp(b)
    k_t = (k * jnp.exp(-b)).astype(BF16)
    k_dec = (k * jnp.exp(b_last_rows - b)).astype(BF16)
    yield

    lane = lax.broadcasted_iota(jnp.int32, (1, LANES), 1)
    intra = []
    for hh, head_lanes in enumerate((lane < B_DK, lane >= B_DK)):
        qm = jnp.where(head_lanes, q_t, 0.0).astype(BF16)
        att = jnp.where(in_chunk_causal, _dot_nt(qm, k_t), 0.0).astype(BF16)
        intra.append(_dot(att, v[:, hh * B_DV:(hh + 1) * B_DV]))
    yield

    srow = lax.broadcasted_iota(jnp.int32, (2 * B_DK, 2 * B_DV), 0)
    scol = lax.broadcasted_iota(jnp.int32, (2 * B_DK, 2 * B_DV), 1)
    own_head = (srow // B_DK) == (scol // B_DV)
    updates = [jnp.where(own_head, _dot_tn(k_dec[sl], v[sl]), 0.0) for sl in chunks]
    yield
    state = s_ref[...]
    starts = [None] * n_chunks
    for c in (range(n_chunks) if forward else reversed(range(n_chunks))):
        starts[c] = state.astype(BF16)
        decay = jnp.broadcast_to(jnp.exp(b_last[c]), (LANES, LANES)).T
        state = state * jnp.concatenate([decay, decay], axis=1) + updates[c]
    s_ref[...] = state
    yield
    q_tb = q_t.astype(BF16)
    inter = [_dot(q_tb[sl], starts[c]) for c, sl in enumerate(chunks)]
    o_ref[0] = jnp.concatenate(intra, axis=1) + jnp.concatenate(inter, axis=0)


def _gla_kernel(qf_ref, kf_ref, vf_ref, gf_ref, qb_ref, kb_ref, vb_ref, gb_ref, of_ref, ob_ref, sf_ref, sb_ref):
    @pl.when(pl.program_id(2) == 0)
    def _():
        sf_ref[...] = jnp.zeros_like(sf_ref)
        sb_ref[...] = jnp.zeros_like(sb_ref)

    stages = [_gla_direction(qf_ref, kf_ref, vf_ref, gf_ref, of_ref, sf_ref, True),
              _gla_direction(qb_ref, kb_ref, vb_ref, gb_ref, ob_ref, sb_ref, False)]
    while stages:
        stages = [s for s in stages if next(s, StopIteration) is not StopIteration]


def _gla(gq, gk, gv, gf, gb):
    fwd_tile = lambda s: (s + CTX_TILE) % N_TILES
    bwd_tile = lambda s: CTX_TILE - s
    kspec = lambda tile: pl.BlockSpec((1, TILE, LANES), lambda b, p, s: (b, tile(s), p))
    vspec = lambda tile: pl.BlockSpec((1, TILE, 2 * B_DV), lambda b, p, s: (b, tile(s), p))
    out = jax.ShapeDtypeStruct((BATCH, T, B_WIDTH), F32)
    return pl.pallas_call(
        _gla_kernel,
        out_shape=(out, out),
        grid=(BATCH, B_HEADS // 2, N_TILES),
        in_specs=[kspec(fwd_tile), kspec(fwd_tile), vspec(fwd_tile), kspec(fwd_tile),
                  kspec(bwd_tile), kspec(bwd_tile), vspec(bwd_tile), kspec(bwd_tile)],
        out_specs=(vspec(fwd_tile), vspec(bwd_tile)),
        scratch_shapes=[pltpu.VMEM((2 * B_DK, 2 * B_DV), F32), pltpu.VMEM((2 * B_DK, 2 * B_DV), F32)],
        compiler_params=_params(("parallel", "parallel", "arbitrary")),
        name="gla_scan",
    )(gq, gk, gv, gf, gq, gk, gv, gb)


def _even_out_kernel(a_ref, of_ref, ob_ref, zb_ref, gn_ref, wa_ref, wb_ref, x_ref, mod_ref, o_ref):
    bsum = of_ref[0] + ob_ref[0]
    zb = zb_ref[0]
    parts = []
    for hh in range(B_HEADS):
        sl = slice(hh * B_DV, (hh + 1) * B_DV)
        parts.append((_rms(bsum[:, sl], B_DV) * gn_ref[:, sl] * _silu(zb[:, sl])).astype(BF16))
    y = _dot(a_ref[0], wa_ref[...]) + _dot(jnp.concatenate(parts, axis=1), wb_ref[...])
    o_ref[0] = x_ref[0] + mod_ref[0, 2:3, :] * y


def _even_out(a, of, ob, zb, gn, wa, wb, xu, mod):
    tok = lambda width: pl.BlockSpec((1, TILE, width), lambda b, t: (b, t, 0))
    return pl.pallas_call(
        _even_out_kernel,
        out_shape=jax.ShapeDtypeStruct((BATCH, T, D), F32),
        grid=(BATCH, N_TILES),
        in_specs=[tok(A_WIDTH), tok(B_WIDTH), tok(B_WIDTH), tok(B_WIDTH), _const_spec((1, B_WIDTH)),
                  _const_spec((A_WIDTH, D)), _const_spec((B_WIDTH, D)), tok(D),
                  pl.BlockSpec((1, 3, D), _mod_index)],
        out_specs=tok(D),
        compiler_params=_params(("parallel", "parallel")),
        name="even_out",
    )(a, of, ob, zb, gn, wa, wb, xu, mod)


def _odd_in_kernel(x_ref, mod_ref, ng_ref, win_ref, qg_ref, kg_ref, q_ref, k_ref, v_ref, z_ref):
    h = _modulated_norm(x_ref[0], mod_ref, ng_ref)
    u = _dot(h.astype(BF16), win_ref[...])
    grow = lax.broadcasted_iota(jnp.int32, (LANES, LANES), 0) // C_HD
    gcol = lax.broadcasted_iota(jnp.int32, (LANES, LANES), 1) // C_HD
    group = (grow == gcol).astype(BF16)

    def head_norm(xb, gain):
        ss = functools.reduce(jnp.add, [_dot(part, group) for part in _split3(xb * xb)])
        return xb * lax.rsqrt(ss * (1.0 / C_HD) + EPS) * gain

    for j in range(C_WIDTH // LANES):
        sl = slice(j * LANES, (j + 1) * LANES)
        q_ref[0, :, sl] = (head_norm(u[:, sl], qg_ref[...]) * (C_HD ** -0.5 * LOG2E)).astype(BF16)
        ks = slice(C_WIDTH + j * LANES, C_WIDTH + (j + 1) * LANES)
        k_ref[0, :, sl] = head_norm(u[:, ks], kg_ref[...]).astype(BF16)
    v_ref[0] = u[:, 2 * C_WIDTH:3 * C_WIDTH].astype(BF16)
    z_ref[0] = u[:, 3 * C_WIDTH:]


def _odd_in(xu, mod, ng, win, qg, kg):
    tok = lambda width: pl.BlockSpec((1, TILE, width), lambda b, t: (b, t, 0))
    sd = jax.ShapeDtypeStruct
    return pl.pallas_call(
        _odd_in_kernel,
        out_shape=(sd((BATCH, T, C_WIDTH), BF16), sd((BATCH, T, C_WIDTH), BF16), sd((BATCH, T, C_WIDTH), BF16),
                   sd((BATCH, T, C_WIDTH), F32)),
        grid=(BATCH, N_TILES),
        in_specs=[tok(D), pl.BlockSpec((1, 3, D), _mod_index), _const_spec((1, D)), _const_spec((D, 4 * C_WIDTH)),
                  _const_spec((1, LANES)), _const_spec((1, LANES))],
        out_specs=(tok(C_WIDTH), tok(C_WIDTH), tok(C_WIDTH), tok(C_WIDTH)),
        compiler_params=_params(("parallel", "parallel")),
        name="odd_in",
    )(xu, mod, ng, win, qg, kg)


ROWS_PER_TILE = TILE // GRID_W
ROWS_PER_CHAIN = 2
UNION_ROWS = WIN_ROWS + ROWS_PER_CHAIN
UNION_KEYS = UNION_ROWS * GRID_W
BIAS_SLOTS = 2 * WIN_ROWS


def _nbr_attn_kernel(q_ref, k_ref, v_ref, z_ref, bias_ref, o_ref):
    t = pl.program_id(2)
    lane = lax.broadcasted_iota(jnp.int32, (1, LANES), 1)
    first = lane < C_HD
    head_mask = (first, lane >= C_HD)
    ctx = slice(SEQ, T)

    def stack_heads(blocks):
        zero = jnp.zeros(blocks[0].shape, BF16)
        return jnp.concatenate([jnp.where(head_mask[hh], blk, zero) for blk in blocks for hh in range(2)], axis=0)

    def probabilities(s):
        m = jnp.max(s, axis=-1, keepdims=True)
        return jnp.exp2(s - m).astype(BF16)

    def finish(p, v_all, n_blocks, rows, out_rows):
        o = _normalise(_dot(p, _values_with_ones(v_all)))
        o = jnp.concatenate(
            [jnp.where(first, o[(2 * i) * rows:(2 * i + 1) * rows], o[(2 * i + 1) * rows:(2 * i + 2) * rows])
             for i in range(n_blocks)], axis=0)
        o_ref[0, out_rows, :] = (o * _silu(z_ref[0, out_rows, :])).astype(BF16)

    def chain_keys(chain):
        r0 = t * ROWS_PER_TILE + chain * ROWS_PER_CHAIN
        u0 = jnp.clip(r0 - WIN_ROWS // 2, 0, GRID_ROWS - UNION_ROWS)
        return r0, u0, pl.ds(pl.multiple_of(u0 * GRID_W, 2 * GRID_W), UNION_KEYS)

    def raw_scores(chain, k_ctx):
        _, _, win = chain_keys(chain)
        q_rows = [q_ref[0, (chain * ROWS_PER_CHAIN + rr) * GRID_W:(chain * ROWS_PER_CHAIN + rr + 1) * GRID_W, :]
                  for rr in range(ROWS_PER_CHAIN)]
        k_all = jnp.concatenate([k_ref[0, win, :], k_ctx], axis=0)
        return _dot_nt(stack_heads(q_rows), k_all)

    def biased_scores(chain, s):
        r0, u0, _ = chain_keys(chain)
        blocks = []
        for rr in range(ROWS_PER_CHAIN):
            r = r0 + rr
            r_start = jnp.clip(r - WIN_ROWS // 2, 0, GRID_ROWS - WIN_ROWS)
            for hh in range(2):
                blk = s[(2 * rr + hh) * GRID_W:(2 * rr + hh + 1) * GRID_W]
                pieces = []
                for mm in range(UNION_ROWS // 2):
                    u = u0 + 2 * mm
                    ok_l = ((u >= r_start) & (u < r_start + WIN_ROWS)).astype(jnp.int32)
                    ok_r = ((u + 1 >= r_start) & (u + 1 < r_start + WIN_ROWS)).astype(jnp.int32)
                    slot = jnp.clip(u - r + WIN_ROWS, 0, BIAS_SLOTS - 1)
                    in_window = jnp.where(first, ok_l, ok_r) > 0
                    scores = blk[:, mm * LANES:(mm + 1) * LANES] + bias_ref[0, hh, slot]
                    pieces.append(jnp.where(in_window, scores, NEG))
                pieces.append(blk[:, UNION_KEYS:])
                blocks.append(jnp.concatenate(pieces, axis=1))
        return jnp.concatenate(blocks, axis=0)

    @pl.when(t < CTX_TILE)
    def _():
        k_ctx = k_ref[0, ctx, :]
        v_ctx = v_ref[0, ctx, :]
        n_chains = ROWS_PER_TILE // ROWS_PER_CHAIN
        raw = [raw_scores(chain, k_ctx) for chain in range(n_chains)]
        for chain in range(n_chains):
            p = probabilities(biased_scores(chain, raw[chain]))
            v_all = jnp.concatenate([v_ref[0, chain_keys(chain)[2], :], v_ctx], axis=0)
            chain_rows = slice(chain * ROWS_PER_CHAIN * GRID_W, (chain + 1) * ROWS_PER_CHAIN * GRID_W)
            finish(p, v_all, ROWS_PER_CHAIN, GRID_W, chain_rows)

    @pl.when(t == CTX_TILE)
    def _():
        p = probabilities(_dot_nt(stack_heads([q_ref[0]]), k_ref[0, ctx, :]))
        finish(p, v_ref[0, ctx, :], 1, TILE, slice(0, TILE))


def _nbr_attn(q, k, v, z, bias):
    pair = lambda rows: pl.BlockSpec((1, rows, LANES), lambda b, p, t: (b, 0 if rows == T else t, p))
    return pl.pallas_call(
        _nbr_attn_kernel,
        out_shape=jax.ShapeDtypeStruct((BATCH, T, C_WIDTH), BF16),
        grid=(BATCH, C_HEADS // 2, N_TILES),
        in_specs=[pair(TILE), pair(T), pair(T), pair(TILE),
                  pl.BlockSpec((1, 2, BIAS_SLOTS, GRID_W, LANES), lambda b, p, t: (p, 0, 0, 0, 0))],
        out_specs=pair(TILE),
        compiler_params=_params(("parallel", "parallel", "parallel")),
        name="nbr_attn",
    )(q, k, v, z, bias)


def _odd_out_kernel(o_ref, w_ref, x_ref, mod_ref, y_ref):
    y_ref[0] = x_ref[0] + mod_ref[0, 2:3, :] * _dot(o_ref[0], w_ref[...])


def _odd_out(o, w, xu, mod, n_tiles):
    tok = lambda width: pl.BlockSpec((1, TILE, width), lambda b, t: (b, t, 0))
    return pl.pallas_call(
        _odd_out_kernel,
        out_shape=jax.ShapeDtypeStruct((BATCH, n_tiles * TILE, D), F32),
        grid=(BATCH, n_tiles),
        in_specs=[tok(C_WIDTH), _const_spec((C_WIDTH, D)), tok(D), pl.BlockSpec((1, 3, D), _mod_index)],
        out_specs=tok(D),
        compiler_params=_params(("parallel", "parallel")),
        name="odd_out",
    )(o, w, xu, mod)


def _rope_tables():
    n = jnp.arange(SEQ)
    pos = jnp.stack([(n // GRID_W).astype(F32), (n % GRID_W).astype(F32)], axis=1)
    d_ax = A_ROPE // 2
    inv = ROPE_THETA ** (-jnp.arange(0, d_ax, 2, dtype=F32) / d_ax)
    ang = pos[:, :, None] * inv
    cos, sin, zero = jnp.cos(ang), jnp.sin(ang), jnp.zeros_like(ang)
    lay = lambda first, second: jnp.stack([first, second], axis=2).reshape(SEQ, A_ROPE)

    def place(tab, fill):
        full = jnp.full((T, LANES), fill, F32)
        return full.at[:SEQ, A_NOPE:A_QK].set(tab)

    return place(lay(cos, cos), 1.0), place(lay(-sin, zero), 0.0), place(lay(zero, sin), 0.0)


def _pad_lanes(v, width=LANES):
    return jnp.pad(v, (0, width - v.shape[0])).reshape(1, width)


def _even_weights(j, p, rope):
    w_in = p["ev_w_in"][j]
    q_l, kv_l, k_r, z_a, q_b, k_b, v_b, lr, z_b = jnp.split(
        w_in, [768, 1024, 1056, 1568, 1824, 2080, 2592, 2624], axis=1)
    zeros = jnp.zeros((D, 32), F32)
    w_in2 = jnp.concatenate([q_l, kv_l, z_a, q_b, k_b, v_b, z_b, lr, zeros, k_r, zeros], axis=1).astype(BF16)
    w_uq = jnp.pad(p["ev_w_uq"][j].reshape(Q_LORA, A_HEADS, A_QK), ((0, 0), (0, 0), (0, LANES - A_QK)))
    w_ukv = p["ev_w_ukv"][j].reshape(KV_LORA, A_HEADS, A_NOPE + A_V)
    w_uk = jnp.pad(w_ukv[:, :, :A_NOPE], ((0, 0), (0, 0), (0, LANES - A_NOPE))).reshape(KV_LORA, A_HEADS * LANES)
    w_uv = w_ukv[:, :, A_NOPE:].reshape(KV_LORA, A_WIDTH)
    gate_w = p["ev_gate_w"][j]
    w_gate = jnp.zeros((LANES, 2 * B_KW), F32)
    w_gate = w_gate.at[:GATE_RANK, :B_KW].set(gate_w[0]).at[GATE_RANK:2 * GATE_RANK, B_KW:].set(gate_w[1])
    w_out = p["ev_w_out"][j].astype(BF16)
    return {
        "w_in": w_in2,
        "q_norm": p["ev_q_norm"][j].reshape(1, Q_LORA),
        "w_uq": w_uq.reshape(Q_LORA, A_HEADS * LANES).astype(BF16),
        "kv_norm": p["ev_kv_norm"][j].reshape(1, KV_LORA),
        "w_ukv": jnp.concatenate([w_uk, w_uv], axis=1).astype(BF16),
        "q_gain": _pad_lanes(p["ev_q_gain"][j]),
        "k_gain": _pad_lanes(p["ev_k_gain"][j]),
        "w_gate": w_gate.astype(BF16),
        "gate_b": p["ev_gate_b"][j].reshape(1, 2 * B_KW),
        "cos": rope[0], "sina": rope[1], "sinb": rope[2],
        "gla_norm": p["ev_gla_norm"][j].reshape(1, B_WIDTH),
        "w_out_a": w_out[:A_WIDTH],
        "w_out_b": w_out[A_WIDTH:],
    }


def _nbr_bias_table(rpb):
    col = jnp.arange(GRID_W)
    col_start = jnp.clip(col - WIN_COLS // 2, 0, GRID_W - WIN_COLS)
    col_mask = (col[None, :] >= col_start[:, None]) & (col[None, :] < col_start[:, None] + WIN_COLS)
    col_idx = jnp.clip(col[None, :] - col[:, None] + WIN_COLS - 1, 0, 2 * WIN_COLS - 2)
    tab = jnp.where(col_mask[None, None], rpb[:, :, col_idx] * LOG2E, NEG)
    masked = jnp.full((C_HEADS, 1, GRID_W, GRID_W), NEG, F32)
    ext = jnp.concatenate([masked, tab, masked], axis=1)
    pairs = jnp.concatenate([ext[:, :-1], ext[:, 1:]], axis=-1)
    return pairs.reshape(C_HEADS // 2, 2, BIAS_SLOTS, GRID_W, LANES)


def kernel(x, c, ctx, c_ctx, norm_g, ada_w, ada_b, ev_w_in, ev_q_norm, ev_w_uq, ev_kv_norm, ev_w_ukv, ev_q_gain,
           ev_k_gain, ev_gate_w, ev_gate_b, ev_gla_norm, ev_w_out, od_w_in, od_q_gain, od_k_gain, od_rpb, od_w_out):
    p = dict(ev_w_in=ev_w_in, ev_q_norm=ev_q_norm, ev_w_uq=ev_w_uq, ev_kv_norm=ev_kv_norm, ev_w_ukv=ev_w_ukv,
             ev_q_gain=ev_q_gain, ev_k_gain=ev_k_gain, ev_gate_w=ev_gate_w, ev_gate_b=ev_gate_b,
             ev_gla_norm=ev_gla_norm, ev_w_out=ev_w_out)
    c16 = jnp.zeros((MOD_ROWS, D), F32).at[:BATCH].set(c).at[BATCH].set(c_ctx)
    mod_all = _modulation(c16, ada_w, ada_b).reshape(DEPTH, MOD_ROWS, 3, D)
    rope = _rope_tables()
    xu = jnp.concatenate([x, ctx], axis=1)
    for i in range(DEPTH):
        j = i // 2
        mod = mod_all[i]
        ng = norm_g[i].reshape(1, D)
        if i % 2 == 0:
            w = _even_weights(j, p, rope)
            q, k, v, za, gq, gk, gv, gf, gb, zb = _even_in(xu, mod, ng, w)
            a = _mla_attn(q, k, v, za)
            of, ob = _gla(gq, gk, gv, gf, gb)
            xu = _even_out(a, of, ob, zb, w["gla_norm"], w["w_out_a"], w["w_out_b"], xu, mod)
        else:
            q, k, v, z = _odd_in(xu, mod, ng, od_w_in[j].astype(BF16), _pad_lanes(jnp.tile(od_q_gain[j], 2)),
                                 _pad_lanes(jnp.tile(od_k_gain[j], 2)))
            o = _nbr_attn(q, k, v, z, _nbr_bias_table(od_rpb[j]))
            n_tiles = N_TILES if i < DEPTH - 1 else CTX_TILE
            xu = _odd_out(o, od_w_out[j].astype(BF16), xu, mod, n_tiles)
    return xu
```

```python
import functools

import jax
import jax.numpy as jnp
import numpy as np
from jax import lax
from jax.experimental import pallas as pl
from jax.experimental.pallas import tpu as pltpu

D = 1024
BATCH = 8
SEQ = 2048
DEPTH = 4
GRID_W = 64
GRID_ROWS = SEQ // GRID_W
CTX = 256
T = SEQ + CTX
EPS = 1e-6
ROPE_THETA = 10000.0

A_HEADS = 8
A_NOPE = 64
A_ROPE = 32
A_V = 64
A_QK = A_NOPE + A_ROPE
KV_LORA = 256
Q_LORA = 768
A_WIDTH = A_HEADS * A_V

B_HEADS = 4
B_DK = 64
B_DV = 128
B_KW = B_HEADS * B_DK
B_WIDTH = B_HEADS * B_DV
GATE_RANK = 16
GATE_TAU = 16.0
CHUNK = 64

C_HEADS = 16
C_HD = 64
C_WIDTH = C_HEADS * C_HD
WIN_ROWS = 8
WIN_COLS = 16

LANES = 128
TILE = 256
N_TILES = T // TILE
CTX_TILE = N_TILES - 1
MOD_ROWS = 16
EVEN_COLS = 3200
NEG = -1e30
LOG2E = 1.4426950408889634
VMEM_LIMIT = 56 * 1024 * 1024

F32 = jnp.float32
BF16 = jnp.bfloat16


def _dot(a, b):
    return lax.dot_general(a, b, (((1,), (0,)), ((), ())), preferred_element_type=F32)


def _dot_nt(a, b):
    return lax.dot_general(a, b, (((1,), (1,)), ((), ())), preferred_element_type=F32)


def _dot_tn(a, b):
    return lax.dot_general(a, b, (((0,), (0,)), ((), ())), preferred_element_type=F32)


def _silu(x):
    return x / (1.0 + jnp.exp(-x))


def _rms(x, n):
    return x * lax.rsqrt(jnp.sum(x * x, axis=-1, keepdims=True) * (1.0 / n) + EPS)


def _params(sem):
    return pltpu.CompilerParams(dimension_semantics=sem, vmem_limit_bytes=VMEM_LIMIT)


def _const_spec(shape):
    nd = len(shape)
    return pl.BlockSpec(shape, lambda *_: (0,) * nd)


def _mod_index(b, t):
    return (jnp.where(t == CTX_TILE, BATCH, b), 0, 0)


def _mod_kernel(c_ref, w_ref, b_ref, o_ref):
    a = _silu(c_ref[...]).astype(BF16)
    o_ref[0] = _dot(a, w_ref[0].astype(BF16)) + b_ref[0]


def _modulation(c16, ada_w, ada_b):
    return pl.pallas_call(
        _mod_kernel,
        out_shape=jax.ShapeDtypeStruct((DEPTH, MOD_ROWS, 3 * D), F32),
        grid=(DEPTH, 3),
        in_specs=[
            pl.BlockSpec((MOD_ROWS, D), lambda i, j: (0, 0)),
            pl.BlockSpec((1, D, D), lambda i, j: (i, 0, j)),
            pl.BlockSpec((1, 1, D), lambda i, j: (i, 0, j)),
        ],
        out_specs=pl.BlockSpec((1, MOD_ROWS, D), lambda i, j: (i, 0, j)),
        compiler_params=_params(("parallel", "parallel")),
        name="modulation",
    )(c16, ada_w, ada_b.reshape(DEPTH, 1, 3 * D))


def _modulated_norm(x, mod_ref, ng_ref):
    return _rms(x, D) * ng_ref[...] * (1.0 + mod_ref[0, 1:2, :]) + mod_ref[0, 0:1, :]


def _group_sumsq(u, group, counted):
    sq = u * u
    hi = sq.astype(BF16)
    lo = (sq - hi.astype(F32)).astype(BF16)
    width = 2 * LANES
    row = lax.broadcasted_iota(jnp.int32, (width, width), 0)
    col = lax.broadcasted_iota(jnp.int32, (width, width), 1)
    ones = ((row // group == col // group) & (row % group < counted)).astype(BF16)
    cols = [slice(c, c + width) for c in range(0, u.shape[1], width)]
    return jnp.concatenate([_dot(hi[:, c], ones) + _dot(lo[:, c], ones) for c in cols], axis=1)


def _slot_sumsq(u):
    return _group_sumsq(u, LANES, A_QK)


def _even_in_kernel(x_ref, mod_ref, ng_ref, win_ref, qn_ref, wuq_ref, kvn_ref, wukv_ref, qg_ref, kg_ref,
                    wg_ref, gbias_ref, cos_ref, sin_ref,
                    q_ref, k_ref, v_ref, za_ref, gq_ref, gk_ref, gv_ref, gf_ref, gb_ref, zb_ref):
    h = _modulated_norm(x_ref[0], mod_ref, ng_ref)
    lat = _dot(h.astype(BF16), win_ref[...])
    q_lat = lat[:, 0:768]
    kv_lat = lat[:, 768:1024]
    za_ref[0] = lat[:, 1024:1536]
    gq_ref[0] = lat[:, 1536:1792] * (B_DK ** -0.5)
    gk_ref[0] = lat[:, 1792:2048]
    gv_ref[0] = lat[:, 2048:2560].astype(BF16)
    zb_ref[0] = lat[:, 2560:3072]
    misc = lat[:, 3072:3200]

    cos = cos_ref[...]
    sin = sin_ref[...]

    def rope(u):
        return u * cos + pltpu.roll(u, LANES - HALF, 1) * sin

    qf = _dot((_rms(q_lat, Q_LORA) * qn_ref[...]).astype(BF16), wuq_ref[...])
    kvf = _dot((_rms(kv_lat, KV_LORA) * kvn_ref[...]).astype(BF16), wukv_ref[...])
    lane = lax.broadcasted_iota(jnp.int32, (1, LANES), 1)
    rotary = (lane >= A_NOPE) & (lane < A_QK)
    kr_roped = rope(jnp.where(lane >= A_NOPE, misc, 0.0) * kg_ref[...])
    k_rope = jnp.where(rotary, misc, 0.0)
    kr_ss = jnp.sum(k_rope * k_rope, axis=-1, keepdims=True)
    q_ss = _slot_sumsq(qf)
    k_ss = _slot_sumsq(kvf[:, :A_HEADS * LANES])
    for hh in range(A_HEADS):
        sl = slice(hh * LANES, (hh + 1) * LANES)
        q_scale = lax.rsqrt(q_ss[:, sl] * (1.0 / A_QK) + EPS) * (A_QK ** -0.5 * LOG2E)
        q_ref[0, hh] = (rope(qf[:, sl] * qg_ref[...]) * q_scale).astype(BF16)
        k_scale = lax.rsqrt((k_ss[:, sl] + kr_ss) * (1.0 / A_QK) + EPS)
        k_ref[0, hh] = ((kvf[:, sl] * kg_ref[...] + kr_roped) * k_scale).astype(BF16)
    v_ref[0] = kvf[:, A_HEADS * LANES:].astype(BF16)

    logits = _dot(misc.astype(BF16), wg_ref[...]) + gbias_ref[...]
    g = (jnp.minimum(logits, 0.0) - jnp.log1p(jnp.exp(-jnp.abs(logits)))) * (1.0 / GATE_TAU)
    gf_ref[0] = g[:, :B_KW]
    gb_ref[0] = g[:, B_KW:]


def _even_in(xu, mod, ng, w):
    tok = lambda width: pl.BlockSpec((1, TILE, width), lambda b, t: (b, t, 0))
    head = pl.BlockSpec((1, A_HEADS, TILE, LANES), lambda b, t: (b, 0, t, 0))
    rope_spec = pl.BlockSpec((TILE, LANES), lambda b, t: (t, 0))
    sd = jax.ShapeDtypeStruct
    return pl.pallas_call(
        _even_in_kernel,
        out_shape=(
            sd((BATCH, A_HEADS, T, LANES), BF16),
            sd((BATCH, A_HEADS, T, LANES), BF16),
            sd((BATCH, T, A_WIDTH), BF16),
            sd((BATCH, T, A_WIDTH), F32),
            sd((BATCH, T, B_KW), F32),
            sd((BATCH, T, B_KW), F32),
            sd((BATCH, T, B_WIDTH), BF16),
            sd((BATCH, T, B_KW), F32),
            sd((BATCH, T, B_KW), F32),
            sd((BATCH, T, B_WIDTH), F32),
        ),
        grid=(BATCH, N_TILES),
        in_specs=[
            tok(D),
            pl.BlockSpec((1, 3, D), _mod_index),
            _const_spec((1, D)),
            _const_spec((D, EVEN_COLS)),
            _const_spec((1, Q_LORA)),
            _const_spec((Q_LORA, A_HEADS * LANES)),
            _const_spec((1, KV_LORA)),
            _const_spec((KV_LORA, A_HEADS * LANES + A_WIDTH)),
            _const_spec((1, LANES)),
            _const_spec((1, LANES)),
            _const_spec((LANES, 2 * B_KW)),
            _const_spec((1, 2 * B_KW)),
            rope_spec, rope_spec,
        ],
        out_specs=(head, head, tok(A_WIDTH), tok(A_WIDTH), tok(B_KW), tok(B_KW), tok(B_WIDTH),
                   tok(B_KW), tok(B_KW), tok(B_WIDTH)),
        compiler_params=_params(("parallel", "parallel")),
        name="even_in",
    )(xu, mod, ng, w["w_in"], w["q_norm"], w["w_uq"], w["kv_norm"], w["w_ukv"], w["q_gain"], w["k_gain"],
      w["w_gate"], w["gate_b"], w["cos"], w["sin"])


def _values_with_ones(v):
    return jnp.concatenate([v, jnp.ones(v.shape, BF16)], axis=1)


def _normalise(acc):
    return acc[:, :LANES] / acc[:, LANES:]


def _mla_attn_kernel(q_ref, k_ref, v_ref, z_ref, o_ref):
    t = pl.program_id(2)
    first_head = lax.broadcasted_iota(jnp.int32, (1, LANES), 1) < A_V

    def attend(key_groups):
        rowmax = lambda s: functools.reduce(jnp.maximum, [jnp.max(x, axis=-1, keepdims=True) for x in s])
        s0 = [_dot_nt(q_ref[0, 0], k_ref[0, 0, g, :]) for g in key_groups]
        m0 = rowmax(s0)
        s1, p0 = [], []
        for i, g in enumerate(key_groups):
            s1.append(_dot_nt(q_ref[0, 1], k_ref[0, 1, g, :]))
            p0.append(jnp.exp2(s0[i] - m0).astype(BF16))
        m1 = rowmax(s1)
        acc0, p1 = [], []
        for i, g in enumerate(key_groups):
            acc0.append(_dot(p0[i], _values_with_ones(v_ref[0, g, :])))
            p1.append(jnp.exp2(s1[i] - m1).astype(BF16))
        acc1 = [_dot(p1[i], _values_with_ones(v_ref[0, g, :])) for i, g in enumerate(key_groups)]
        o0 = _normalise(functools.reduce(jnp.add, acc0))
        o1 = _normalise(functools.reduce(jnp.add, acc1))
        o_ref[0] = (jnp.where(first_head, o0, o1) * _silu(z_ref[0])).astype(BF16)

    @pl.when(t < CTX_TILE)
    def _():
        attend([slice(0, 768), slice(768, 1536), slice(1536, T)])

    @pl.when(t == CTX_TILE)
    def _():
        attend([slice(SEQ, T)])


def _mla_attn(q, k, v, za):
    return pl.pallas_call(
        _mla_attn_kernel,
        out_shape=jax.ShapeDtypeStruct((BATCH, T, A_WIDTH), BF16),
        grid=(BATCH, A_HEADS // 2, N_TILES),
        in_specs=[
            pl.BlockSpec((1, 2, TILE, LANES), lambda b, p, t: (b, p, t, 0)),
            pl.BlockSpec((1, 2, T, LANES), lambda b, p, t: (b, p, 0, 0)),
            pl.BlockSpec((1, T, LANES), lambda b, p, t: (b, 0, p)),
            pl.BlockSpec((1, TILE, LANES), lambda b, p, t: (b, t, p)),
        ],
        out_specs=pl.BlockSpec((1, TILE, LANES), lambda b, p, t: (b, t, p)),
        compiler_params=_params(("parallel", "parallel", "parallel")),
        name="mla_attn",
    )(q, k, v, za)


def _gla_direction(q_ref, k_ref, v_ref, g_ref, o_ref, s_ref, forward):
    q = q_ref[0]
    k = k_ref[0]
    g = g_ref[0]
    v = v_ref[0]
    row = lax.broadcasted_iota(jnp.int32, (TILE, TILE), 0)
    col = lax.broadcasted_iota(jnp.int32, (TILE, TILE), 1)
    in_chunk_causal = ((row // CHUNK) == (col // CHUNK)) & ((col <= row) if forward else (col >= row))
    tri = in_chunk_causal.astype(BF16)
    g_hi = g.astype(BF16)
    g_lo = (g - g_hi.astype(F32)).astype(BF16)
    b = _dot(tri, g_hi) + _dot(tri, g_lo)

    n_chunks = TILE // CHUNK
    chunks = [slice(c * CHUNK, (c + 1) * CHUNK) for c in range(n_chunks)]
    last = CHUNK - 1 if forward else 0
    b_last = [b[c * CHUNK + last:c * CHUNK + last + 1] for c in range(n_chunks)]
    b_last_rows = jnp.concatenate([jnp.broadcast_to(bl, (CHUNK, LANES)) for bl in b_last], axis=0)
    q_t = q * jnp.exp(b)
    k_t = (k * jnp.exp(-b)).astype(BF16)
    k_dec = (k * jnp.exp(b_last_rows - b)).astype(BF16)
    yield

    lane = lax.broadcasted_iota(jnp.int32, (1, LANES), 1)
    intra = []
    for hh, head_lanes in enumerate((lane < B_DK, lane >= B_DK)):
        qm = jnp.where(head_lanes, q_t, 0.0).astype(BF16)
        att = jnp.where(in_chunk_causal, _dot_nt(qm, k_t), 0.0).astype(BF16)
        intra.append(_dot(att, v[:, hh * B_DV:(hh + 1) * B_DV]))
    yield

    srow = lax.broadcasted_iota(jnp.int32, (2 * B_DK, 2 * B_DV), 0)
    scol = lax.broadcasted_iota(jnp.int32, (2 * B_DK, 2 * B_DV), 1)
    own_head = (srow // B_DK) == (scol // B_DV)
    updates = [jnp.where(own_head, _dot_tn(k_dec[sl], v[sl]), 0.0) for sl in chunks]
    yield
    state = s_ref[...]
    starts = [None] * n_chunks
    for c in (range(n_chunks) if forward else reversed(range(n_chunks))):
        starts[c] = state.astype(BF16)
        decay = jnp.broadcast_to(jnp.exp(b_last[c]), (LANES, LANES)).T
        state = state * jnp.concatenate([decay, decay], axis=1) + updates[c]
    s_ref[...] = state
    yield
    q_tb = q_t.astype(BF16)
    inter = [_dot(q_tb[sl], starts[c]) for c, sl in enumerate(chunks)]
    o_ref[0] = jnp.concatenate(intra, axis=1) + jnp.concatenate(inter, axis=0)


def _gla_kernel(qf_ref, kf_ref, vf_ref, gf_ref, qb_ref, kb_ref, vb_ref, gb_ref, of_ref, ob_ref, sf_ref, sb_ref):
    @pl.when(pl.program_id(2) == 0)
    def _():
        sf_ref[...] = jnp.zeros_like(sf_ref)
        sb_ref[...] = jnp.zeros_like(sb_ref)

    stages = [_gla_direction(qf_ref, kf_ref, vf_ref, gf_ref, of_ref, sf_ref, True),
              _gla_direction(qb_ref, kb_ref, vb_ref, gb_ref, ob_ref, sb_ref, False)]
    while stages:
        stages = [s for s in stages if next(s, StopIteration) is not StopIteration]


def _gla(gq, gk, gv, gf, gb):
    fwd_tile = lambda s: (s + CTX_TILE) % N_TILES
    bwd_tile = lambda s: CTX_TILE - s
    kspec = lambda tile: pl.BlockSpec((1, TILE, LANES), lambda b, p, s: (b, tile(s), p))
    vspec = lambda tile: pl.BlockSpec((1, TILE, 2 * B_DV), lambda b, p, s: (b, tile(s), p))
    out = jax.ShapeDtypeStruct((BATCH, T, B_WIDTH), F32)
    return pl.pallas_call(
        _gla_kernel,
        out_shape=(out, out),
        grid=(BATCH, B_HEADS // 2, N_TILES),
        in_specs=[kspec(fwd_tile), kspec(fwd_tile), vspec(fwd_tile), kspec(fwd_tile),
                  kspec(bwd_tile), kspec(bwd_tile), vspec(bwd_tile), kspec(bwd_tile)],
        out_specs=(vspec(fwd_tile), vspec(bwd_tile)),
        scratch_shapes=[pltpu.VMEM((2 * B_DK, 2 * B_DV), F32), pltpu.VMEM((2 * B_DK, 2 * B_DV), F32)],
        compiler_params=_params(("parallel", "parallel", "arbitrary")),
        name="gla_scan",
    )(gq, gk, gv, gf, gq, gk, gv, gb)


def _even_out_kernel(a_ref, of_ref, ob_ref, zb_ref, gn_ref, wa_ref, wb_ref, x_ref, mod_ref, o_ref):
    bsum = of_ref[0] + ob_ref[0]
    zb = zb_ref[0]
    parts = []
    for hh in range(B_HEADS):
        sl = slice(hh * B_DV, (hh + 1) * B_DV)
        parts.append((_rms(bsum[:, sl], B_DV) * gn_ref[:, sl] * _silu(zb[:, sl])).astype(BF16))
    y = _dot(a_ref[0], wa_ref[...]) + _dot(jnp.concatenate(parts, axis=1), wb_ref[...])
    o_ref[0] = x_ref[0] + mod_ref[0, 2:3, :] * y


def _even_out(a, of, ob, zb, gn, wa, wb, xu, mod):
    tok = lambda width: pl.BlockSpec((1, TILE, width), lambda b, t: (b, t, 0))
    return pl.pallas_call(
        _even_out_kernel,
        out_shape=jax.ShapeDtypeStruct((BATCH, T, D), F32),
        grid=(BATCH, N_TILES),
        in_specs=[tok(A_WIDTH), tok(B_WIDTH), tok(B_WIDTH), tok(B_WIDTH), _const_spec((1, B_WIDTH)),
                  _const_spec((A_WIDTH, D)), _const_spec((B_WIDTH, D)), tok(D),
                  pl.BlockSpec((1, 3, D), _mod_index)],
        out_specs=tok(D),
        compiler_params=_params(("parallel", "parallel")),
        name="even_out",
    )(a, of, ob, zb, gn, wa, wb, xu, mod)


def _odd_in_kernel(x_ref, mod_ref, ng_ref, win_ref, qg_ref, kg_ref, q_ref, k_ref, v_ref, z_ref):
    h = _modulated_norm(x_ref[0], mod_ref, ng_ref)
    u = _dot(h.astype(BF16), win_ref[...])
    ss = _group_sumsq(u[:, :2 * C_WIDTH], C_HD, C_HD)
    scale = lax.rsqrt(ss * (1.0 / C_HD) + EPS)
    for j in range(C_WIDTH // LANES):
        sl = slice(j * LANES, (j + 1) * LANES)
        q_ref[0, :, sl] = (u[:, sl] * scale[:, sl] * qg_ref[...] * (C_HD ** -0.5 * LOG2E)).astype(BF16)
        ks = slice(C_WIDTH + j * LANES, C_WIDTH + (j + 1) * LANES)
        k_ref[0, :, sl] = (u[:, ks] * scale[:, ks] * kg_ref[...]).astype(BF16)
    v_ref[0] = u[:, 2 * C_WIDTH:3 * C_WIDTH].astype(BF16)
    z_ref[0] = u[:, 3 * C_WIDTH:]


def _odd_in(xu, mod, ng, win, qg, kg):
    tok = lambda width: pl.BlockSpec((1, TILE, width), lambda b, t: (b, t, 0))
    sd = jax.ShapeDtypeStruct
    return pl.pallas_call(
        _odd_in_kernel,
        out_shape=(sd((BATCH, T, C_WIDTH), BF16), sd((BATCH, T, C_WIDTH), BF16), sd((BATCH, T, C_WIDTH), BF16),
                   sd((BATCH, T, C_WIDTH), F32)),
        grid=(BATCH, N_TILES),
        in_specs=[tok(D), pl.BlockSpec((1, 3, D), _mod_index), _const_spec((1, D)), _const_spec((D, 4 * C_WIDTH)),
                  _const_spec((1, LANES)), _const_spec((1, LANES))],
        out_specs=(tok(C_WIDTH), tok(C_WIDTH), tok(C_WIDTH), tok(C_WIDTH)),
        compiler_params=_params(("parallel", "parallel")),
        name="odd_in",
    )(xu, mod, ng, win, qg, kg)


ROWS_PER_TILE = TILE // GRID_W
ROWS_PER_CHAIN = 2
UNION_ROWS = WIN_ROWS + ROWS_PER_CHAIN
UNION_KEYS = UNION_ROWS * GRID_W
BIAS_SLOTS = 2 * WIN_ROWS


def _nbr_attn_kernel(q_ref, k_ref, v_ref, z_ref, bias_ref, o_ref):
    t = pl.program_id(2)
    lane = lax.broadcasted_iota(jnp.int32, (1, LANES), 1)
    first = lane < C_HD
    head_mask = (first, lane >= C_HD)
    ctx = slice(SEQ, T)

    def stack_heads(blocks):
        zero = jnp.zeros(blocks[0].shape, BF16)
        return jnp.concatenate([jnp.where(head_mask[hh], blk, zero) for blk in blocks for hh in range(2)], axis=0)

    def probabilities(s):
        m = jnp.max(s, axis=-1, keepdims=True)
        return jnp.exp2(s - m).astype(BF16)

    def finish(p, v_all, n_blocks, rows, out_rows):
        o = _normalise(_dot(p, _values_with_ones(v_all)))
        o = jnp.concatenate(
            [jnp.where(first, o[(2 * i) * rows:(2 * i + 1) * rows], o[(2 * i + 1) * rows:(2 * i + 2) * rows])
             for i in range(n_blocks)], axis=0)
        o_ref[0, out_rows, :] = (o * _silu(z_ref[0, out_rows, :])).astype(BF16)

    def chain_keys(chain):
        r0 = t * ROWS_PER_TILE + chain * ROWS_PER_CHAIN
        u0 = jnp.clip(r0 - WIN_ROWS // 2, 0, GRID_ROWS - UNION_ROWS)
        return r0, u0, pl.ds(pl.multiple_of(u0 * GRID_W, 2 * GRID_W), UNION_KEYS)

    def raw_scores(chain, k_ctx):
        _, _, win = chain_keys(chain)
        q_rows = [q_ref[0, (chain * ROWS_PER_CHAIN + rr) * GRID_W:(chain * ROWS_PER_CHAIN + rr + 1) * GRID_W, :]
                  for rr in range(ROWS_PER_CHAIN)]
        k_all = jnp.concatenate([k_ref[0, win, :], k_ctx], axis=0)
        return _dot_nt(stack_heads(q_rows), k_all)

    def biased_scores(chain, s):
        r0, u0, _ = chain_keys(chain)
        blocks = []
        for rr in range(ROWS_PER_CHAIN):
            r = r0 + rr
            r_start = jnp.clip(r - WIN_ROWS // 2, 0, GRID_ROWS - WIN_ROWS)
            for hh in range(2):
                blk = s[(2 * rr + hh) * GRID_W:(2 * rr + hh + 1) * GRID_W]
                pieces = []
                for mm in range(UNION_ROWS // 2):
                    u = u0 + 2 * mm
                    ok_l = ((u >= r_start) & (u < r_start + WIN_ROWS)).astype(jnp.int32)
                    ok_r = ((u + 1 >= r_start) & (u + 1 < r_start + WIN_ROWS)).astype(jnp.int32)
                    slot = jnp.clip(u - r + WIN_ROWS, 0, BIAS_SLOTS - 1)
                    in_window = jnp.where(first, ok_l, ok_r) > 0
                    scores = blk[:, mm * LANES:(mm + 1) * LANES] + bias_ref[0, hh, slot]
                    pieces.append(jnp.where(in_window, scores, NEG))
                pieces.append(blk[:, UNION_KEYS:])
                blocks.append(jnp.concatenate(pieces, axis=1))
        return jnp.concatenate(blocks, axis=0)

    @pl.when(t < CTX_TILE)
    def _():
        k_ctx = k_ref[0, ctx, :]
        v_ctx = v_ref[0, ctx, :]
        n_chains = ROWS_PER_TILE // ROWS_PER_CHAIN
        raw = [raw_scores(chain, k_ctx) for chain in range(n_chains)]
        for chain in range(n_chains):
            p = probabilities(biased_scores(chain, raw[chain]))
            v_all = jnp.concatenate([v_ref[0, chain_keys(chain)[2], :], v_ctx], axis=0)
            chain_rows = slice(chain * ROWS_PER_CHAIN * GRID_W, (chain + 1) * ROWS_PER_CHAIN * GRID_W)
            finish(p, v_all, ROWS_PER_CHAIN, GRID_W, chain_rows)

    @pl.when(t == CTX_TILE)
    def _():
        p = probabilities(_dot_nt(stack_heads([q_ref[0]]), k_ref[0, ctx, :]))
        finish(p, v_ref[0, ctx, :], 1, TILE, slice(0, TILE))


def _nbr_attn(q, k, v, z, bias):
    pair = lambda rows: pl.BlockSpec((1, rows, LANES), lambda b, p, t: (b, 0 if rows == T else t, p))
    return pl.pallas_call(
        _nbr_attn_kernel,
        out_shape=jax.ShapeDtypeStruct((BATCH, T, C_WIDTH), BF16),
        grid=(BATCH, C_HEADS // 2, N_TILES),
        in_specs=[pair(TILE), pair(T), pair(T), pair(TILE),
                  pl.BlockSpec((1, 2, BIAS_SLOTS, GRID_W, LANES), lambda b, p, t: (p, 0, 0, 0, 0))],
        out_specs=pair(TILE),
        compiler_params=_params(("parallel", "parallel", "parallel")),
        name="nbr_attn",
    )(q, k, v, z, bias)


def _odd_out_kernel(o_ref, w_ref, x_ref, mod_ref, y_ref):
    y_ref[0] = x_ref[0] + mod_ref[0, 2:3, :] * _dot(o_ref[0], w_ref[...])


def _odd_out(o, w, xu, mod, n_tiles):
    tok = lambda width: pl.BlockSpec((1, TILE, width), lambda b, t: (b, t, 0))
    return pl.pallas_call(
        _odd_out_kernel,
        out_shape=jax.ShapeDtypeStruct((BATCH, n_tiles * TILE, D), F32),
        grid=(BATCH, n_tiles),
        in_specs=[tok(C_WIDTH), _const_spec((C_WIDTH, D)), tok(D), pl.BlockSpec((1, 3, D), _mod_index)],
        out_specs=tok(D),
        compiler_params=_params(("parallel", "parallel")),
        name="odd_out",
    )(o, w, xu, mod)


HALF = A_ROPE // 2
ROPE_PERM = np.array([a * HALF + h * (HALF // 2) + f for h in range(2) for a in range(2) for f in range(HALF // 2)])


def _head_slot(nope, rope):
    pad = jnp.zeros(rope.shape[:-1] + (LANES - A_QK - HALF,), rope.dtype)
    return jnp.concatenate([nope, rope[..., ROPE_PERM], rope[..., ROPE_PERM[:HALF]], pad], axis=-1)


def _rope_tables():
    n = np.arange(SEQ)
    pos = np.stack([n // GRID_W, n % GRID_W], axis=1).astype(np.float32)
    inv = (ROPE_THETA ** (-np.arange(0, HALF, 2, dtype=np.float32) / HALF)).astype(np.float32)
    ang = (pos[:, :, None] * inv).astype(np.float64).reshape(SEQ, HALF)
    cos = np.ones((T, LANES), np.float32)
    sin = np.zeros((T, LANES), np.float32)
    cos[:, A_QK:] = 0.0
    cos[:SEQ, A_NOPE:A_QK] = np.tile(np.cos(ang), (1, 2))
    sin[:SEQ, A_NOPE:A_QK] = np.concatenate([-np.sin(ang), np.sin(ang)], axis=1)
    return jnp.asarray(cos), jnp.asarray(sin)


def _pad_lanes(v, width=LANES):
    return jnp.pad(v, (0, width - v.shape[0])).reshape(1, width)


def _even_weights(j, p, rope):
    w_in = p["ev_w_in"][j]
    q_l, kv_l, k_r, z_a, q_b, k_b, v_b, lr, z_b = jnp.split(
        w_in, [768, 1024, 1056, 1568, 1824, 2080, 2592, 2624], axis=1)
    zeros = jnp.zeros((D, 32), F32)
    misc = jnp.concatenate([lr, zeros, _head_slot(jnp.zeros((D, 0), F32), k_r)], axis=1)
    w_in2 = jnp.concatenate([q_l, kv_l, z_a, q_b, k_b, v_b, z_b, misc], axis=1).astype(BF16)
    w_uq = p["ev_w_uq"][j].reshape(Q_LORA, A_HEADS, A_QK)
    w_uq = _head_slot(w_uq[..., :A_NOPE], w_uq[..., A_NOPE:])
    w_ukv = p["ev_w_ukv"][j].reshape(KV_LORA, A_HEADS, A_NOPE + A_V)
    w_uk = jnp.pad(w_ukv[:, :, :A_NOPE], ((0, 0), (0, 0), (0, LANES - A_NOPE))).reshape(KV_LORA, A_HEADS * LANES)
    w_uv = w_ukv[:, :, A_NOPE:].reshape(KV_LORA, A_WIDTH)
    gate_w = p["ev_gate_w"][j]
    w_gate = jnp.zeros((LANES, 2 * B_KW), F32)
    w_gate = w_gate.at[:GATE_RANK, :B_KW].set(gate_w[0]).at[GATE_RANK:2 * GATE_RANK, B_KW:].set(gate_w[1])
    w_out = p["ev_w_out"][j].astype(BF16)
    return {
        "w_in": w_in2,
        "q_norm": p["ev_q_norm"][j].reshape(1, Q_LORA),
        "w_uq": w_uq.reshape(Q_LORA, A_HEADS * LANES).astype(BF16),
        "kv_norm": p["ev_kv_norm"][j].reshape(1, KV_LORA),
        "w_ukv": jnp.concatenate([w_uk, w_uv], axis=1).astype(BF16),
        "q_gain": _head_slot(p["ev_q_gain"][j][:A_NOPE], p["ev_q_gain"][j][A_NOPE:]).reshape(1, LANES),
        "k_gain": _head_slot(p["ev_k_gain"][j][:A_NOPE], p["ev_k_gain"][j][A_NOPE:]).reshape(1, LANES),
        "w_gate": w_gate.astype(BF16),
        "gate_b": p["ev_gate_b"][j].reshape(1, 2 * B_KW),
        "cos": rope[0], "sin": rope[1],
        "gla_norm": p["ev_gla_norm"][j].reshape(1, B_WIDTH),
        "w_out_a": w_out[:A_WIDTH],
        "w_out_b": w_out[A_WIDTH:],
    }


def _nbr_bias_table(rpb):
    col = np.arange(GRID_W)
    col_start = np.clip(col - WIN_COLS // 2, 0, GRID_W - WIN_COLS)
    col_mask = (col[None, :] >= col_start[:, None]) & (col[None, :] < col_start[:, None] + WIN_COLS)
    col_idx = np.clip(col[None, :] - col[:, None] + WIN_COLS - 1, 0, 2 * WIN_COLS - 2)
    n_rel = 2 * WIN_COLS - 1
    select = (col_idx.reshape(1, -1) == np.arange(n_rel)[:, None]).astype(np.float32)
    picked = jnp.dot(rpb.reshape(-1, n_rel), jnp.asarray(select), precision=lax.Precision.HIGHEST)
    picked = picked.reshape(C_HEADS, 2 * WIN_ROWS - 1, GRID_W, GRID_W)
    tab = jnp.where(jnp.asarray(col_mask)[None, None], picked * LOG2E, NEG)
    masked = jnp.full((C_HEADS, 1, GRID_W, GRID_W), NEG, F32)
    ext = jnp.concatenate([masked, tab, masked], axis=1)
    pairs = jnp.concatenate([ext[:, :-1], ext[:, 1:]], axis=-1)
    return pairs.reshape(C_HEADS // 2, 2, BIAS_SLOTS, GRID_W, LANES)


def kernel(x, c, ctx, c_ctx, norm_g, ada_w, ada_b, ev_w_in, ev_q_norm, ev_w_uq, ev_kv_norm, ev_w_ukv, ev_q_gain,
           ev_k_gain, ev_gate_w, ev_gate_b, ev_gla_norm, ev_w_out, od_w_in, od_q_gain, od_k_gain, od_rpb, od_w_out):
    p = dict(ev_w_in=ev_w_in, ev_q_norm=ev_q_norm, ev_w_uq=ev_w_uq, ev_kv_norm=ev_kv_norm, ev_w_ukv=ev_w_ukv,
             ev_q_gain=ev_q_gain, ev_k_gain=ev_k_gain, ev_gate_w=ev_gate_w, ev_gate_b=ev_gate_b,
             ev_gla_norm=ev_gla_norm, ev_w_out=ev_w_out)
    c16 = jnp.zeros((MOD_ROWS, D), F32).at[:BATCH].set(c).at[BATCH].set(c_ctx)
    mod_all = _modulation(c16, ada_w, ada_b).reshape(DEPTH, MOD_ROWS, 3, D)
    rope = _rope_tables()
    xu = jnp.concatenate([x, ctx], axis=1)
    for i in range(DEPTH):
        j = i // 2
        mod = mod_all[i]
        ng = norm_g[i].reshape(1, D)
        if i % 2 == 0:
            w = _even_weights(j, p, rope)
            q, k, v, za, gq, gk, gv, gf, gb, zb = _even_in(xu, mod, ng, w)
            a = _mla_attn(q, k, v, za)
            of, ob = _gla(gq, gk, gv, gf, gb)
            xu = _even_out(a, of, ob, zb, w["gla_norm"], w["w_out_a"], w["w_out_b"], xu, mod)
        else:
            q, k, v, z = _odd_in(xu, mod, ng, od_w_in[j].astype(BF16), _pad_lanes(jnp.tile(od_q_gain[j], 2)),
                                 _pad_lanes(jnp.tile(od_k_gain[j], 2)))
            o = _nbr_attn(q, k, v, z, _nbr_bias_table(od_rpb[j]))
            n_tiles = N_TILES if i < DEPTH - 1 else CTX_TILE
            xu = _odd_out(o, od_w_out[j].astype(BF16), xu, mod, n_tiles)
    return xu
```

```python
import functools

import jax
import jax.numpy as jnp
import numpy as np
from jax import lax
from jax.experimental import pallas as pl
from jax.experimental.pallas import tpu as pltpu

D = 1024
BATCH = 8
SEQ = 2048
DEPTH = 4
GRID_W = 64
GRID_ROWS = SEQ // GRID_W
CTX = 256
T = SEQ + CTX
EPS = 1e-6
ROPE_THETA = 10000.0

A_HEADS = 8
A_NOPE = 64
A_ROPE = 32
A_V = 64
A_QK = A_NOPE + A_ROPE
KV_LORA = 256
Q_LORA = 768
A_WIDTH = A_HEADS * A_V

B_HEADS = 4
B_DK = 64
B_DV = 128
B_KW = B_HEADS * B_DK
B_WIDTH = B_HEADS * B_DV
GATE_RANK = 16
GATE_TAU = 16.0
CHUNK = 64

C_HEADS = 16
C_HD = 64
C_WIDTH = C_HEADS * C_HD
WIN_ROWS = 8
WIN_COLS = 16

LANES = 128
TILE = 256
N_TILES = T // TILE
CTX_TILE = N_TILES - 1
MOD_ROWS = 16
EVEN_COLS = 3200
NEG = -1e30
LOG2E = 1.4426950408889634
VMEM_LIMIT = 56 * 1024 * 1024

F32 = jnp.float32
BF16 = jnp.bfloat16


def _dot(a, b):
    return lax.dot_general(a, b, (((1,), (0,)), ((), ())), preferred_element_type=F32)


def _dot_nt(a, b):
    return lax.dot_general(a, b, (((1,), (1,)), ((), ())), preferred_element_type=F32)


def _dot_tn(a, b):
    return lax.dot_general(a, b, (((0,), (0,)), ((), ())), preferred_element_type=F32)


def _silu(x):
    return x / (1.0 + jnp.exp(-x))


def _rms(x, n):
    return x * lax.rsqrt(jnp.sum(x * x, axis=-1, keepdims=True) * (1.0 / n) + EPS)


def _params(sem):
    return pltpu.CompilerParams(dimension_semantics=sem, vmem_limit_bytes=VMEM_LIMIT)


def _const_spec(shape):
    nd = len(shape)
    return pl.BlockSpec(shape, lambda *_: (0,) * nd)


def _mod_index(b, t):
    return (jnp.where(t == CTX_TILE, BATCH, b), 0, 0)


def _mod_kernel(c_ref, w_ref, b_ref, o_ref):
    a = _silu(c_ref[...]).astype(BF16)
    o_ref[0] = _dot(a, w_ref[0].astype(BF16)) + b_ref[0]


def _modulation(c16, ada_w, ada_b):
    return pl.pallas_call(
        _mod_kernel,
        out_shape=jax.ShapeDtypeStruct((DEPTH, MOD_ROWS, 3 * D), F32),
        grid=(DEPTH, 3),
        in_specs=[
            pl.BlockSpec((MOD_ROWS, D), lambda i, j: (0, 0)),
            pl.BlockSpec((1, D, D), lambda i, j: (i, 0, j)),
            pl.BlockSpec((1, 1, D), lambda i, j: (i, 0, j)),
        ],
        out_specs=pl.BlockSpec((1, MOD_ROWS, D), lambda i, j: (i, 0, j)),
        compiler_params=_params(("parallel", "parallel")),
        name="modulation",
    )(c16, ada_w, ada_b.reshape(DEPTH, 1, 3 * D))


def _modulated_norm(x, mod_ref, ng_ref):
    return _rms(x, D) * ng_ref[...] * (1.0 + mod_ref[0, 1:2, :]) + mod_ref[0, 0:1, :]


def _group_sumsq(u, group, counted):
    sq = u * u
    hi = sq.astype(BF16)
    lo = (sq - hi.astype(F32)).astype(BF16)
    width = 2 * LANES
    row = lax.broadcasted_iota(jnp.int32, (width, width), 0)
    col = lax.broadcasted_iota(jnp.int32, (width, width), 1)
    ones = ((row // group == col // group) & (row % group < counted)).astype(BF16)
    cols = [slice(c, c + width) for c in range(0, u.shape[1], width)]
    return jnp.concatenate([_dot(hi[:, c], ones) + _dot(lo[:, c], ones) for c in cols], axis=1)


def _slot_sumsq(u):
    return _group_sumsq(u, LANES, A_QK)


def _stream_specs(ctx_block):
    return [pl.BlockSpec((1, TILE, D), lambda b, t: (b, jnp.minimum(t, CTX_TILE - 1), 0)),
            pl.BlockSpec((1, TILE, D), lambda b, t: (b, ctx_block, 0))]


def _stream_tile(lat_ref, ctx_ref):
    return jnp.where(pl.program_id(1) == CTX_TILE, ctx_ref[0], lat_ref[0])


def _even_in_kernel(xl_ref, xc_ref, mod_ref, ng_ref, win_ref, qn_ref, wuq_ref, kvn_ref, wukv_ref, qg_ref, kg_ref,
                    wg_ref, gbias_ref, cos_ref, sin_ref,
                    q_ref, k_ref, v_ref, za_ref, gq_ref, gk_ref, gv_ref, gf_ref, gb_ref, zb_ref):
    h = _modulated_norm(_stream_tile(xl_ref, xc_ref), mod_ref, ng_ref)
    lat = _dot(h.astype(BF16), win_ref[...])
    q_lat = lat[:, 0:768]
    kv_lat = lat[:, 768:1024]
    za_ref[0] = lat[:, 1024:1536]
    gq_ref[0] = lat[:, 1536:1792] * (B_DK ** -0.5)
    gk_ref[0] = lat[:, 1792:2048]
    gv_ref[0] = lat[:, 2048:2560].astype(BF16)
    zb_ref[0] = lat[:, 2560:3072]
    misc = lat[:, 3072:3200]

    cos = cos_ref[...]
    sin = sin_ref[...]

    def rope(u):
        return u * cos + pltpu.roll(u, LANES - HALF, 1) * sin

    qf = _dot((_rms(q_lat, Q_LORA) * qn_ref[...]).astype(BF16), wuq_ref[...])
    kvf = _dot((_rms(kv_lat, KV_LORA) * kvn_ref[...]).astype(BF16), wukv_ref[...])
    lane = lax.broadcasted_iota(jnp.int32, (1, LANES), 1)
    rotary = (lane >= A_NOPE) & (lane < A_QK)
    kr_roped = rope(jnp.where(lane >= A_NOPE, misc, 0.0) * kg_ref[...])
    k_rope = jnp.where(rotary, misc, 0.0)
    kr_ss = jnp.sum(k_rope * k_rope, axis=-1, keepdims=True)
    q_ss = _slot_sumsq(qf)
    k_ss = _slot_sumsq(kvf[:, :A_HEADS * LANES])
    for hh in range(A_HEADS):
        sl = slice(hh * LANES, (hh + 1) * LANES)
        q_scale = lax.rsqrt(q_ss[:, sl] * (1.0 / A_QK) + EPS) * (A_QK ** -0.5 * LOG2E)
        q_ref[0, hh] = (rope(qf[:, sl] * qg_ref[...]) * q_scale).astype(BF16)
        k_scale = lax.rsqrt((k_ss[:, sl] + kr_ss) * (1.0 / A_QK) + EPS)
        k_ref[0, hh] = ((kvf[:, sl] * kg_ref[...] + kr_roped) * k_scale).astype(BF16)
    v_ref[0] = kvf[:, A_HEADS * LANES:].astype(BF16)

    logits = _dot(misc.astype(BF16), wg_ref[...]) + gbias_ref[...]
    g = (jnp.minimum(logits, 0.0) - jnp.log1p(jnp.exp(-jnp.abs(logits)))) * (1.0 / GATE_TAU)
    gf_ref[0] = g[:, :B_KW]
    gb_ref[0] = g[:, B_KW:]


def _even_in(stream, mod, ng, w):
    x_lat, x_ctx, ctx_block = stream
    tok = lambda width: pl.BlockSpec((1, TILE, width), lambda b, t: (b, t, 0))
    head = pl.BlockSpec((1, A_HEADS, TILE, LANES), lambda b, t: (b, 0, t, 0))
    rope_spec = pl.BlockSpec((TILE, LANES), lambda b, t: (t, 0))
    sd = jax.ShapeDtypeStruct
    return pl.pallas_call(
        _even_in_kernel,
        out_shape=(
            sd((BATCH, A_HEADS, T, LANES), BF16),
            sd((BATCH, A_HEADS, T, LANES), BF16),
            sd((BATCH, T, A_WIDTH), BF16),
            sd((BATCH, T, A_WIDTH), F32),
            sd((BATCH, T, B_KW), F32),
            sd((BATCH, T, B_KW), F32),
            sd((BATCH, T, B_WIDTH), BF16),
            sd((BATCH, T, B_KW), F32),
            sd((BATCH, T, B_KW), F32),
            sd((BATCH, T, B_WIDTH), F32),
        ),
        grid=(BATCH, N_TILES),
        in_specs=_stream_specs(ctx_block) + [
            pl.BlockSpec((1, 3, D), _mod_index),
            _const_spec((1, D)),
            _const_spec((D, EVEN_COLS)),
            _const_spec((1, Q_LORA)),
            _const_spec((Q_LORA, A_HEADS * LANES)),
            _const_spec((1, KV_LORA)),
            _const_spec((KV_LORA, A_HEADS * LANES + A_WIDTH)),
            _const_spec((1, LANES)),
            _const_spec((1, LANES)),
            _const_spec((LANES, 2 * B_KW)),
            _const_spec((1, 2 * B_KW)),
            rope_spec, rope_spec,
        ],
        out_specs=(head, head, tok(A_WIDTH), tok(A_WIDTH), tok(B_KW), tok(B_KW), tok(B_WIDTH),
                   tok(B_KW), tok(B_KW), tok(B_WIDTH)),
        compiler_params=_params(("parallel", "parallel")),
        name="even_in",
    )(x_lat, x_ctx, mod, ng, w["w_in"], w["q_norm"], w["w_uq"], w["kv_norm"], w["w_ukv"], w["q_gain"], w["k_gain"],
      w["w_gate"], w["gate_b"], w["cos"], w["sin"])


def _values_with_ones(v):
    return jnp.concatenate([v, jnp.ones(v.shape, BF16)], axis=1)


def _normalise(acc):
    return acc[:, :LANES] / acc[:, LANES:]


def _mla_attn_kernel(q_ref, k_ref, v_ref, z_ref, o_ref):
    first_head = lax.broadcasted_iota(jnp.int32, (1, LANES), 1) < A_V

    def attend(rows, key_groups):
        rowmax = lambda s: functools.reduce(jnp.maximum, [jnp.max(x, axis=-1, keepdims=True) for x in s])
        q0 = q_ref[0, 0, rows, :]
        q1 = q_ref[0, 1, rows, :]
        s0 = [_dot_nt(q0, k_ref[0, 0, g, :]) for g in key_groups]
        m0 = rowmax(s0)
        s1, p0 = [], []
        for i, g in enumerate(key_groups):
            s1.append(_dot_nt(q1, k_ref[0, 1, g, :]))
            p0.append(jnp.exp2(s0[i] - m0).astype(BF16))
        m1 = rowmax(s1)
        acc0, p1 = [], []
        for i, g in enumerate(key_groups):
            acc0.append(_dot(p0[i], _values_with_ones(v_ref[0, g, :])))
            p1.append(jnp.exp2(s1[i] - m1).astype(BF16))
        acc1 = [_dot(p1[i], _values_with_ones(v_ref[0, g, :])) for i, g in enumerate(key_groups)]
        o0 = _normalise(functools.reduce(jnp.add, acc0))
        o1 = _normalise(functools.reduce(jnp.add, acc1))
        o_ref[0, rows, :] = (jnp.where(first_head, o0, o1) * _silu(z_ref[0, rows, :])).astype(BF16)

    @pl.loop(0, CTX_TILE)
    def _(t):
        attend(pl.ds(pl.multiple_of(t * TILE, TILE), TILE), [slice(0, 768), slice(768, 1536), slice(1536, T)])

    attend(slice(SEQ, T), [slice(SEQ, T)])


def _mla_attn(q, k, v, za):
    pair = pl.BlockSpec((1, T, LANES), lambda b, p: (b, 0, p))
    heads = pl.BlockSpec((1, 2, T, LANES), lambda b, p: (b, p, 0, 0))
    return pl.pallas_call(
        _mla_attn_kernel,
        out_shape=jax.ShapeDtypeStruct((BATCH, T, A_WIDTH), BF16),
        grid=(BATCH, A_HEADS // 2),
        in_specs=[heads, heads, pair, pair],
        out_specs=pair,
        compiler_params=_params(("parallel", "parallel")),
        name="mla_attn",
    )(q, k, v, za)


def _gla_direction(q_ref, k_ref, v_ref, g_ref, o_ref, s_ref, tile, forward):
    rows = pl.ds(pl.multiple_of(tile * TILE, TILE), TILE)
    q = q_ref[0, rows, :]
    k = k_ref[0, rows, :]
    g = g_ref[0, rows, :]
    v = v_ref[0, rows, :]
    row = lax.broadcasted_iota(jnp.int32, (TILE, TILE), 0)
    col = lax.broadcasted_iota(jnp.int32, (TILE, TILE), 1)
    in_chunk_causal = ((row // CHUNK) == (col // CHUNK)) & ((col <= row) if forward else (col >= row))
    tri = in_chunk_causal.astype(BF16)
    g_hi = g.astype(BF16)
    g_lo = (g - g_hi.astype(F32)).astype(BF16)
    b = _dot(tri, g_hi) + _dot(tri, g_lo)

    n_chunks = TILE // CHUNK
    chunks = [slice(c * CHUNK, (c + 1) * CHUNK) for c in range(n_chunks)]
    last = CHUNK - 1 if forward else 0
    b_last = [b[c * CHUNK + last:c * CHUNK + last + 1] for c in range(n_chunks)]
    b_last_rows = jnp.concatenate([jnp.broadcast_to(bl, (CHUNK, LANES)) for bl in b_last], axis=0)
    q_t = q * jnp.exp(b)
    k_t = (k * jnp.exp(-b)).astype(BF16)
    k_dec = (k * jnp.exp(b_last_rows - b)).astype(BF16)
    yield

    lane = lax.broadcasted_iota(jnp.int32, (1, LANES), 1)
    intra = []
    for hh, head_lanes in enumerate((lane < B_DK, lane >= B_DK)):
        qm = jnp.where(head_lanes, q_t, 0.0).astype(BF16)
        att = jnp.where(in_chunk_causal, _dot_nt(qm, k_t), 0.0).astype(BF16)
        intra.append(_dot(att, v[:, hh * B_DV:(hh + 1) * B_DV]))
    yield

    srow = lax.broadcasted_iota(jnp.int32, (2 * B_DK, 2 * B_DV), 0)
    scol = lax.broadcasted_iota(jnp.int32, (2 * B_DK, 2 * B_DV), 1)
    own_head = (srow // B_DK) == (scol // B_DV)
    updates = [jnp.where(own_head, _dot_tn(k_dec[sl], v[sl]), 0.0) for sl in chunks]
    yield
    state = s_ref[...]
    starts = [None] * n_chunks
    for c in (range(n_chunks) if forward else reversed(range(n_chunks))):
        starts[c] = state.astype(BF16)
        decay = jnp.broadcast_to(jnp.exp(b_last[c]), (LANES, LANES)).T
        state = state * jnp.concatenate([decay, decay], axis=1) + updates[c]
    s_ref[...] = state
    yield
    q_tb = q_t.astype(BF16)
    inter = [_dot(q_tb[sl], starts[c]) for c, sl in enumerate(chunks)]
    o_ref[0, rows, :] = jnp.concatenate(intra, axis=1) + jnp.concatenate(inter, axis=0)


def _gla_kernel(q_ref, k_ref, v_ref, gf_ref, gb_ref, of_ref, ob_ref, sf_ref, sb_ref):
    sf_ref[...] = jnp.zeros_like(sf_ref)
    sb_ref[...] = jnp.zeros_like(sb_ref)

    @pl.loop(0, N_TILES)
    def _(step):
        fwd_tile = jnp.where(step == 0, CTX_TILE, step - 1)
        bwd_tile = CTX_TILE - step
        stages = [_gla_direction(q_ref, k_ref, v_ref, gf_ref, of_ref, sf_ref, fwd_tile, True),
                  _gla_direction(q_ref, k_ref, v_ref, gb_ref, ob_ref, sb_ref, bwd_tile, False)]
        while stages:
            stages = [s for s in stages if next(s, StopIteration) is not StopIteration]


def _gla(gq, gk, gv, gf, gb):
    kspec = pl.BlockSpec((1, T, LANES), lambda b, p: (b, 0, p))
    vspec = pl.BlockSpec((1, T, 2 * B_DV), lambda b, p: (b, 0, p))
    out = jax.ShapeDtypeStruct((BATCH, T, B_WIDTH), F32)
    return pl.pallas_call(
        _gla_kernel,
        out_shape=(out, out),
        grid=(BATCH, B_HEADS // 2),
        in_specs=[kspec, kspec, vspec, kspec, kspec],
        out_specs=(vspec, vspec),
        scratch_shapes=[pltpu.VMEM((2 * B_DK, 2 * B_DV), F32), pltpu.VMEM((2 * B_DK, 2 * B_DV), F32)],
        compiler_params=_params(("parallel", "parallel")),
        name="gla_scan",
    )(gq, gk, gv, gf, gb)


def _even_out_kernel(a_ref, of_ref, ob_ref, zb_ref, gn_ref, wa_ref, wb_ref, xl_ref, xc_ref, mod_ref, o_ref):
    bsum = of_ref[0] + ob_ref[0]
    zb = zb_ref[0]
    parts = []
    for hh in range(B_HEADS):
        sl = slice(hh * B_DV, (hh + 1) * B_DV)
        parts.append((_rms(bsum[:, sl], B_DV) * gn_ref[:, sl] * _silu(zb[:, sl])).astype(BF16))
    y = _dot(a_ref[0], wa_ref[...]) + _dot(jnp.concatenate(parts, axis=1), wb_ref[...])
    o_ref[0] = _stream_tile(xl_ref, xc_ref) + mod_ref[0, 2:3, :] * y


def _even_out(a, of, ob, zb, gn, wa, wb, stream, mod):
    x_lat, x_ctx, ctx_block = stream
    tok = lambda width: pl.BlockSpec((1, TILE, width), lambda b, t: (b, t, 0))
    return pl.pallas_call(
        _even_out_kernel,
        out_shape=jax.ShapeDtypeStruct((BATCH, T, D), F32),
        grid=(BATCH, N_TILES),
        in_specs=[tok(A_WIDTH), tok(B_WIDTH), tok(B_WIDTH), tok(B_WIDTH), _const_spec((1, B_WIDTH)),
                  _const_spec((A_WIDTH, D)), _const_spec((B_WIDTH, D))] + _stream_specs(ctx_block) + [
                  pl.BlockSpec((1, 3, D), _mod_index)],
        out_specs=tok(D),
        compiler_params=_params(("parallel", "parallel")),
        name="even_out",
    )(a, of, ob, zb, gn, wa, wb, x_lat, x_ctx, mod)


def _odd_in_kernel(x_ref, mod_ref, ng_ref, win_ref, qg_ref, kg_ref, q_ref, k_ref, v_ref, z_ref):
    h = _modulated_norm(x_ref[0], mod_ref, ng_ref)
    u = _dot(h.astype(BF16), win_ref[...])
    ss = _group_sumsq(u[:, :2 * C_WIDTH], C_HD, C_HD)
    scale = lax.rsqrt(ss * (1.0 / C_HD) + EPS)
    for j in range(C_WIDTH // LANES):
        sl = slice(j * LANES, (j + 1) * LANES)
        q_ref[0, :, sl] = (u[:, sl] * scale[:, sl] * qg_ref[...] * (C_HD ** -0.5 * LOG2E)).astype(BF16)
        ks = slice(C_WIDTH + j * LANES, C_WIDTH + (j + 1) * LANES)
        k_ref[0, :, sl] = (u[:, ks] * scale[:, ks] * kg_ref[...]).astype(BF16)
    v_ref[0] = u[:, 2 * C_WIDTH:3 * C_WIDTH].astype(BF16)
    z_ref[0] = u[:, 3 * C_WIDTH:]


def _odd_in(xu, mod, ng, win, qg, kg):
    tok = lambda width: pl.BlockSpec((1, TILE, width), lambda b, t: (b, t, 0))
    sd = jax.ShapeDtypeStruct
    return pl.pallas_call(
        _odd_in_kernel,
        out_shape=(sd((BATCH, T, C_WIDTH), BF16), sd((BATCH, T, C_WIDTH), BF16), sd((BATCH, T, C_WIDTH), BF16),
                   sd((BATCH, T, C_WIDTH), F32)),
        grid=(BATCH, N_TILES),
        in_specs=[tok(D), pl.BlockSpec((1, 3, D), _mod_index), _const_spec((1, D)), _const_spec((D, 4 * C_WIDTH)),
                  _const_spec((1, LANES)), _const_spec((1, LANES))],
        out_specs=(tok(C_WIDTH), tok(C_WIDTH), tok(C_WIDTH), tok(C_WIDTH)),
        compiler_params=_params(("parallel", "parallel")),
        name="odd_in",
    )(xu, mod, ng, win, qg, kg)


ROWS_PER_TILE = TILE // GRID_W
ROWS_PER_CHAIN = 2
UNION_ROWS = WIN_ROWS + ROWS_PER_CHAIN
UNION_KEYS = UNION_ROWS * GRID_W
BIAS_SLOTS = 2 * WIN_ROWS


def _nbr_attn_kernel(q_ref, k_ref, v_ref, z_ref, bias_ref, o_ref):
    lane = lax.broadcasted_iota(jnp.int32, (1, LANES), 1)
    first = lane < C_HD
    head_mask = (first, lane >= C_HD)
    ctx = slice(SEQ, T)

    def stack_heads(blocks):
        zero = jnp.zeros(blocks[0].shape, BF16)
        return jnp.concatenate([jnp.where(head_mask[hh], blk, zero) for blk in blocks for hh in range(2)], axis=0)

    def probabilities(s):
        m = jnp.max(s, axis=-1, keepdims=True)
        return jnp.exp2(s - m).astype(BF16)

    def finish(p, v_all, n_blocks, rows, out_rows):
        o = _normalise(_dot(p, _values_with_ones(v_all)))
        o = jnp.concatenate(
            [jnp.where(first, o[(2 * i) * rows:(2 * i + 1) * rows], o[(2 * i + 1) * rows:(2 * i + 2) * rows])
             for i in range(n_blocks)], axis=0)
        o_ref[0, out_rows, :] = (o * _silu(z_ref[0, out_rows, :])).astype(BF16)

    def chain_keys(r0):
        u0 = jnp.clip(r0 - WIN_ROWS // 2, 0, GRID_ROWS - UNION_ROWS)
        return u0, pl.ds(pl.multiple_of(u0 * GRID_W, 2 * GRID_W), UNION_KEYS)

    def query_rows(r0):
        return pl.ds(pl.multiple_of(r0 * GRID_W, ROWS_PER_CHAIN * GRID_W), ROWS_PER_CHAIN * GRID_W)

    def raw_scores(r0, k_ctx):
        q_pair = q_ref[0, query_rows(r0), :]
        q_rows = [q_pair[rr * GRID_W:(rr + 1) * GRID_W] for rr in range(ROWS_PER_CHAIN)]
        k_all = jnp.concatenate([k_ref[0, chain_keys(r0)[1], :], k_ctx], axis=0)
        return _dot_nt(stack_heads(q_rows), k_all)

    def biased_scores(r0, s):
        u0, _ = chain_keys(r0)
        blocks = []
        for rr in range(ROWS_PER_CHAIN):
            r = r0 + rr
            r_start = jnp.clip(r - WIN_ROWS // 2, 0, GRID_ROWS - WIN_ROWS)
            for hh in range(2):
                blk = s[(2 * rr + hh) * GRID_W:(2 * rr + hh + 1) * GRID_W]
                pieces = []
                for mm in range(UNION_ROWS // 2):
                    u = u0 + 2 * mm
                    ok_l = ((u >= r_start) & (u < r_start + WIN_ROWS)).astype(jnp.int32)
                    ok_r = ((u + 1 >= r_start) & (u + 1 < r_start + WIN_ROWS)).astype(jnp.int32)
                    slot = jnp.clip(u - r + WIN_ROWS, 0, BIAS_SLOTS - 1)
                    in_window = jnp.where(first, ok_l, ok_r) > 0
                    scores = blk[:, mm * LANES:(mm + 1) * LANES] + bias_ref[0, hh, slot]
                    pieces.append(jnp.where(in_window, scores, NEG))
                pieces.append(blk[:, UNION_KEYS:])
                blocks.append(jnp.concatenate(pieces, axis=1))
        return jnp.concatenate(blocks, axis=0)

    @pl.loop(0, GRID_ROWS // ROWS_PER_TILE)
    def _(t):
        k_ctx = k_ref[0, ctx, :]
        v_ctx = v_ref[0, ctx, :]
        firsts = [t * ROWS_PER_TILE + chain * ROWS_PER_CHAIN for chain in range(ROWS_PER_TILE // ROWS_PER_CHAIN)]
        raw = [raw_scores(r0, k_ctx) for r0 in firsts]
        for r0, s in zip(firsts, raw):
            p = probabilities(biased_scores(r0, s))
            v_all = jnp.concatenate([v_ref[0, chain_keys(r0)[1], :], v_ctx], axis=0)
            finish(p, v_all, ROWS_PER_CHAIN, GRID_W, query_rows(r0))

    p = probabilities(_dot_nt(stack_heads([q_ref[0, ctx, :]]), k_ref[0, ctx, :]))
    finish(p, v_ref[0, ctx, :], 1, CTX, ctx)


def _nbr_attn(q, k, v, z, bias):
    pair = pl.BlockSpec((1, T, LANES), lambda p, b: (b, 0, p))
    return pl.pallas_call(
        _nbr_attn_kernel,
        out_shape=jax.ShapeDtypeStruct((BATCH, T, C_WIDTH), BF16),
        grid=(C_HEADS // 2, BATCH),
        in_specs=[pair, pair, pair, pair,
                  pl.BlockSpec((1, 2, BIAS_SLOTS, GRID_W, LANES), lambda p, b: (p, 0, 0, 0, 0))],
        out_specs=pair,
        compiler_params=_params(("parallel", "parallel")),
        name="nbr_attn",
    )(q, k, v, z, bias)


def _odd_out_kernel(o_ref, w_ref, x_ref, mod_ref, y_ref):
    y_ref[0] = x_ref[0] + mod_ref[0, 2:3, :] * _dot(o_ref[0], w_ref[...])


def _odd_out(o, w, xu, mod, n_tiles):
    tok = lambda width: pl.BlockSpec((1, TILE, width), lambda b, t: (b, t, 0))
    return pl.pallas_call(
        _odd_out_kernel,
        out_shape=jax.ShapeDtypeStruct((BATCH, n_tiles * TILE, D), F32),
        grid=(BATCH, n_tiles),
        in_specs=[tok(C_WIDTH), _const_spec((C_WIDTH, D)), tok(D), pl.BlockSpec((1, 3, D), _mod_index)],
        out_specs=tok(D),
        compiler_params=_params(("parallel", "parallel")),
        name="odd_out",
    )(o, w, xu, mod)


HALF = A_ROPE // 2
ROPE_PERM = np.array([a * HALF + h * (HALF // 2) + f for h in range(2) for a in range(2) for f in range(HALF // 2)])


def _head_slot(nope, rope):
    pad = jnp.zeros(rope.shape[:-1] + (LANES - A_QK - HALF,), rope.dtype)
    return jnp.concatenate([nope, rope[..., ROPE_PERM], rope[..., ROPE_PERM[:HALF]], pad], axis=-1)


def _rope_tables():
    n = np.arange(SEQ)
    pos = np.stack([n // GRID_W, n % GRID_W], axis=1).astype(np.float32)
    inv = (ROPE_THETA ** (-np.arange(0, HALF, 2, dtype=np.float32) / HALF)).astype(np.float32)
    ang = (pos[:, :, None] * inv).astype(np.float64).reshape(SEQ, HALF)
    cos = np.ones((T, LANES), np.float32)
    sin = np.zeros((T, LANES), np.float32)
    cos[:, A_QK:] = 0.0
    cos[:SEQ, A_NOPE:A_QK] = np.tile(np.cos(ang), (1, 2))
    sin[:SEQ, A_NOPE:A_QK] = np.concatenate([-np.sin(ang), np.sin(ang)], axis=1)
    return jnp.asarray(cos), jnp.asarray(sin)


def _pad_lanes(v, width=LANES):
    return jnp.pad(v, (0, width - v.shape[0])).reshape(1, width)


def _even_weights(j, p, rope):
    w_in = p["ev_w_in"][j]
    q_l, kv_l, k_r, z_a, q_b, k_b, v_b, lr, z_b = jnp.split(
        w_in, [768, 1024, 1056, 1568, 1824, 2080, 2592, 2624], axis=1)
    zeros = jnp.zeros((D, 32), F32)
    misc = jnp.concatenate([lr, zeros, _head_slot(jnp.zeros((D, 0), F32), k_r)], axis=1)
    w_in2 = jnp.concatenate([q_l, kv_l, z_a, q_b, k_b, v_b, z_b, misc], axis=1).astype(BF16)
    w_uq = p["ev_w_uq"][j].reshape(Q_LORA, A_HEADS, A_QK)
    w_uq = _head_slot(w_uq[..., :A_NOPE], w_uq[..., A_NOPE:])
    w_ukv = p["ev_w_ukv"][j].reshape(KV_LORA, A_HEADS, A_NOPE + A_V)
    w_uk = jnp.pad(w_ukv[:, :, :A_NOPE], ((0, 0), (0, 0), (0, LANES - A_NOPE))).reshape(KV_LORA, A_HEADS * LANES)
    w_uv = w_ukv[:, :, A_NOPE:].reshape(KV_LORA, A_WIDTH)
    gate_w = p["ev_gate_w"][j]
    w_gate = jnp.zeros((LANES, 2 * B_KW), F32)
    w_gate = w_gate.at[:GATE_RANK, :B_KW].set(gate_w[0]).at[GATE_RANK:2 * GATE_RANK, B_KW:].set(gate_w[1])
    w_out = p["ev_w_out"][j].astype(BF16)
    return {
        "w_in": w_in2,
        "q_norm": p["ev_q_norm"][j].reshape(1, Q_LORA),
        "w_uq": w_uq.reshape(Q_LORA, A_HEADS * LANES).astype(BF16),
        "kv_norm": p["ev_kv_norm"][j].reshape(1, KV_LORA),
        "w_ukv": jnp.concatenate([w_uk, w_uv], axis=1).astype(BF16),
        "q_gain": _head_slot(p["ev_q_gain"][j][:A_NOPE], p["ev_q_gain"][j][A_NOPE:]).reshape(1, LANES),
        "k_gain": _head_slot(p["ev_k_gain"][j][:A_NOPE], p["ev_k_gain"][j][A_NOPE:]).reshape(1, LANES),
        "w_gate": w_gate.astype(BF16),
        "gate_b": p["ev_gate_b"][j].reshape(1, 2 * B_KW),
        "cos": rope[0], "sin": rope[1],
        "gla_norm": p["ev_gla_norm"][j].reshape(1, B_WIDTH),
        "w_out_a": w_out[:A_WIDTH],
        "w_out_b": w_out[A_WIDTH:],
    }


def _nbr_bias_table(rpb):
    col = np.arange(GRID_W)
    col_start = np.clip(col - WIN_COLS // 2, 0, GRID_W - WIN_COLS)
    col_mask = (col[None, :] >= col_start[:, None]) & (col[None, :] < col_start[:, None] + WIN_COLS)
    col_idx = np.clip(col[None, :] - col[:, None] + WIN_COLS - 1, 0, 2 * WIN_COLS - 2)
    n_rel = 2 * WIN_COLS - 1
    select = (col_idx.reshape(1, -1) == np.arange(n_rel)[:, None]).astype(np.float32)
    picked = jnp.dot(rpb.reshape(-1, n_rel), jnp.asarray(select), precision=lax.Precision.HIGHEST)
    picked = picked.reshape(C_HEADS, 2 * WIN_ROWS - 1, GRID_W, GRID_W)
    tab = jnp.where(jnp.asarray(col_mask)[None, None], picked * LOG2E, NEG)
    masked = jnp.full((C_HEADS, 1, GRID_W, GRID_W), NEG, F32)
    ext = jnp.concatenate([masked, tab, masked], axis=1)
    pairs = jnp.concatenate([ext[:, :-1], ext[:, 1:]], axis=-1)
    return pairs.reshape(C_HEADS // 2, 2, BIAS_SLOTS, GRID_W, LANES)


def kernel(x, c, ctx, c_ctx, norm_g, ada_w, ada_b, ev_w_in, ev_q_norm, ev_w_uq, ev_kv_norm, ev_w_ukv, ev_q_gain,
           ev_k_gain, ev_gate_w, ev_gate_b, ev_gla_norm, ev_w_out, od_w_in, od_q_gain, od_k_gain, od_rpb, od_w_out):
    p = dict(ev_w_in=ev_w_in, ev_q_norm=ev_q_norm, ev_w_uq=ev_w_uq, ev_kv_norm=ev_kv_norm, ev_w_ukv=ev_w_ukv,
             ev_q_gain=ev_q_gain, ev_k_gain=ev_k_gain, ev_gate_w=ev_gate_w, ev_gate_b=ev_gate_b,
             ev_gla_norm=ev_gla_norm, ev_w_out=ev_w_out)
    c16 = jnp.zeros((MOD_ROWS, D), F32).at[:BATCH].set(c).at[BATCH].set(c_ctx)
    mod_all = _modulation(c16, ada_w, ada_b).reshape(DEPTH, MOD_ROWS, 3, D)
    rope = _rope_tables()
    xu = None
    for i in range(DEPTH):
        j = i // 2
        mod = mod_all[i]
        ng = norm_g[i].reshape(1, D)
        if i % 2 == 0:
            w = _even_weights(j, p, rope)
            stream = (x, ctx, 0) if xu is None else (xu, xu, CTX_TILE)
            q, k, v, za, gq, gk, gv, gf, gb, zb = _even_in(stream, mod, ng, w)
            a = _mla_attn(q, k, v, za)
            of, ob = _gla(gq, gk, gv, gf, gb)
            xu = _even_out(a, of, ob, zb, w["gla_norm"], w["w_out_a"], w["w_out_b"], stream, mod)
        else:
            q, k, v, z = _odd_in(xu, mod, ng, od_w_in[j].astype(BF16), _pad_lanes(jnp.tile(od_q_gain[j], 2)),
                                 _pad_lanes(jnp.tile(od_k_gain[j], 2)))
            o = _nbr_attn(q, k, v, z, _nbr_bias_table(od_rpb[j]))
            n_tiles = N_TILES if i < DEPTH - 1 else CTX_TILE
            xu = _odd_out(o, od_w_out[j].astype(BF16), xu, mod, n_tiles)
    return xu
```

```python
import functools

import jax
import jax.numpy as jnp
import numpy as np
from jax import lax
from jax.experimental import pallas as pl
from jax.experimental.pallas import tpu as pltpu

D = 1024
BATCH = 8
SEQ = 2048
DEPTH = 4
GRID_W = 64
GRID_ROWS = SEQ // GRID_W
CTX = 256
T = SEQ + CTX
EPS = 1e-6
ROPE_THETA = 10000.0

A_HEADS = 8
A_NOPE = 64
A_ROPE = 32
A_V = 64
A_QK = A_NOPE + A_ROPE
KV_LORA = 256
Q_LORA = 768
A_WIDTH = A_HEADS * A_V

B_HEADS = 4
B_DK = 64
B_DV = 128
B_KW = B_HEADS * B_DK
B_WIDTH = B_HEADS * B_DV
GATE_RANK = 16
GATE_TAU = 16.0
CHUNK = 64

C_HEADS = 16
C_HD = 64
C_WIDTH = C_HEADS * C_HD
WIN_ROWS = 8
WIN_COLS = 16

LANES = 128
TILE = 256
N_TILES = T // TILE
CTX_TILE = N_TILES - 1
MOD_ROWS = 16
EVEN_COLS = 3200
NEG = -1e30
LOG2E = 1.4426950408889634
VMEM_LIMIT = 56 * 1024 * 1024

F32 = jnp.float32
BF16 = jnp.bfloat16


def _dot(a, b):
    return lax.dot_general(a, b, (((1,), (0,)), ((), ())), preferred_element_type=F32)


def _dot_nt(a, b):
    return lax.dot_general(a, b, (((1,), (1,)), ((), ())), preferred_element_type=F32)


def _dot_tn(a, b):
    return lax.dot_general(a, b, (((0,), (0,)), ((), ())), preferred_element_type=F32)


def _silu(x):
    return x / (1.0 + jnp.exp(-x))


def _rms(x, n):
    return x * lax.rsqrt(jnp.sum(x * x, axis=-1, keepdims=True) * (1.0 / n) + EPS)


def _params(sem):
    return pltpu.CompilerParams(dimension_semantics=sem, vmem_limit_bytes=VMEM_LIMIT)


def _const_spec(shape):
    nd = len(shape)
    return pl.BlockSpec(shape, lambda *_: (0,) * nd)


def _mod_index(b, t):
    return (jnp.where(t == CTX_TILE, BATCH, b), 0, 0)


def _mod_kernel(c_ref, w_ref, b_ref, o_ref):
    a = _silu(c_ref[...]).astype(BF16)
    o_ref[0] = _dot(a, w_ref[0].astype(BF16)) + b_ref[0]


def _modulation(c16, ada_w, ada_b):
    return pl.pallas_call(
        _mod_kernel,
        out_shape=jax.ShapeDtypeStruct((DEPTH, MOD_ROWS, 3 * D), F32),
        grid=(DEPTH, 3),
        in_specs=[
            pl.BlockSpec((MOD_ROWS, D), lambda i, j: (0, 0)),
            pl.BlockSpec((1, D, D), lambda i, j: (i, 0, j)),
            pl.BlockSpec((1, 1, D), lambda i, j: (i, 0, j)),
        ],
        out_specs=pl.BlockSpec((1, MOD_ROWS, D), lambda i, j: (i, 0, j)),
        compiler_params=_params(("parallel", "parallel")),
        name="modulation",
    )(c16, ada_w, ada_b.reshape(DEPTH, 1, 3 * D))


def _modulated_norm(x, mod_ref, ng_ref):
    return _rms(x, D) * ng_ref[...] * (1.0 + mod_ref[0, 1:2, :]) + mod_ref[0, 0:1, :]


def _group_sumsq(u, group, counted):
    sq = (u * u).astype(BF16)
    width = 2 * LANES
    row = lax.broadcasted_iota(jnp.int32, (width, width), 0)
    col = lax.broadcasted_iota(jnp.int32, (width, width), 1)
    ones = ((row // group == col // group) & (row % group < counted)).astype(BF16)
    cols = [slice(c, c + width) for c in range(0, u.shape[1], width)]
    return jnp.concatenate([_dot(sq[:, c], ones) for c in cols], axis=1)


def _slot_sumsq(u):
    return _group_sumsq(u, LANES, A_QK)


def _stream_specs(ctx_block):
    return [pl.BlockSpec((1, TILE, D), lambda b, t: (b, jnp.minimum(t, CTX_TILE - 1), 0)),
            pl.BlockSpec((1, TILE, D), lambda b, t: (b, ctx_block, 0))]


def _stream_tile(lat_ref, ctx_ref):
    return jnp.where(pl.program_id(1) == CTX_TILE, ctx_ref[0], lat_ref[0])


def _even_in_kernel(xl_ref, xc_ref, mod_ref, ng_ref, win_ref, qn_ref, wuq_ref, kvn_ref, wukv_ref, qg_ref, kg_ref,
                    wg_ref, gbias_ref, cos_ref, sin_ref,
                    q_ref, k_ref, v_ref, za_ref, gq_ref, gk_ref, gv_ref, gf_ref, gb_ref, zb_ref):
    h = _modulated_norm(_stream_tile(xl_ref, xc_ref), mod_ref, ng_ref)
    lat = _dot(h.astype(BF16), win_ref[...])
    q_lat = lat[:, 0:768]
    kv_lat = lat[:, 768:1024]
    za_ref[0] = lat[:, 1024:1536]
    gq_ref[0] = lat[:, 1536:1792] * (B_DK ** -0.5)
    gk_ref[0] = lat[:, 1792:2048]
    gv_ref[0] = lat[:, 2048:2560].astype(BF16)
    zb_ref[0] = lat[:, 2560:3072]
    misc = lat[:, 3072:3200]

    cos = cos_ref[...]
    sin = sin_ref[...]

    def rope(u):
        return u * cos + pltpu.roll(u, LANES - HALF, 1) * sin

    qf = _dot((_rms(q_lat, Q_LORA) * qn_ref[...]).astype(BF16), wuq_ref[...])
    kvf = _dot((_rms(kv_lat, KV_LORA) * kvn_ref[...]).astype(BF16), wukv_ref[...])
    lane = lax.broadcasted_iota(jnp.int32, (1, LANES), 1)
    rotary = (lane >= A_NOPE) & (lane < A_QK)
    kr_roped = rope(jnp.where(lane >= A_NOPE, misc, 0.0) * kg_ref[...])
    k_rope = jnp.where(rotary, misc, 0.0)
    kr_ss = jnp.sum(k_rope * k_rope, axis=-1, keepdims=True)
    q_ss = _slot_sumsq(qf)
    k_ss = _slot_sumsq(kvf[:, :A_HEADS * LANES])
    for hh in range(A_HEADS):
        sl = slice(hh * LANES, (hh + 1) * LANES)
        q_scale = lax.rsqrt(q_ss[:, sl] * (1.0 / A_QK) + EPS) * (A_QK ** -0.5 * LOG2E)
        q_ref[0, hh] = (rope(qf[:, sl] * qg_ref[...]) * q_scale).astype(BF16)
        k_scale = lax.rsqrt((k_ss[:, sl] + kr_ss) * (1.0 / A_QK) + EPS)
        k_ref[0, hh] = ((kvf[:, sl] * kg_ref[...] + kr_roped) * k_scale).astype(BF16)
    v_ref[0] = kvf[:, A_HEADS * LANES:].astype(BF16)

    logits = _dot(misc.astype(BF16), wg_ref[...]) + gbias_ref[...]
    g = (jnp.minimum(logits, 0.0) - jnp.log1p(jnp.exp(-jnp.abs(logits)))) * (1.0 / GATE_TAU)
    gf_ref[0] = g[:, :B_KW]
    gb_ref[0] = g[:, B_KW:]


def _even_in(stream, mod, ng, w):
    x_lat, x_ctx, ctx_block = stream
    tok = lambda width: pl.BlockSpec((1, TILE, width), lambda b, t: (b, t, 0))
    head = pl.BlockSpec((1, A_HEADS, TILE, LANES), lambda b, t: (b, 0, t, 0))
    rope_spec = pl.BlockSpec((TILE, LANES), lambda b, t: (t, 0))
    sd = jax.ShapeDtypeStruct
    return pl.pallas_call(
        _even_in_kernel,
        out_shape=(
            sd((BATCH, A_HEADS, T, LANES), BF16),
            sd((BATCH, A_HEADS, T, LANES), BF16),
            sd((BATCH, T, A_WIDTH), BF16),
            sd((BATCH, T, A_WIDTH), F32),
            sd((BATCH, T, B_KW), F32),
            sd((BATCH, T, B_KW), F32),
            sd((BATCH, T, B_WIDTH), BF16),
            sd((BATCH, T, B_KW), F32),
            sd((BATCH, T, B_KW), F32),
            sd((BATCH, T, B_WIDTH), F32),
        ),
        grid=(BATCH, N_TILES),
        in_specs=_stream_specs(ctx_block) + [
            pl.BlockSpec((1, 3, D), _mod_index),
            _const_spec((1, D)),
            _const_spec((D, EVEN_COLS)),
            _const_spec((1, Q_LORA)),
            _const_spec((Q_LORA, A_HEADS * LANES)),
            _const_spec((1, KV_LORA)),
            _const_spec((KV_LORA, A_HEADS * LANES + A_WIDTH)),
            _const_spec((1, LANES)),
            _const_spec((1, LANES)),
            _const_spec((LANES, 2 * B_KW)),
            _const_spec((1, 2 * B_KW)),
            rope_spec, rope_spec,
        ],
        out_specs=(head, head, tok(A_WIDTH), tok(A_WIDTH), tok(B_KW), tok(B_KW), tok(B_WIDTH),
                   tok(B_KW), tok(B_KW), tok(B_WIDTH)),
        compiler_params=_params(("parallel", "parallel")),
        name="even_in",
    )(x_lat, x_ctx, mod, ng, w["w_in"], w["q_norm"], w["w_uq"], w["kv_norm"], w["w_ukv"], w["q_gain"], w["k_gain"],
      w["w_gate"], w["gate_b"], w["cos"], w["sin"])


def _values_with_ones(v):
    return jnp.concatenate([v, jnp.ones(v.shape, BF16)], axis=1)


def _normalise(acc):
    return acc[:, :LANES] / acc[:, LANES:]


MLA_TILES_PER_STEP = 4


def _mla_attn_kernel(q_ref, k_ref, v_ref, z_ref, o_ref):
    first_head = lax.broadcasted_iota(jnp.int32, (1, LANES), 1) < A_V

    def attend(row_tiles, key_groups):
        rowmax = lambda s: functools.reduce(jnp.maximum, [jnp.max(x, axis=-1, keepdims=True) for x in s])
        chains = [(rows, hh) for rows in row_tiles for hh in range(2)]
        n, groups = len(chains), range(len(key_groups))
        scores, top, probs, acc = {}, {}, {}, {}
        for step in range(n + 2):
            if step >= 1 and step - 1 < n:
                top[step - 1] = rowmax(scores[step - 1])
            for i in groups:
                g = key_groups[i]
                if step < n:
                    rows, hh = chains[step]
                    scores.setdefault(step, []).append(_dot_nt(q_ref[0, hh, rows, :], k_ref[0, hh, g, :]))
                if 0 <= step - 1 < n:
                    probs.setdefault(step - 1, []).append(
                        jnp.exp2(scores[step - 1][i] - top[step - 1]).astype(BF16))
                if 0 <= step - 2 < n:
                    acc.setdefault(step - 2, []).append(_dot(probs[step - 2][i], _values_with_ones(v_ref[0, g, :])))
            done = step - 2
            if done >= 0 and done % 2 == 1:
                rows = chains[done][0]
                o0 = _normalise(functools.reduce(jnp.add, acc[done - 1]))
                o1 = _normalise(functools.reduce(jnp.add, acc[done]))
                o_ref[0, rows, :] = (jnp.where(first_head, o0, o1) * _silu(z_ref[0, rows, :])).astype(BF16)

    @pl.loop(0, CTX_TILE // MLA_TILES_PER_STEP)
    def _(t):
        base = t * (MLA_TILES_PER_STEP * TILE)
        tiles = [pl.ds(pl.multiple_of(base + i * TILE, TILE), TILE) for i in range(MLA_TILES_PER_STEP)]
        attend(tiles, [slice(0, 768), slice(768, 1536), slice(1536, T)])

    attend([slice(SEQ, T)], [slice(SEQ, T)])


def _mla_attn(q, k, v, za):
    pair = pl.BlockSpec((1, T, LANES), lambda b, p: (b, 0, p))
    heads = pl.BlockSpec((1, 2, T, LANES), lambda b, p: (b, p, 0, 0))
    return pl.pallas_call(
        _mla_attn_kernel,
        out_shape=jax.ShapeDtypeStruct((BATCH, T, A_WIDTH), BF16),
        grid=(BATCH, A_HEADS // 2),
        in_specs=[heads, heads, pair, pair],
        out_specs=pair,
        compiler_params=_params(("parallel", "parallel")),
        name="mla_attn",
    )(q, k, v, za)


def _gla_direction(q_ref, k_ref, v_ref, g_ref, o_ref, s_ref, tile, forward):
    rows = pl.ds(pl.multiple_of(tile * TILE, TILE), TILE)
    q = q_ref[0, rows, :]
    k = k_ref[0, rows, :]
    g = g_ref[0, rows, :]
    v = v_ref[0, rows, :]
    row = lax.broadcasted_iota(jnp.int32, (TILE, TILE), 0)
    col = lax.broadcasted_iota(jnp.int32, (TILE, TILE), 1)
    in_chunk_causal = ((row // CHUNK) == (col // CHUNK)) & ((col <= row) if forward else (col >= row))
    tri = in_chunk_causal.astype(BF16)
    g_hi = g.astype(BF16)
    g_lo = (g - g_hi.astype(F32)).astype(BF16)
    b = _dot(tri, g_hi) + _dot(tri, g_lo)

    n_chunks = TILE // CHUNK
    chunks = [slice(c * CHUNK, (c + 1) * CHUNK) for c in range(n_chunks)]
    last = CHUNK - 1 if forward else 0
    b_last = [b[c * CHUNK + last:c * CHUNK + last + 1] for c in range(n_chunks)]
    b_last_rows = jnp.concatenate([jnp.broadcast_to(bl, (CHUNK, LANES)) for bl in b_last], axis=0)
    q_t = q * jnp.exp(b)
    k_t = (k * jnp.exp(-b)).astype(BF16)
    k_dec = (k * jnp.exp(b_last_rows - b)).astype(BF16)
    yield

    lane = lax.broadcasted_iota(jnp.int32, (1, LANES), 1)
    intra = []
    for hh, head_lanes in enumerate((lane < B_DK, lane >= B_DK)):
        qm = jnp.where(head_lanes, q_t, 0.0).astype(BF16)
        att = jnp.where(in_chunk_causal, _dot_nt(qm, k_t), 0.0).astype(BF16)
        intra.append(_dot(att, v[:, hh * B_DV:(hh + 1) * B_DV]))
    yield

    srow = lax.broadcasted_iota(jnp.int32, (2 * B_DK, 2 * B_DV), 0)
    scol = lax.broadcasted_iota(jnp.int32, (2 * B_DK, 2 * B_DV), 1)
    own_head = (srow // B_DK) == (scol // B_DV)
    updates = [jnp.where(own_head, _dot_tn(k_dec[sl], v[sl]), 0.0) for sl in chunks]
    yield
    state = s_ref[...]
    starts = [None] * n_chunks
    for c in (range(n_chunks) if forward else reversed(range(n_chunks))):
        starts[c] = state.astype(BF16)
        decay = jnp.broadcast_to(jnp.exp(b_last[c]), (LANES, LANES)).T
        state = state * jnp.concatenate([decay, decay], axis=1) + updates[c]
    s_ref[...] = state
    yield
    q_tb = q_t.astype(BF16)
    inter = [_dot(q_tb[sl], starts[c]) for c, sl in enumerate(chunks)]
    o_ref[0, rows, :] += jnp.concatenate(intra, axis=1) + jnp.concatenate(inter, axis=0)


def _gla_kernel(q_ref, k_ref, v_ref, gf_ref, gb_ref, o_ref, sf_ref, sb_ref):
    sf_ref[...] = jnp.zeros_like(sf_ref)
    sb_ref[...] = jnp.zeros_like(sb_ref)
    o_ref[...] = jnp.zeros_like(o_ref)

    @pl.loop(0, N_TILES)
    def _(step):
        fwd_tile = jnp.where(step == 0, CTX_TILE, step - 1)
        bwd_tile = CTX_TILE - step
        stages = [_gla_direction(q_ref, k_ref, v_ref, gf_ref, o_ref, sf_ref, fwd_tile, True),
                  _gla_direction(q_ref, k_ref, v_ref, gb_ref, o_ref, sb_ref, bwd_tile, False)]
        while stages:
            stages = [s for s in stages if next(s, StopIteration) is not StopIteration]


def _gla(gq, gk, gv, gf, gb):
    kspec = pl.BlockSpec((1, T, LANES), lambda b, p: (b, 0, p))
    vspec = pl.BlockSpec((1, T, 2 * B_DV), lambda b, p: (b, 0, p))
    out = jax.ShapeDtypeStruct((BATCH, T, B_WIDTH), F32)
    return pl.pallas_call(
        _gla_kernel,
        out_shape=out,
        grid=(BATCH, B_HEADS // 2),
        in_specs=[kspec, kspec, vspec, kspec, kspec],
        out_specs=vspec,
        scratch_shapes=[pltpu.VMEM((2 * B_DK, 2 * B_DV), F32), pltpu.VMEM((2 * B_DK, 2 * B_DV), F32)],
        compiler_params=_params(("parallel", "parallel")),
        name="gla_scan",
    )(gq, gk, gv, gf, gb)


def _even_out_kernel(a_ref, gla_ref, zb_ref, gn_ref, wa_ref, wb_ref, xl_ref, xc_ref, mod_ref, o_ref):
    bsum = gla_ref[0]
    zb = zb_ref[0]
    parts = []
    for hh in range(B_HEADS):
        sl = slice(hh * B_DV, (hh + 1) * B_DV)
        parts.append((_rms(bsum[:, sl], B_DV) * gn_ref[:, sl] * _silu(zb[:, sl])).astype(BF16))
    y = _dot(a_ref[0], wa_ref[...]) + _dot(jnp.concatenate(parts, axis=1), wb_ref[...])
    o_ref[0] = _stream_tile(xl_ref, xc_ref) + mod_ref[0, 2:3, :] * y


def _even_out(a, gla, zb, gn, wa, wb, stream, mod):
    x_lat, x_ctx, ctx_block = stream
    tok = lambda width: pl.BlockSpec((1, TILE, width), lambda b, t: (b, t, 0))
    return pl.pallas_call(
        _even_out_kernel,
        out_shape=jax.ShapeDtypeStruct((BATCH, T, D), F32),
        grid=(BATCH, N_TILES),
        in_specs=[tok(A_WIDTH), tok(B_WIDTH), tok(B_WIDTH), _const_spec((1, B_WIDTH)),
                  _const_spec((A_WIDTH, D)), _const_spec((B_WIDTH, D))] + _stream_specs(ctx_block) + [
                  pl.BlockSpec((1, 3, D), _mod_index)],
        out_specs=tok(D),
        compiler_params=_params(("parallel", "parallel")),
        name="even_out",
    )(a, gla, zb, gn, wa, wb, x_lat, x_ctx, mod)


def _odd_in_kernel(x_ref, mod_ref, ng_ref, win_ref, qg_ref, kg_ref, q_ref, k_ref, v_ref, z_ref):
    h = _modulated_norm(x_ref[0], mod_ref, ng_ref)
    u = _dot(h.astype(BF16), win_ref[...])
    ss = _group_sumsq(u[:, :2 * C_WIDTH], C_HD, C_HD)
    scale = lax.rsqrt(ss * (1.0 / C_HD) + EPS)
    for j in range(C_WIDTH // LANES):
        sl = slice(j * LANES, (j + 1) * LANES)
        q_ref[0, :, sl] = (u[:, sl] * scale[:, sl] * qg_ref[...] * (C_HD ** -0.5 * LOG2E)).astype(BF16)
        ks = slice(C_WIDTH + j * LANES, C_WIDTH + (j + 1) * LANES)
        k_ref[0, :, sl] = (u[:, ks] * scale[:, ks] * kg_ref[...]).astype(BF16)
    v_ref[0] = u[:, 2 * C_WIDTH:3 * C_WIDTH].astype(BF16)
    z_ref[0] = u[:, 3 * C_WIDTH:]


def _odd_in(xu, mod, ng, win, qg, kg):
    tok = lambda width: pl.BlockSpec((1, TILE, width), lambda b, t: (b, t, 0))
    sd = jax.ShapeDtypeStruct
    return pl.pallas_call(
        _odd_in_kernel,
        out_shape=(sd((BATCH, T, C_WIDTH), BF16), sd((BATCH, T, C_WIDTH), BF16), sd((BATCH, T, C_WIDTH), BF16),
                   sd((BATCH, T, C_WIDTH), F32)),
        grid=(BATCH, N_TILES),
        in_specs=[tok(D), pl.BlockSpec((1, 3, D), _mod_index), _const_spec((1, D)), _const_spec((D, 4 * C_WIDTH)),
                  _const_spec((1, LANES)), _const_spec((1, LANES))],
        out_specs=(tok(C_WIDTH), tok(C_WIDTH), tok(C_WIDTH), tok(C_WIDTH)),
        compiler_params=_params(("parallel", "parallel")),
        name="odd_in",
    )(xu, mod, ng, win, qg, kg)


ROWS_PER_CHAIN = 2
ROWS_PER_STEP = 16
SCORE_LEAD = 1
UNION_ROWS = WIN_ROWS + ROWS_PER_CHAIN
UNION_KEYS = UNION_ROWS * GRID_W
BIAS_SLOTS = 2 * WIN_ROWS


def _nbr_attn_kernel(q_ref, k_ref, v_ref, z_ref, bias_ref, o_ref):
    lane = lax.broadcasted_iota(jnp.int32, (1, LANES), 1)
    first = lane < C_HD
    head_mask = (first, lane >= C_HD)
    ctx = slice(SEQ, T)

    def stack_heads(blocks):
        zero = jnp.zeros(blocks[0].shape, BF16)
        return jnp.concatenate([jnp.where(head_mask[hh], blk, zero) for blk in blocks for hh in range(2)], axis=0)

    def probabilities(s):
        m = jnp.max(s, axis=-1, keepdims=True)
        return jnp.exp2(s - m).astype(BF16)

    def finish(p, v_all, n_blocks, rows, out_rows):
        o = _normalise(_dot(p, _values_with_ones(v_all)))
        o = jnp.concatenate(
            [jnp.where(first, o[(2 * i) * rows:(2 * i + 1) * rows], o[(2 * i + 1) * rows:(2 * i + 2) * rows])
             for i in range(n_blocks)], axis=0)
        o_ref[0, out_rows, :] = (o * _silu(z_ref[0, out_rows, :])).astype(BF16)

    def chain_keys(r0):
        u0 = jnp.clip(r0 - WIN_ROWS // 2, 0, GRID_ROWS - UNION_ROWS)
        return u0, pl.ds(pl.multiple_of(u0 * GRID_W, 2 * GRID_W), UNION_KEYS)

    def query_rows(r0):
        return pl.ds(pl.multiple_of(r0 * GRID_W, ROWS_PER_CHAIN * GRID_W), ROWS_PER_CHAIN * GRID_W)

    def raw_scores(r0, k_ctx):
        q_pair = q_ref[0, query_rows(r0), :]
        q_rows = [q_pair[rr * GRID_W:(rr + 1) * GRID_W] for rr in range(ROWS_PER_CHAIN)]
        k_all = jnp.concatenate([k_ref[0, chain_keys(r0)[1], :], k_ctx], axis=0)
        return _dot_nt(stack_heads(q_rows), k_all)

    def biased_scores(r0, s):
        u0, _ = chain_keys(r0)
        blocks = []
        for rr in range(ROWS_PER_CHAIN):
            r = r0 + rr
            r_start = jnp.clip(r - WIN_ROWS // 2, 0, GRID_ROWS - WIN_ROWS)
            for hh in range(2):
                blk = s[(2 * rr + hh) * GRID_W:(2 * rr + hh + 1) * GRID_W]
                pieces = []
                for mm in range(UNION_ROWS // 2):
                    u = u0 + 2 * mm
                    ok_l = ((u >= r_start) & (u < r_start + WIN_ROWS)).astype(jnp.int32)
                    ok_r = ((u + 1 >= r_start) & (u + 1 < r_start + WIN_ROWS)).astype(jnp.int32)
                    slot = jnp.clip(u - r + WIN_ROWS, 0, BIAS_SLOTS - 1)
                    in_window = jnp.where(first, ok_l, ok_r) > 0
                    scores = blk[:, mm * LANES:(mm + 1) * LANES] + bias_ref[0, hh, slot]
                    pieces.append(jnp.where(in_window, scores, NEG))
                pieces.append(blk[:, UNION_KEYS:])
                blocks.append(jnp.concatenate(pieces, axis=1))
        return jnp.concatenate(blocks, axis=0)

    @pl.loop(0, GRID_ROWS // ROWS_PER_STEP)
    def _(t):
        k_ctx = k_ref[0, ctx, :]
        v_ctx = v_ref[0, ctx, :]
        firsts = [t * ROWS_PER_STEP + chain * ROWS_PER_CHAIN for chain in range(ROWS_PER_STEP // ROWS_PER_CHAIN)]
        raw = [raw_scores(r0, k_ctx) for r0 in firsts[:SCORE_LEAD]]
        for i, r0 in enumerate(firsts):
            p = probabilities(biased_scores(r0, raw[i]))
            if i + SCORE_LEAD < len(firsts):
                raw.append(raw_scores(firsts[i + SCORE_LEAD], k_ctx))
            v_all = jnp.concatenate([v_ref[0, chain_keys(r0)[1], :], v_ctx], axis=0)
            finish(p, v_all, ROWS_PER_CHAIN, GRID_W, query_rows(r0))

    p = probabilities(_dot_nt(stack_heads([q_ref[0, ctx, :]]), k_ref[0, ctx, :]))
    finish(p, v_ref[0, ctx, :], 1, CTX, ctx)


def _nbr_attn(q, k, v, z, bias):
    pair = pl.BlockSpec((1, T, LANES), lambda p, b: (b, 0, p))
    return pl.pallas_call(
        _nbr_attn_kernel,
        out_shape=jax.ShapeDtypeStruct((BATCH, T, C_WIDTH), BF16),
        grid=(C_HEADS // 2, BATCH),
        in_specs=[pair, pair, pair, pair,
                  pl.BlockSpec((1, 2, BIAS_SLOTS, GRID_W, LANES), lambda p, b: (p, 0, 0, 0, 0))],
        out_specs=pair,
        compiler_params=_params(("parallel", "parallel")),
        name="nbr_attn",
    )(q, k, v, z, bias)


def _odd_out_kernel(o_ref, w_ref, x_ref, mod_ref, y_ref):
    y_ref[0] = x_ref[0] + mod_ref[0, 2:3, :] * _dot(o_ref[0], w_ref[...])


def _odd_out(o, w, xu, mod, n_tiles):
    tok = lambda width: pl.BlockSpec((1, TILE, width), lambda b, t: (b, t, 0))
    return pl.pallas_call(
        _odd_out_kernel,
        out_shape=jax.ShapeDtypeStruct((BATCH, n_tiles * TILE, D), F32),
        grid=(BATCH, n_tiles),
        in_specs=[tok(C_WIDTH), _const_spec((C_WIDTH, D)), tok(D), pl.BlockSpec((1, 3, D), _mod_index)],
        out_specs=tok(D),
        compiler_params=_params(("parallel", "parallel")),
        name="odd_out",
    )(o, w, xu, mod)


HALF = A_ROPE // 2
ROPE_PERM = np.array([a * HALF + h * (HALF // 2) + f for h in range(2) for a in range(2) for f in range(HALF // 2)])


def _head_slot(nope, rope):
    pad = jnp.zeros(rope.shape[:-1] + (LANES - A_QK - HALF,), rope.dtype)
    return jnp.concatenate([nope, rope[..., ROPE_PERM], rope[..., ROPE_PERM[:HALF]], pad], axis=-1)


def _rope_tables():
    n = np.arange(SEQ)
    pos = np.stack([n // GRID_W, n % GRID_W], axis=1).astype(np.float32)
    inv = (ROPE_THETA ** (-np.arange(0, HALF, 2, dtype=np.float32) / HALF)).astype(np.float32)
    ang = (pos[:, :, None] * inv).astype(np.float64).reshape(SEQ, HALF)
    cos = np.ones((T, LANES), np.float32)
    sin = np.zeros((T, LANES), np.float32)
    cos[:, A_QK:] = 0.0
    cos[:SEQ, A_NOPE:A_QK] = np.tile(np.cos(ang), (1, 2))
    sin[:SEQ, A_NOPE:A_QK] = np.concatenate([-np.sin(ang), np.sin(ang)], axis=1)
    return jnp.asarray(cos), jnp.asarray(sin)


def _pad_lanes(v, width=LANES):
    return jnp.pad(v, (0, width - v.shape[0])).reshape(1, width)


def _even_weights(j, p, rope):
    w_in = p["ev_w_in"][j]
    q_l, kv_l, k_r, z_a, q_b, k_b, v_b, lr, z_b = jnp.split(
        w_in, [768, 1024, 1056, 1568, 1824, 2080, 2592, 2624], axis=1)
    zeros = jnp.zeros((D, 32), F32)
    misc = jnp.concatenate([lr, zeros, _head_slot(jnp.zeros((D, 0), F32), k_r)], axis=1)
    w_in2 = jnp.concatenate([q_l, kv_l, z_a, q_b, k_b, v_b, z_b, misc], axis=1).astype(BF16)
    w_uq = p["ev_w_uq"][j].reshape(Q_LORA, A_HEADS, A_QK)
    w_uq = _head_slot(w_uq[..., :A_NOPE], w_uq[..., A_NOPE:])
    w_ukv = p["ev_w_ukv"][j].reshape(KV_LORA, A_HEADS, A_NOPE + A_V)
    w_uk = jnp.pad(w_ukv[:, :, :A_NOPE], ((0, 0), (0, 0), (0, LANES - A_NOPE))).reshape(KV_LORA, A_HEADS * LANES)
    w_uv = w_ukv[:, :, A_NOPE:].reshape(KV_LORA, A_WIDTH)
    gate_w = p["ev_gate_w"][j]
    w_gate = jnp.zeros((LANES, 2 * B_KW), F32)
    w_gate = w_gate.at[:GATE_RANK, :B_KW].set(gate_w[0]).at[GATE_RANK:2 * GATE_RANK, B_KW:].set(gate_w[1])
    w_out = p["ev_w_out"][j].astype(BF16)
    return {
        "w_in": w_in2,
        "q_norm": p["ev_q_norm"][j].reshape(1, Q_LORA),
        "w_uq": w_uq.reshape(Q_LORA, A_HEADS * LANES).astype(BF16),
        "kv_norm": p["ev_kv_norm"][j].reshape(1, KV_LORA),
        "w_ukv": jnp.concatenate([w_uk, w_uv], axis=1).astype(BF16),
        "q_gain": _head_slot(p["ev_q_gain"][j][:A_NOPE], p["ev_q_gain"][j][A_NOPE:]).reshape(1, LANES),
        "k_gain": _head_slot(p["ev_k_gain"][j][:A_NOPE], p["ev_k_gain"][j][A_NOPE:]).reshape(1, LANES),
        "w_gate": w_gate.astype(BF16),
        "gate_b": p["ev_gate_b"][j].reshape(1, 2 * B_KW),
        "cos": rope[0], "sin": rope[1],
        "gla_norm": p["ev_gla_norm"][j].reshape(1, B_WIDTH),
        "w_out_a": w_out[:A_WIDTH],
        "w_out_b": w_out[A_WIDTH:],
    }


def _nbr_bias_table(rpb):
    col = np.arange(GRID_W)
    col_start = np.clip(col - WIN_COLS // 2, 0, GRID_W - WIN_COLS)
    col_mask = (col[None, :] >= col_start[:, None]) & (col[None, :] < col_start[:, None] + WIN_COLS)
    col_idx = np.clip(col[None, :] - col[:, None] + WIN_COLS - 1, 0, 2 * WIN_COLS - 2)
    n_rel = 2 * WIN_COLS - 1
    select = (col_idx.reshape(1, -1) == np.arange(n_rel)[:, None]).astype(np.float32)
    picked = jnp.dot(rpb.reshape(-1, n_rel), jnp.asarray(select), precision=lax.Precision.HIGHEST)
    picked = picked.reshape(C_HEADS, 2 * WIN_ROWS - 1, GRID_W, GRID_W)
    tab = jnp.where(jnp.asarray(col_mask)[None, None], picked * LOG2E, NEG)
    masked = jnp.full((C_HEADS, 1, GRID_W, GRID_W), NEG, F32)
    ext = jnp.concatenate([masked, tab, masked], axis=1)
    pairs = jnp.concatenate([ext[:, :-1], ext[:, 1:]], axis=-1)
    return pairs.reshape(C_HEADS // 2, 2, BIAS_SLOTS, GRID_W, LANES)


def kernel(x, c, ctx, c_ctx, norm_g, ada_w, ada_b, ev_w_in, ev_q_norm, ev_w_uq, ev_kv_norm, ev_w_ukv, ev_q_gain,
           ev_k_gain, ev_gate_w, ev_gate_b, ev_gla_norm, ev_w_out, od_w_in, od_q_gain, od_k_gain, od_rpb, od_w_out):
    p = dict(ev_w_in=ev_w_in, ev_q_norm=ev_q_norm, ev_w_uq=ev_w_uq, ev_kv_norm=ev_kv_norm, ev_w_ukv=ev_w_ukv,
             ev_q_gain=ev_q_gain, ev_k_gain=ev_k_gain, ev_gate_w=ev_gate_w, ev_gate_b=ev_gate_b,
             ev_gla_norm=ev_gla_norm, ev_w_out=ev_w_out)
    c16 = jnp.zeros((MOD_ROWS, D), F32).at[:BATCH].set(c).at[BATCH].set(c_ctx)
    mod_all = _modulation(c16, ada_w, ada_b).reshape(DEPTH, MOD_ROWS, 3, D)
    rope = _rope_tables()
    xu = None
    for i in range(DEPTH):
        j = i // 2
        mod = mod_all[i]
        ng = norm_g[i].reshape(1, D)
        if i % 2 == 0:
            w = _even_weights(j, p, rope)
            stream = (x, ctx, 0) if xu is None else (xu, xu, CTX_TILE)
            q, k, v, za, gq, gk, gv, gf, gb, zb = _even_in(stream, mod, ng, w)
            a = _mla_attn(q, k, v, za)
            gla = _gla(gq, gk, gv, gf, gb)
            xu = _even_out(a, gla, zb, w["gla_norm"], w["w_out_a"], w["w_out_b"], stream, mod)
        else:
            q, k, v, z = _odd_in(xu, mod, ng, od_w_in[j].astype(BF16), _pad_lanes(jnp.tile(od_q_gain[j], 2)),
                                 _pad_lanes(jnp.tile(od_k_gain[j], 2)))
            o = _nbr_attn(q, k, v, z, _nbr_bias_table(od_rpb[j]))
            n_tiles = N_TILES if i < DEPTH - 1 else CTX_TILE
            xu = _odd_out(o, od_w_out[j].astype(BF16), xu, mod, n_tiles)
    return xu
```

```python
import functools
from typing import Callable, NamedTuple

import jax
import jax.numpy as jnp
import numpy as np
from jax import lax
from jax.experimental import pallas as pl
from jax.experimental.pallas import tpu as pltpu

D = 1024
BATCH = 8
SEQ = 2048
DEPTH = 4
GRID_W = 64
GRID_ROWS = SEQ // GRID_W
CTX = 256
T = SEQ + CTX
EPS = 1e-6
ROPE_THETA = 10000.0

A_HEADS = 8
A_NOPE = 64
A_ROPE = 32
A_V = 64
A_QK = A_NOPE + A_ROPE
KV_LORA = 256
Q_LORA = 768
A_WIDTH = A_HEADS * A_V

B_HEADS = 4
B_DK = 64
B_DV = 128
B_KW = B_HEADS * B_DK
B_WIDTH = B_HEADS * B_DV
GATE_RANK = 16
GATE_TAU = 16.0
CHUNK = 64

C_HEADS = 16
C_HD = 64
C_WIDTH = C_HEADS * C_HD
WIN_ROWS = 8
WIN_COLS = 16

LANES = 128
TILE = 256
N_TILES = T // TILE
CTX_TILE = N_TILES - 1
MOD_ROWS = 16
EVEN_COLS = 3200
EVEN_EARLY = Q_LORA + KV_LORA + LANES
NEG = -1e30
LOG2E = 1.4426950408889634
VMEM_LIMIT = 56 * 1024 * 1024

F32 = jnp.float32
BF16 = jnp.bfloat16


def _dot(a, b):
    return lax.dot_general(a, b, (((1,), (0,)), ((), ())), preferred_element_type=F32)


def _dot_nt(a, b):
    return lax.dot_general(a, b, (((1,), (1,)), ((), ())), preferred_element_type=F32)


def _dot_tn(a, b):
    return lax.dot_general(a, b, (((0,), (0,)), ((), ())), preferred_element_type=F32)


def _silu(x):
    return x / (1.0 + jnp.exp(-x))


def _rms(x, n):
    return x * lax.rsqrt(jnp.sum(x * x, axis=-1, keepdims=True) * (1.0 / n) + EPS)


def _params(sem):
    return pltpu.CompilerParams(dimension_semantics=sem, vmem_limit_bytes=VMEM_LIMIT)


def _const_spec(shape):
    nd = len(shape)
    return pl.BlockSpec(shape, lambda *_: (0,) * nd)


def _mod_index(b, t):
    return (jnp.where(t == CTX_TILE, BATCH, b), 0, 0)


def _mod_kernel(c_ref, w_ref, b_ref, o_ref):
    a = _silu(c_ref[...]).astype(BF16)
    o_ref[0] = _dot(a, w_ref[0].astype(BF16)) + b_ref[0]


def _modulation(c16, ada_w, ada_b):
    return pl.pallas_call(
        _mod_kernel,
        out_shape=jax.ShapeDtypeStruct((DEPTH, MOD_ROWS, 3 * D), F32),
        grid=(DEPTH, 3),
        in_specs=[
            pl.BlockSpec((MOD_ROWS, D), lambda i, j: (0, 0)),
            pl.BlockSpec((1, D, D), lambda i, j: (i, 0, j)),
            pl.BlockSpec((1, 1, D), lambda i, j: (i, 0, j)),
        ],
        out_specs=pl.BlockSpec((1, MOD_ROWS, D), lambda i, j: (i, 0, j)),
        compiler_params=_params(("parallel", "parallel")),
        name="modulation",
    )(c16, ada_w, ada_b.reshape(DEPTH, 1, 3 * D))


def _modulated_norm(x, mod_ref, ng_ref):
    return _rms(x, D) * ng_ref[...] * (1.0 + mod_ref[0, 1:2, :]) + mod_ref[0, 0:1, :]


def _group_sumsq(u, group, counted):
    sq = (u * u).astype(BF16)
    width = 2 * LANES
    row = lax.broadcasted_iota(jnp.int32, (width, width), 0)
    col = lax.broadcasted_iota(jnp.int32, (width, width), 1)
    ones = ((row // group == col // group) & (row % group < counted)).astype(BF16)
    cols = [slice(c, c + width) for c in range(0, u.shape[1], width)]
    return jnp.concatenate([_dot(sq[:, c], ones) for c in cols], axis=1)


def _slot_sumsq(u):
    return _group_sumsq(u, LANES, A_QK)


def _stream_specs(ctx_block):
    return [pl.BlockSpec((1, TILE, D), lambda b, t: (b, jnp.minimum(t, CTX_TILE - 1), 0)),
            pl.BlockSpec((1, TILE, D), lambda b, t: (b, ctx_block, 0))]


def _stream_tile(lat_ref, ctx_ref):
    return jnp.where(pl.program_id(1) == CTX_TILE, ctx_ref[0], lat_ref[0])


class _Part(NamedTuple):
    arrays: tuple
    in_specs: tuple
    out_shapes: tuple
    out_specs: tuple
    body: Callable
    writes_stream: bool = False


def _row_spec(width):
    return pl.BlockSpec((1, TILE, width), lambda b, t: (b, t, 0))


def _boundary_call(name, finish, start, n_tiles=N_TILES):
    parts = [p for p in (finish, start) if p is not None]
    stream_shape = (jax.ShapeDtypeStruct((BATCH, n_tiles * TILE, D), F32),) if finish.writes_stream else ()
    stream_spec = (_row_spec(D),) if finish.writes_stream else ()
    n_finish_in = len(finish.arrays)
    n_in = sum(len(p.arrays) for p in parts)

    def kernel(*refs):
        ins, outs = refs[:n_in], refs[n_in:]
        x = finish.body(ins[:n_finish_in])
        if finish.writes_stream:
            outs[0][0] = x
        if start is not None:
            start.body(x, ins[n_finish_in:], outs[len(stream_shape):])

    out_shapes = stream_shape + (start.out_shapes if start is not None else ())
    out_specs = stream_spec + (start.out_specs if start is not None else ())
    return pl.pallas_call(
        kernel,
        out_shape=out_shapes,
        grid=(BATCH, n_tiles),
        in_specs=[s for p in parts for s in p.in_specs],
        out_specs=out_specs,
        compiler_params=_params(("parallel", "parallel")),
        name=name,
    )(*[a for p in parts for a in p.arrays])


def _source_part(x, ctx):
    return _Part((x, ctx), tuple(_stream_specs(0)), (), (), lambda ins: _stream_tile(*ins))


def _even_in_body(x, ins, outs):
    (mod_ref, ng_ref, win_ref, qn_ref, wuq_ref, kvn_ref, wukv_ref, qg_ref, kg_ref, wg_ref, gbias_ref,
     cos_ref, sin_ref) = ins
    q_ref, k_ref, v_ref, za_ref, gq_ref, gk_ref, gv_ref, gf_ref, gb_ref, zb_ref = outs
    h = _modulated_norm(x, mod_ref, ng_ref).astype(BF16)
    lat = _dot(h, win_ref[:, :EVEN_EARLY])
    q_lat = lat[:, 0:Q_LORA]
    kv_lat = lat[:, Q_LORA:Q_LORA + KV_LORA]
    misc = lat[:, Q_LORA + KV_LORA:]

    cos = cos_ref[...]
    sin = sin_ref[...]

    def rope(u):
        return u * cos + pltpu.roll(u, LANES - HALF, 1) * sin

    qf = _dot((_rms(q_lat, Q_LORA) * qn_ref[...]).astype(BF16), wuq_ref[...])
    kvf = _dot((_rms(kv_lat, KV_LORA) * kvn_ref[...]).astype(BF16), wukv_ref[...])

    logits = _dot(misc.astype(BF16), wg_ref[...]) + gbias_ref[...]
    g = (jnp.minimum(logits, 0.0) - jnp.log1p(jnp.exp(-jnp.abs(logits)))) * (1.0 / GATE_TAU)
    gf_ref[0] = g[:, :B_KW]
    gb_ref[0] = g[:, B_KW:]

    lane = lax.broadcasted_iota(jnp.int32, (1, LANES), 1)
    rotary = (lane >= A_NOPE) & (lane < A_QK)
    kr_roped = rope(jnp.where(lane >= A_NOPE, misc, 0.0) * kg_ref[...])
    k_rope = jnp.where(rotary, misc, 0.0)
    kr_ss = jnp.sum(k_rope * k_rope, axis=-1, keepdims=True)
    q_ss = _slot_sumsq(qf)
    k_ss = _slot_sumsq(kvf[:, :A_HEADS * LANES])

    late = _dot(h, win_ref[:, EVEN_EARLY:])
    za_ref[0] = late[:, 0:512]
    gq_ref[0] = late[:, 512:768] * (B_DK ** -0.5)
    gk_ref[0] = late[:, 768:1024]
    gv_ref[0] = late[:, 1024:1536].astype(BF16)
    zb_ref[0] = late[:, 1536:2048]

    for hh in range(A_HEADS):
        sl = slice(hh * LANES, (hh + 1) * LANES)
        q_scale = lax.rsqrt(q_ss[:, sl] * (1.0 / A_QK) + EPS) * (A_QK ** -0.5 * LOG2E)
        q_ref[0, hh] = (rope(qf[:, sl] * qg_ref[...]) * q_scale).astype(BF16)
        k_scale = lax.rsqrt((k_ss[:, sl] + kr_ss) * (1.0 / A_QK) + EPS)
        k_ref[0, hh] = ((kvf[:, sl] * kg_ref[...] + kr_roped) * k_scale).astype(BF16)
    v_ref[0] = kvf[:, A_HEADS * LANES:].astype(BF16)


def _even_in_part(mod, ng, w):
    tok = _row_spec
    head = pl.BlockSpec((1, A_HEADS, TILE, LANES), lambda b, t: (b, 0, t, 0))
    rope_spec = pl.BlockSpec((TILE, LANES), lambda b, t: (t, 0))
    sd = jax.ShapeDtypeStruct
    return _Part(
        arrays=(mod, ng, w["w_in"], w["q_norm"], w["w_uq"], w["kv_norm"], w["w_ukv"], w["q_gain"], w["k_gain"],
                w["w_gate"], w["gate_b"], w["cos"], w["sin"]),
        in_specs=(
            pl.BlockSpec((1, 3, D), _mod_index),
            _const_spec((1, D)),
            _const_spec((D, EVEN_COLS)),
            _const_spec((1, Q_LORA)),
            _const_spec((Q_LORA, A_HEADS * LANES)),
            _const_spec((1, KV_LORA)),
            _const_spec((KV_LORA, A_HEADS * LANES + A_WIDTH)),
            _const_spec((1, LANES)),
            _const_spec((1, LANES)),
            _const_spec((LANES, 2 * B_KW)),
            _const_spec((1, 2 * B_KW)),
            rope_spec, rope_spec,
        ),
        out_shapes=(
            sd((BATCH, A_HEADS, T, LANES), BF16),
            sd((BATCH, A_HEADS, T, LANES), BF16),
            sd((BATCH, T, A_WIDTH), BF16),
            sd((BATCH, T, A_WIDTH), F32),
            sd((BATCH, T, B_KW), F32),
            sd((BATCH, T, B_KW), F32),
            sd((BATCH, T, B_WIDTH), BF16),
            sd((BATCH, T, B_KW), F32),
            sd((BATCH, T, B_KW), F32),
            sd((BATCH, T, B_WIDTH), F32),
        ),
        out_specs=(head, head, tok(A_WIDTH), tok(A_WIDTH), tok(B_KW), tok(B_KW), tok(B_WIDTH),
                   tok(B_KW), tok(B_KW), tok(B_WIDTH)),
        body=_even_in_body,
    )


def _values_with_ones(v):
    return jnp.concatenate([v, jnp.ones(v.shape, BF16)], axis=1)


def _normalise(acc):
    return acc[:, :LANES] / acc[:, LANES:]


MLA_TILES_PER_STEP = 4


def _mla_attn_kernel(q_ref, k_ref, v_ref, z_ref, o_ref):
    first_head = lax.broadcasted_iota(jnp.int32, (1, LANES), 1) < A_V

    def attend(row_tiles, key_groups):
        rowmax = lambda s: functools.reduce(jnp.maximum, [jnp.max(x, axis=-1, keepdims=True) for x in s])
        chains = [(rows, hh) for rows in row_tiles for hh in range(2)]
        n, groups = len(chains), range(len(key_groups))
        scores, top, probs, acc = {}, {}, {}, {}
        for step in range(n + 2):
            if step >= 1 and step - 1 < n:
                top[step - 1] = rowmax(scores[step - 1])
            for i in groups:
                g = key_groups[i]
                if step < n:
                    rows, hh = chains[step]
                    scores.setdefault(step, []).append(_dot_nt(q_ref[0, hh, rows, :], k_ref[0, hh, g, :]))
                if 0 <= step - 1 < n:
                    probs.setdefault(step - 1, []).append(
                        jnp.exp2(scores[step - 1][i] - top[step - 1]).astype(BF16))
                if 0 <= step - 2 < n:
                    acc.setdefault(step - 2, []).append(_dot(probs[step - 2][i], _values_with_ones(v_ref[0, g, :])))
            done = step - 2
            if done >= 0 and done % 2 == 1:
                rows = chains[done][0]
                o0 = _normalise(functools.reduce(jnp.add, acc[done - 1]))
                o1 = _normalise(functools.reduce(jnp.add, acc[done]))
                o_ref[0, rows, :] = (jnp.where(first_head, o0, o1) * _silu(z_ref[0, rows, :])).astype(BF16)

    @pl.loop(0, CTX_TILE // MLA_TILES_PER_STEP)
    def _(t):
        base = t * (MLA_TILES_PER_STEP * TILE)
        tiles = [pl.ds(pl.multiple_of(base + i * TILE, TILE), TILE) for i in range(MLA_TILES_PER_STEP)]
        attend(tiles, [slice(0, 768), slice(768, 1536), slice(1536, T)])

    attend([slice(SEQ, T)], [slice(SEQ, T)])


def _mla_attn(q, k, v, za):
    pair = pl.BlockSpec((1, T, LANES), lambda b, p: (b, 0, p))
    heads = pl.BlockSpec((1, 2, T, LANES), lambda b, p: (b, p, 0, 0))
    return pl.pallas_call(
        _mla_attn_kernel,
        out_shape=jax.ShapeDtypeStruct((BATCH, T, A_WIDTH), BF16),
        grid=(BATCH, A_HEADS // 2),
        in_specs=[heads, heads, pair, pair],
        out_specs=pair,
        compiler_params=_params(("parallel", "parallel")),
        name="mla_attn",
    )(q, k, v, za)


def _gla_direction(q_ref, k_ref, v_ref, g_ref, o_ref, s_ref, tile, forward):
    rows = pl.ds(pl.multiple_of(tile * TILE, TILE), TILE)
    q = q_ref[0, rows, :]
    k = k_ref[0, rows, :]
    g = g_ref[0, rows, :]
    v = v_ref[0, rows, :]
    row = lax.broadcasted_iota(jnp.int32, (TILE, TILE), 0)
    col = lax.broadcasted_iota(jnp.int32, (TILE, TILE), 1)
    in_chunk_causal = ((row // CHUNK) == (col // CHUNK)) & ((col <= row) if forward else (col >= row))
    tri = in_chunk_causal.astype(BF16)
    g_hi = g.astype(BF16)
    g_lo = (g - g_hi.astype(F32)).astype(BF16)
    b = _dot(tri, g_hi) + _dot(tri, g_lo)

    n_chunks = TILE // CHUNK
    chunks = [slice(c * CHUNK, (c + 1) * CHUNK) for c in range(n_chunks)]
    last = CHUNK - 1 if forward else 0
    b_last = [b[c * CHUNK + last:c * CHUNK + last + 1] for c in range(n_chunks)]
    b_last_rows = jnp.concatenate([jnp.broadcast_to(bl, (CHUNK, LANES)) for bl in b_last], axis=0)
    q_t = q * jnp.exp(b)
    k_t = (k * jnp.exp(-b)).astype(BF16)
    k_dec = (k * jnp.exp(b_last_rows - b)).astype(BF16)
    yield

    lane = lax.broadcasted_iota(jnp.int32, (1, LANES), 1)
    intra = []
    for hh, head_lanes in enumerate((lane < B_DK, lane >= B_DK)):
        qm = jnp.where(head_lanes, q_t, 0.0).astype(BF16)
        att = jnp.where(in_chunk_causal, _dot_nt(qm, k_t), 0.0).astype(BF16)
        intra.append(_dot(att, v[:, hh * B_DV:(hh + 1) * B_DV]))
    yield

    srow = lax.broadcasted_iota(jnp.int32, (2 * B_DK, 2 * B_DV), 0)
    scol = lax.broadcasted_iota(jnp.int32, (2 * B_DK, 2 * B_DV), 1)
    own_head = (srow // B_DK) == (scol // B_DV)
    updates = [jnp.where(own_head, _dot_tn(k_dec[sl], v[sl]), 0.0) for sl in chunks]
    yield
    state = s_ref[...]
    starts = [None] * n_chunks
    for c in (range(n_chunks) if forward else reversed(range(n_chunks))):
        starts[c] = state.astype(BF16)
        decay = jnp.broadcast_to(jnp.exp(b_last[c]), (LANES, LANES)).T
        state = state * jnp.concatenate([decay, decay], axis=1) + updates[c]
    s_ref[...] = state
    yield
    q_tb = q_t.astype(BF16)
    inter = [_dot(q_tb[sl], starts[c]) for c, sl in enumerate(chunks)]
    o_ref[0, rows, :] += jnp.concatenate(intra, axis=1) + jnp.concatenate(inter, axis=0)


GLA_STEPS_PER_ITER = 3


def _gla_kernel(q_ref, k_ref, v_ref, gf_ref, gb_ref, o_ref, sf_ref, sb_ref):
    sf_ref[...] = jnp.zeros_like(sf_ref)
    sb_ref[...] = jnp.zeros_like(sb_ref)
    o_ref[...] = jnp.zeros_like(o_ref)

    @pl.loop(0, N_TILES // GLA_STEPS_PER_ITER)
    def _(it):
        stages = []
        for sub in range(GLA_STEPS_PER_ITER):
            step = it * GLA_STEPS_PER_ITER + sub
            fwd_tile = jnp.where(step == 0, CTX_TILE, step - 1)
            bwd_tile = CTX_TILE - step
            stages.append(_gla_direction(q_ref, k_ref, v_ref, gf_ref, o_ref, sf_ref, fwd_tile, True))
            stages.append(_gla_direction(q_ref, k_ref, v_ref, gb_ref, o_ref, sb_ref, bwd_tile, False))
        while stages:
            stages = [s for s in stages if next(s, StopIteration) is not StopIteration]


def _gla(gq, gk, gv, gf, gb):
    kspec = pl.BlockSpec((1, T, LANES), lambda b, p: (b, 0, p))
    vspec = pl.BlockSpec((1, T, 2 * B_DV), lambda b, p: (b, 0, p))
    out = jax.ShapeDtypeStruct((BATCH, T, B_WIDTH), F32)
    return pl.pallas_call(
        _gla_kernel,
        out_shape=out,
        grid=(BATCH, B_HEADS // 2),
        in_specs=[kspec, kspec, vspec, kspec, kspec],
        out_specs=vspec,
        scratch_shapes=[pltpu.VMEM((2 * B_DK, 2 * B_DV), F32), pltpu.VMEM((2 * B_DK, 2 * B_DV), F32)],
        compiler_params=_params(("parallel", "parallel")),
        name="gla_scan",
    )(gq, gk, gv, gf, gb)


def _even_out_body(ins):
    a_ref, gla_ref, zb_ref, gn_ref, wa_ref, wb_ref, xl_ref, xc_ref, mod_ref = ins
    bsum = gla_ref[0]
    zb = zb_ref[0]
    parts = []
    for hh in range(B_HEADS):
        sl = slice(hh * B_DV, (hh + 1) * B_DV)
        parts.append((_rms(bsum[:, sl], B_DV) * gn_ref[:, sl] * _silu(zb[:, sl])).astype(BF16))
    y = _dot(a_ref[0], wa_ref[...]) + _dot(jnp.concatenate(parts, axis=1), wb_ref[...])
    return _stream_tile(xl_ref, xc_ref) + mod_ref[0, 2:3, :] * y


def _even_out_part(a, gla, zb, gn, wa, wb, stream, mod):
    x_lat, x_ctx, ctx_block = stream
    tok = _row_spec
    return _Part(
        arrays=(a, gla, zb, gn, wa, wb, x_lat, x_ctx, mod),
        in_specs=(tok(A_WIDTH), tok(B_WIDTH), tok(B_WIDTH), _const_spec((1, B_WIDTH)),
                  _const_spec((A_WIDTH, D)), _const_spec((B_WIDTH, D)), *_stream_specs(ctx_block),
                  pl.BlockSpec((1, 3, D), _mod_index)),
        out_shapes=(), out_specs=(), body=_even_out_body, writes_stream=True)


def _odd_in_body(x, ins, outs):
    mod_ref, ng_ref, win_ref, qg_ref, kg_ref = ins
    q_ref, k_ref, v_ref, z_ref = outs
    h = _modulated_norm(x, mod_ref, ng_ref)
    u = _dot(h.astype(BF16), win_ref[...])
    ss = _group_sumsq(u[:, :2 * C_WIDTH], C_HD, C_HD)
    scale = lax.rsqrt(ss * (1.0 / C_HD) + EPS)
    for j in range(C_WIDTH // LANES):
        sl = slice(j * LANES, (j + 1) * LANES)
        q_ref[0, :, sl] = (u[:, sl] * scale[:, sl] * qg_ref[...] * (C_HD ** -0.5 * LOG2E)).astype(BF16)
        ks = slice(C_WIDTH + j * LANES, C_WIDTH + (j + 1) * LANES)
        k_ref[0, :, sl] = (u[:, ks] * scale[:, ks] * kg_ref[...]).astype(BF16)
    v_ref[0] = u[:, 2 * C_WIDTH:3 * C_WIDTH].astype(BF16)
    z_ref[0] = u[:, 3 * C_WIDTH:]


def _odd_in_part(mod, ng, win, qg, kg):
    tok = _row_spec
    sd = jax.ShapeDtypeStruct
    return _Part(
        arrays=(mod, ng, win, qg, kg),
        in_specs=(pl.BlockSpec((1, 3, D), _mod_index), _const_spec((1, D)), _const_spec((D, 4 * C_WIDTH)),
                  _const_spec((1, LANES)), _const_spec((1, LANES))),
        out_shapes=(sd((BATCH, T, C_WIDTH), BF16), sd((BATCH, T, C_WIDTH), BF16), sd((BATCH, T, C_WIDTH), BF16),
                    sd((BATCH, T, C_WIDTH), F32)),
        out_specs=(tok(C_WIDTH), tok(C_WIDTH), tok(C_WIDTH), tok(C_WIDTH)),
        body=_odd_in_body)


ROWS_PER_CHAIN = 2
ROWS_PER_STEP = 16
SCORE_LEAD = 1
UNION_ROWS = WIN_ROWS + ROWS_PER_CHAIN
UNION_KEYS = UNION_ROWS * GRID_W
BIAS_SLOTS = 2 * WIN_ROWS


def _nbr_attn_kernel(q_ref, k_ref, v_ref, z_ref, bias_ref, o_ref):
    lane = lax.broadcasted_iota(jnp.int32, (1, LANES), 1)
    first = lane < C_HD
    head_mask = (first, lane >= C_HD)
    ctx = slice(SEQ, T)

    def stack_heads(blocks):
        zero = jnp.zeros(blocks[0].shape, BF16)
        return jnp.concatenate([jnp.where(head_mask[hh], blk, zero) for blk in blocks for hh in range(2)], axis=0)

    def probabilities(s):
        m = jnp.max(s, axis=-1, keepdims=True)
        return jnp.exp2(s - m).astype(BF16)

    def finish(p, v_all, n_blocks, rows, out_rows):
        o = _normalise(_dot(p, _values_with_ones(v_all)))
        o = jnp.concatenate(
            [jnp.where(first, o[(2 * i) * rows:(2 * i + 1) * rows], o[(2 * i + 1) * rows:(2 * i + 2) * rows])
             for i in range(n_blocks)], axis=0)
        o_ref[0, out_rows, :] = (o * _silu(z_ref[0, out_rows, :])).astype(BF16)

    def chain_keys(r0):
        u0 = jnp.clip(r0 - WIN_ROWS // 2, 0, GRID_ROWS - UNION_ROWS)
        return u0, pl.ds(pl.multiple_of(u0 * GRID_W, 2 * GRID_W), UNION_KEYS)

    def query_rows(r0):
        return pl.ds(pl.multiple_of(r0 * GRID_W, ROWS_PER_CHAIN * GRID_W), ROWS_PER_CHAIN * GRID_W)

    def raw_scores(r0, k_ctx):
        q_pair = q_ref[0, query_rows(r0), :]
        q_rows = [q_pair[rr * GRID_W:(rr + 1) * GRID_W] for rr in range(ROWS_PER_CHAIN)]
        k_all = jnp.concatenate([k_ref[0, chain_keys(r0)[1], :], k_ctx], axis=0)
        return _dot_nt(stack_heads(q_rows), k_all)

    def biased_scores(r0, s):
        u0, _ = chain_keys(r0)
        blocks = []
        for rr in range(ROWS_PER_CHAIN):
            r = r0 + rr
            r_start = jnp.clip(r - WIN_ROWS // 2, 0, GRID_ROWS - WIN_ROWS)
            for hh in range(2):
                blk = s[(2 * rr + hh) * GRID_W:(2 * rr + hh + 1) * GRID_W]
                pieces = []
                for mm in range(UNION_ROWS // 2):
                    u = u0 + 2 * mm
                    ok_l = ((u >= r_start) & (u < r_start + WIN_ROWS)).astype(jnp.int32)
                    ok_r = ((u + 1 >= r_start) & (u + 1 < r_start + WIN_ROWS)).astype(jnp.int32)
                    slot = jnp.clip(u - r + WIN_ROWS, 0, BIAS_SLOTS - 1)
                    in_window = jnp.where(first, ok_l, ok_r) > 0
                    scores = blk[:, mm * LANES:(mm + 1) * LANES] + bias_ref[0, hh, slot]
                    pieces.append(jnp.where(in_window, scores, NEG))
                pieces.append(blk[:, UNION_KEYS:])
                blocks.append(jnp.concatenate(pieces, axis=1))
        return jnp.concatenate(blocks, axis=0)

    @pl.loop(0, GRID_ROWS // ROWS_PER_STEP)
    def _(t):
        k_ctx = k_ref[0, ctx, :]
        v_ctx = v_ref[0, ctx, :]
        firsts = [t * ROWS_PER_STEP + chain * ROWS_PER_CHAIN for chain in range(ROWS_PER_STEP // ROWS_PER_CHAIN)]
        raw = [raw_scores(r0, k_ctx) for r0 in firsts[:SCORE_LEAD]]
        for i, r0 in enumerate(firsts):
            p = probabilities(biased_scores(r0, raw[i]))
            if i + SCORE_LEAD < len(firsts):
                raw.append(raw_scores(firsts[i + SCORE_LEAD], k_ctx))
            v_all = jnp.concatenate([v_ref[0, chain_keys(r0)[1], :], v_ctx], axis=0)
            finish(p, v_all, ROWS_PER_CHAIN, GRID_W, query_rows(r0))

    p = probabilities(_dot_nt(stack_heads([q_ref[0, ctx, :]]), k_ref[0, ctx, :]))
    finish(p, v_ref[0, ctx, :], 1, CTX, ctx)


def _nbr_attn(q, k, v, z, bias):
    pair = pl.BlockSpec((1, T, LANES), lambda p, b: (b, 0, p))
    return pl.pallas_call(
        _nbr_attn_kernel,
        out_shape=jax.ShapeDtypeStruct((BATCH, T, C_WIDTH), BF16),
        grid=(C_HEADS // 2, BATCH),
        in_specs=[pair, pair, pair, pair,
                  pl.BlockSpec((1, 2, BIAS_SLOTS, GRID_W, LANES), lambda p, b: (p, 0, 0, 0, 0))],
        out_specs=pair,
        compiler_params=_params(("parallel", "parallel")),
        name="nbr_attn",
    )(q, k, v, z, bias)


def _odd_out_body(ins):
    o_ref, w_ref, x_ref, mod_ref = ins
    return x_ref[0] + mod_ref[0, 2:3, :] * _dot(o_ref[0], w_ref[...])


def _odd_out_part(o, w, xu, mod):
    return _Part(
        arrays=(o, w, xu, mod),
        in_specs=(_row_spec(C_WIDTH), _const_spec((C_WIDTH, D)), _row_spec(D), pl.BlockSpec((1, 3, D), _mod_index)),
        out_shapes=(), out_specs=(), body=_odd_out_body, writes_stream=True)


HALF = A_ROPE // 2
ROPE_PERM = np.array([a * HALF + h * (HALF // 2) + f for h in range(2) for a in range(2) for f in range(HALF // 2)])


def _head_slot(nope, rope):
    pad = jnp.zeros(rope.shape[:-1] + (LANES - A_QK - HALF,), rope.dtype)
    return jnp.concatenate([nope, rope[..., ROPE_PERM], rope[..., ROPE_PERM[:HALF]], pad], axis=-1)


def _rope_tables():
    n = np.arange(SEQ)
    pos = np.stack([n // GRID_W, n % GRID_W], axis=1).astype(np.float32)
    inv = (ROPE_THETA ** (-np.arange(0, HALF, 2, dtype=np.float32) / HALF)).astype(np.float32)
    ang = (pos[:, :, None] * inv).astype(np.float64).reshape(SEQ, HALF)
    cos = np.ones((T, LANES), np.float32)
    sin = np.zeros((T, LANES), np.float32)
    cos[:, A_QK:] = 0.0
    cos[:SEQ, A_NOPE:A_QK] = np.tile(np.cos(ang), (1, 2))
    sin[:SEQ, A_NOPE:A_QK] = np.concatenate([-np.sin(ang), np.sin(ang)], axis=1)
    return jnp.asarray(cos), jnp.asarray(sin)


def _pad_lanes(v, width=LANES):
    return jnp.pad(v, (0, width - v.shape[0])).reshape(1, width)


def _even_weights(j, p, rope):
    w_in = p["ev_w_in"][j]
    q_l, kv_l, k_r, z_a, q_b, k_b, v_b, lr, z_b = jnp.split(
        w_in, [768, 1024, 1056, 1568, 1824, 2080, 2592, 2624], axis=1)
    zeros = jnp.zeros((D, 32), F32)
    misc = jnp.concatenate([lr, zeros, _head_slot(jnp.zeros((D, 0), F32), k_r)], axis=1)
    w_in2 = jnp.concatenate([q_l, kv_l, misc, z_a, q_b, k_b, v_b, z_b], axis=1).astype(BF16)
    w_uq = p["ev_w_uq"][j].reshape(Q_LORA, A_HEADS, A_QK)
    w_uq = _head_slot(w_uq[..., :A_NOPE], w_uq[..., A_NOPE:])
    w_ukv = p["ev_w_ukv"][j].reshape(KV_LORA, A_HEADS, A_NOPE + A_V)
    w_uk = jnp.pad(w_ukv[:, :, :A_NOPE], ((0, 0), (0, 0), (0, LANES - A_NOPE))).reshape(KV_LORA, A_HEADS * LANES)
    w_uv = w_ukv[:, :, A_NOPE:].reshape(KV_LORA, A_WIDTH)
    gate_w = p["ev_gate_w"][j]
    w_gate = jnp.zeros((LANES, 2 * B_KW), F32)
    w_gate = w_gate.at[:GATE_RANK, :B_KW].set(gate_w[0]).at[GATE_RANK:2 * GATE_RANK, B_KW:].set(gate_w[1])
    w_out = p["ev_w_out"][j].astype(BF16)
    return {
        "w_in": w_in2,
        "q_norm": p["ev_q_norm"][j].reshape(1, Q_LORA),
        "w_uq": w_uq.reshape(Q_LORA, A_HEADS * LANES).astype(BF16),
        "kv_norm": p["ev_kv_norm"][j].reshape(1, KV_LORA),
        "w_ukv": jnp.concatenate([w_uk, w_uv], axis=1).astype(BF16),
        "q_gain": _head_slot(p["ev_q_gain"][j][:A_NOPE], p["ev_q_gain"][j][A_NOPE:]).reshape(1, LANES),
        "k_gain": _head_slot(p["ev_k_gain"][j][:A_NOPE], p["ev_k_gain"][j][A_NOPE:]).reshape(1, LANES),
        "w_gate": w_gate.astype(BF16),
        "gate_b": p["ev_gate_b"][j].reshape(1, 2 * B_KW),
        "cos": rope[0], "sin": rope[1],
        "gla_norm": p["ev_gla_norm"][j].reshape(1, B_WIDTH),
        "w_out_a": w_out[:A_WIDTH],
        "w_out_b": w_out[A_WIDTH:],
    }


def _nbr_bias_table(rpb):
    col = np.arange(GRID_W)
    col_start = np.clip(col - WIN_COLS // 2, 0, GRID_W - WIN_COLS)
    col_mask = (col[None, :] >= col_start[:, None]) & (col[None, :] < col_start[:, None] + WIN_COLS)
    col_idx = np.clip(col[None, :] - col[:, None] + WIN_COLS - 1, 0, 2 * WIN_COLS - 2)
    n_rel = 2 * WIN_COLS - 1
    select = (col_idx.reshape(1, -1) == np.arange(n_rel)[:, None]).astype(np.float32)
    picked = jnp.dot(rpb.reshape(-1, n_rel), jnp.asarray(select), precision=lax.Precision.HIGHEST)
    picked = picked.reshape(C_HEADS, 2 * WIN_ROWS - 1, GRID_W, GRID_W)
    tab = jnp.where(jnp.asarray(col_mask)[None, None], picked * LOG2E, NEG)
    masked = jnp.full((C_HEADS, 1, GRID_W, GRID_W), NEG, F32)
    ext = jnp.concatenate([masked, tab, masked], axis=1)
    pairs = jnp.concatenate([ext[:, :-1], ext[:, 1:]], axis=-1)
    return pairs.reshape(C_HEADS // 2, 2, BIAS_SLOTS, GRID_W, LANES)


def kernel(x, c, ctx, c_ctx, norm_g, ada_w, ada_b, ev_w_in, ev_q_norm, ev_w_uq, ev_kv_norm, ev_w_ukv, ev_q_gain,
           ev_k_gain, ev_gate_w, ev_gate_b, ev_gla_norm, ev_w_out, od_w_in, od_q_gain, od_k_gain, od_rpb, od_w_out):
    p = dict(ev_w_in=ev_w_in, ev_q_norm=ev_q_norm, ev_w_uq=ev_w_uq, ev_kv_norm=ev_kv_norm, ev_w_ukv=ev_w_ukv,
             ev_q_gain=ev_q_gain, ev_k_gain=ev_k_gain, ev_gate_w=ev_gate_w, ev_gate_b=ev_gate_b,
             ev_gla_norm=ev_gla_norm, ev_w_out=ev_w_out)
    c16 = jnp.zeros((MOD_ROWS, D), F32).at[:BATCH].set(c).at[BATCH].set(c_ctx)
    mod_all = _modulation(c16, ada_w, ada_b).reshape(DEPTH, MOD_ROWS, 3, D)
    rope = _rope_tables()
    xu = None
    finish, finish_name = _source_part(x, ctx), "inputs"
    for i in range(DEPTH):
        j = i // 2
        mod = mod_all[i]
        ng = norm_g[i].reshape(1, D)
        if i % 2 == 0:
            w = _even_weights(j, p, rope)
            start, start_name = _even_in_part(mod, ng, w), "even_in"
        else:
            start = _odd_in_part(mod, ng, od_w_in[j].astype(BF16), _pad_lanes(jnp.tile(od_q_gain[j], 2)),
                                 _pad_lanes(jnp.tile(od_k_gain[j], 2)))
            start_name = "odd_in"
        outs = _boundary_call(f"{finish_name}_{start_name}", finish, start)
        if finish.writes_stream:
            xu, outs = outs[0], outs[1:]
        if i % 2 == 0:
            q, k, v, za, gq, gk, gv, gf, gb, zb = outs
            a = _mla_attn(q, k, v, za)
            gla = _gla(gq, gk, gv, gf, gb)
            stream = (x, ctx, 0) if xu is None else (xu, xu, CTX_TILE)
            finish = _even_out_part(a, gla, zb, w["gla_norm"], w["w_out_a"], w["w_out_b"], stream, mod)
            finish_name = "even_out"
        else:
            q, k, v, z = outs
            o = _nbr_attn(q, k, v, z, _nbr_bias_table(od_rpb[j]))
            finish, finish_name = _odd_out_part(o, od_w_out[j].astype(BF16), xu, mod), "odd_out"
    return _boundary_call(finish_name, finish, None, n_tiles=CTX_TILE)[0]
```

```python
import functools
from typing import Callable, NamedTuple

import jax
import jax.numpy as jnp
import numpy as np
from jax import lax
from jax.experimental import pallas as pl
from jax.experimental.pallas import tpu as pltpu

D = 1024
BATCH = 8
SEQ = 2048
DEPTH = 4
GRID_W = 64
GRID_ROWS = SEQ // GRID_W
CTX = 256
T = SEQ + CTX
EPS = 1e-6
ROPE_THETA = 10000.0

A_HEADS = 8
A_NOPE = 64
A_ROPE = 32
A_V = 64
A_QK = A_NOPE + A_ROPE
KV_LORA = 256
Q_LORA = 768
A_WIDTH = A_HEADS * A_V

B_HEADS = 4
B_DK = 64
B_DV = 128
B_KW = B_HEADS * B_DK
B_WIDTH = B_HEADS * B_DV
GATE_RANK = 16
GATE_TAU = 16.0
CHUNK = 64

C_HEADS = 16
C_HD = 64
C_WIDTH = C_HEADS * C_HD
WIN_ROWS = 8
WIN_COLS = 16

LANES = 128
TILE = 256
N_TILES = T // TILE
CTX_TILE = N_TILES - 1
MOD_ROWS = 16
EVEN_COLS = 3200
EVEN_EARLY = Q_LORA + KV_LORA + LANES
NEG = -1e30
LOG2E = 1.4426950408889634
VMEM_LIMIT = 56 * 1024 * 1024

F32 = jnp.float32
BF16 = jnp.bfloat16


def _dot(a, b):
    return lax.dot_general(a, b, (((1,), (0,)), ((), ())), preferred_element_type=F32)


def _dot_nt(a, b):
    return lax.dot_general(a, b, (((1,), (1,)), ((), ())), preferred_element_type=F32)


def _dot_tn(a, b):
    return lax.dot_general(a, b, (((0,), (0,)), ((), ())), preferred_element_type=F32)


def _silu(x):
    return x / (1.0 + jnp.exp(-x))


def _rms(x, n):
    return x * lax.rsqrt(jnp.sum(x * x, axis=-1, keepdims=True) * (1.0 / n) + EPS)


def _params(sem):
    return pltpu.CompilerParams(dimension_semantics=sem, vmem_limit_bytes=VMEM_LIMIT)


def _const_spec(shape):
    nd = len(shape)
    return pl.BlockSpec(shape, lambda *_: (0,) * nd)


def _mod_index(b, t):
    return (jnp.where(t == CTX_TILE, BATCH, b), 0, 0)


def _mod_kernel(c_ref, w_ref, b_ref, o_ref):
    a = _silu(c_ref[...]).astype(BF16)
    o_ref[0] = _dot(a, w_ref[0].astype(BF16)) + b_ref[0]


def _modulation(c16, ada_w, ada_b):
    return pl.pallas_call(
        _mod_kernel,
        out_shape=jax.ShapeDtypeStruct((DEPTH, MOD_ROWS, 3 * D), F32),
        grid=(DEPTH, 3),
        in_specs=[
            pl.BlockSpec((MOD_ROWS, D), lambda i, j: (0, 0)),
            pl.BlockSpec((1, D, D), lambda i, j: (i, 0, j)),
            pl.BlockSpec((1, 1, D), lambda i, j: (i, 0, j)),
        ],
        out_specs=pl.BlockSpec((1, MOD_ROWS, D), lambda i, j: (i, 0, j)),
        compiler_params=_params(("parallel", "parallel")),
        name="modulation",
    )(c16, ada_w, ada_b.reshape(DEPTH, 1, 3 * D))


def _modulated_norm(x, mod_ref, ng_ref):
    return _rms(x, D) * ng_ref[...] * (1.0 + mod_ref[0, 1:2, :]) + mod_ref[0, 0:1, :]


def _group_sumsq(u, group, counted):
    sq = (u * u).astype(BF16)
    width = 2 * LANES
    row = lax.broadcasted_iota(jnp.int32, (width, width), 0)
    col = lax.broadcasted_iota(jnp.int32, (width, width), 1)
    ones = ((row // group == col // group) & (row % group < counted)).astype(BF16)
    cols = [slice(c, c + width) for c in range(0, u.shape[1], width)]
    return jnp.concatenate([_dot(sq[:, c], ones) for c in cols], axis=1)


def _slot_sumsq(u):
    return _group_sumsq(u, LANES, A_QK)


def _stream_specs(ctx_block):
    return [pl.BlockSpec((1, TILE, D), lambda b, t: (b, jnp.minimum(t, CTX_TILE - 1), 0)),
            pl.BlockSpec((1, TILE, D), lambda b, t: (b, ctx_block, 0))]


def _stream_tile(lat_ref, ctx_ref):
    return jnp.where(pl.program_id(1) == CTX_TILE, ctx_ref[0], lat_ref[0])


class _Part(NamedTuple):
    arrays: tuple
    in_specs: tuple
    out_shapes: tuple
    out_specs: tuple
    body: Callable
    writes_stream: bool = False


def _row_spec(width):
    return pl.BlockSpec((1, TILE, width), lambda b, t: (b, t, 0))


def _boundary_call(name, finish, start, n_tiles=N_TILES):
    parts = [p for p in (finish, start) if p is not None]
    stream_shape = (jax.ShapeDtypeStruct((BATCH, n_tiles * TILE, D), F32),) if finish.writes_stream else ()
    stream_spec = (_row_spec(D),) if finish.writes_stream else ()
    n_finish_in = len(finish.arrays)
    n_in = sum(len(p.arrays) for p in parts)

    def kernel(*refs):
        ins, outs = refs[:n_in], refs[n_in:]
        x = finish.body(ins[:n_finish_in])
        if finish.writes_stream:
            outs[0][0] = x
        if start is not None:
            start.body(x, ins[n_finish_in:], outs[len(stream_shape):])

    out_shapes = stream_shape + (start.out_shapes if start is not None else ())
    out_specs = stream_spec + (start.out_specs if start is not None else ())
    return pl.pallas_call(
        kernel,
        out_shape=out_shapes,
        grid=(BATCH, n_tiles),
        in_specs=[s for p in parts for s in p.in_specs],
        out_specs=out_specs,
        compiler_params=_params(("parallel", "parallel")),
        name=name,
    )(*[a for p in parts for a in p.arrays])


def _source_part(x, ctx):
    return _Part((x, ctx), tuple(_stream_specs(0)), (), (), lambda ins: _stream_tile(*ins))


def _even_in_body(x, ins, outs):
    (mod_ref, ng_ref, win_ref, qn_ref, wuq_ref, kvn_ref, wukv_ref, qg_ref, kg_ref, wg_ref, gbias_ref,
     cos_ref, sin_ref) = ins
    q_ref, k_ref, v_ref, za_ref, gq_ref, gk_ref, gv_ref, gf_ref, gb_ref, zb_ref = outs
    h = _modulated_norm(x, mod_ref, ng_ref).astype(BF16)
    lat = _dot(h, win_ref[:, :EVEN_EARLY])
    q_lat = lat[:, 0:Q_LORA]
    kv_lat = lat[:, Q_LORA:Q_LORA + KV_LORA]
    misc = lat[:, Q_LORA + KV_LORA:]

    cos = cos_ref[...]
    sin = sin_ref[...]

    def rope(u):
        return u * cos + pltpu.roll(u, LANES - HALF, 1) * sin

    qf = _dot((_rms(q_lat, Q_LORA) * qn_ref[...]).astype(BF16), wuq_ref[...])
    kvf = _dot((_rms(kv_lat, KV_LORA) * kvn_ref[...]).astype(BF16), wukv_ref[...])

    logits = _dot(misc.astype(BF16), wg_ref[...]) + gbias_ref[...]
    g = (jnp.minimum(logits, 0.0) - jnp.log1p(jnp.exp(-jnp.abs(logits)))) * (1.0 / GATE_TAU)
    gf_ref[0] = g[:, :B_KW]
    gb_ref[0] = g[:, B_KW:]

    lane = lax.broadcasted_iota(jnp.int32, (1, LANES), 1)
    rotary = (lane >= A_NOPE) & (lane < A_QK)
    kr_roped = rope(jnp.where(lane >= A_NOPE, misc, 0.0) * kg_ref[...])
    k_rope = jnp.where(rotary, misc, 0.0)
    kr_ss = jnp.sum(k_rope * k_rope, axis=-1, keepdims=True)
    q_ss = _slot_sumsq(qf)
    k_ss = _slot_sumsq(kvf[:, :A_HEADS * LANES])

    late = _dot(h, win_ref[:, EVEN_EARLY:])
    za_ref[0] = late[:, 0:512]
    gq_ref[0] = late[:, 512:768] * (B_DK ** -0.5)
    gk_ref[0] = late[:, 768:1024]
    gv_ref[0] = late[:, 1024:1536].astype(BF16)
    zb_ref[0] = late[:, 1536:2048]

    for hh in range(A_HEADS):
        sl = slice(hh * LANES, (hh + 1) * LANES)
        q_scale = lax.rsqrt(q_ss[:, sl] * (1.0 / A_QK) + EPS) * (A_QK ** -0.5 * LOG2E)
        q_ref[0, hh] = (rope(qf[:, sl] * qg_ref[...]) * q_scale).astype(BF16)
        k_scale = lax.rsqrt((k_ss[:, sl] + kr_ss) * (1.0 / A_QK) + EPS)
        k_ref[0, hh] = ((kvf[:, sl] * kg_ref[...] + kr_roped) * k_scale).astype(BF16)
    v_ref[0] = kvf[:, A_HEADS * LANES:].astype(BF16)


def _even_in_part(mod, ng, w):
    tok = _row_spec
    head = pl.BlockSpec((1, A_HEADS, TILE, LANES), lambda b, t: (b, 0, t, 0))
    rope_spec = pl.BlockSpec((TILE, LANES), lambda b, t: (t, 0))
    sd = jax.ShapeDtypeStruct
    return _Part(
        arrays=(mod, ng, w["w_in"], w["q_norm"], w["w_uq"], w["kv_norm"], w["w_ukv"], w["q_gain"], w["k_gain"],
                w["w_gate"], w["gate_b"], w["cos"], w["sin"]),
        in_specs=(
            pl.BlockSpec((1, 3, D), _mod_index),
            _const_spec((1, D)),
            _const_spec((D, EVEN_COLS)),
            _const_spec((1, Q_LORA)),
            _const_spec((Q_LORA, A_HEADS * LANES)),
            _const_spec((1, KV_LORA)),
            _const_spec((KV_LORA, A_HEADS * LANES + A_WIDTH)),
            _const_spec((1, LANES)),
            _const_spec((1, LANES)),
            _const_spec((LANES, 2 * B_KW)),
            _const_spec((1, 2 * B_KW)),
            rope_spec, rope_spec,
        ),
        out_shapes=(
            sd((BATCH, A_HEADS, T, LANES), BF16),
            sd((BATCH, A_HEADS, T, LANES), BF16),
            sd((BATCH, T, A_WIDTH), BF16),
            sd((BATCH, T, A_WIDTH), F32),
            sd((BATCH, T, B_KW), F32),
            sd((BATCH, T, B_KW), F32),
            sd((BATCH, T, B_WIDTH), BF16),
            sd((BATCH, T, B_KW), F32),
            sd((BATCH, T, B_KW), F32),
            sd((BATCH, T, B_WIDTH), F32),
        ),
        out_specs=(head, head, tok(A_WIDTH), tok(A_WIDTH), tok(B_KW), tok(B_KW), tok(B_WIDTH),
                   tok(B_KW), tok(B_KW), tok(B_WIDTH)),
        body=_even_in_body,
    )


def _values_with_ones(v):
    return jnp.concatenate([v, jnp.ones(v.shape, BF16)], axis=1)


def _normalise(acc):
    return acc[:, :LANES] / acc[:, LANES:]


MLA_TILES_PER_STEP = 4


def _mla_attn_kernel(q_ref, k_ref, v_ref, z_ref, o_ref):
    first_head = lax.broadcasted_iota(jnp.int32, (1, LANES), 1) < A_V

    def attend(row_tiles, key_groups):
        rowmax = lambda s: functools.reduce(jnp.maximum, [jnp.max(x, axis=-1, keepdims=True) for x in s])
        chains = [(rows, hh) for rows in row_tiles for hh in range(2)]
        n, groups = len(chains), range(len(key_groups))
        scores, top, probs, acc = {}, {}, {}, {}
        for step in range(n + 2):
            if step >= 1 and step - 1 < n:
                top[step - 1] = rowmax(scores[step - 1])
            for i in groups:
                g = key_groups[i]
                if step < n:
                    rows, hh = chains[step]
                    scores.setdefault(step, []).append(_dot_nt(q_ref[0, hh, rows, :], k_ref[0, hh, g, :]))
                if 0 <= step - 1 < n:
                    probs.setdefault(step - 1, []).append(
                        jnp.exp2(scores[step - 1][i] - top[step - 1]).astype(BF16))
                if 0 <= step - 2 < n:
                    acc.setdefault(step - 2, []).append(_dot(probs[step - 2][i], _values_with_ones(v_ref[0, g, :])))
            done = step - 2
            if done >= 0 and done % 2 == 1:
                rows = chains[done][0]
                o0 = _normalise(functools.reduce(jnp.add, acc[done - 1]))
                o1 = _normalise(functools.reduce(jnp.add, acc[done]))
                o_ref[0, rows, :] = (jnp.where(first_head, o0, o1) * _silu(z_ref[0, rows, :])).astype(BF16)

    @pl.loop(0, CTX_TILE // MLA_TILES_PER_STEP)
    def _(t):
        base = t * (MLA_TILES_PER_STEP * TILE)
        tiles = [pl.ds(pl.multiple_of(base + i * TILE, TILE), TILE) for i in range(MLA_TILES_PER_STEP)]
        attend(tiles, [slice(0, 768), slice(768, 1536), slice(1536, T)])

    attend([slice(SEQ, T)], [slice(SEQ, T)])


def _mla_attn(q, k, v, za):
    pair = pl.BlockSpec((1, T, LANES), lambda b, p: (b, 0, p))
    heads = pl.BlockSpec((1, 2, T, LANES), lambda b, p: (b, p, 0, 0))
    return pl.pallas_call(
        _mla_attn_kernel,
        out_shape=jax.ShapeDtypeStruct((BATCH, T, A_WIDTH), BF16),
        grid=(BATCH, A_HEADS // 2),
        in_specs=[heads, heads, pair, pair],
        out_specs=pair,
        compiler_params=_params(("parallel", "parallel")),
        name="mla_attn",
    )(q, k, v, za)


def _gla_direction(q_ref, k_ref, v_ref, g_ref, o_ref, s_ref, tile, forward):
    rows = pl.ds(pl.multiple_of(tile * TILE, TILE), TILE)
    q = q_ref[0, rows, :]
    k = k_ref[0, rows, :]
    g = g_ref[0, rows, :]
    v = v_ref[0, rows, :]
    row = lax.broadcasted_iota(jnp.int32, (TILE, TILE), 0)
    col = lax.broadcasted_iota(jnp.int32, (TILE, TILE), 1)
    in_chunk_causal = ((row // CHUNK) == (col // CHUNK)) & ((col <= row) if forward else (col >= row))
    tri = in_chunk_causal.astype(BF16)
    g_hi = g.astype(BF16)
    g_lo = (g - g_hi.astype(F32)).astype(BF16)
    b = _dot(tri, g_hi) + _dot(tri, g_lo)

    n_chunks = TILE // CHUNK
    chunks = [slice(c * CHUNK, (c + 1) * CHUNK) for c in range(n_chunks)]
    last = CHUNK - 1 if forward else 0
    b_last = [b[c * CHUNK + last:c * CHUNK + last + 1] for c in range(n_chunks)]
    b_last_rows = jnp.concatenate([jnp.broadcast_to(bl, (CHUNK, LANES)) for bl in b_last], axis=0)
    q_t = q * jnp.exp(b)
    k_t = (k * jnp.exp(-b)).astype(BF16)
    k_dec = (k * jnp.exp(b_last_rows - b)).astype(BF16)
    yield

    lane = lax.broadcasted_iota(jnp.int32, (1, LANES), 1)
    intra = []
    for hh, head_lanes in enumerate((lane < B_DK, lane >= B_DK)):
        qm = jnp.where(head_lanes, q_t, 0.0).astype(BF16)
        att = jnp.where(in_chunk_causal, _dot_nt(qm, k_t), 0.0).astype(BF16)
        intra.append(_dot(att, v[:, hh * B_DV:(hh + 1) * B_DV]))
    yield

    srow = lax.broadcasted_iota(jnp.int32, (2 * B_DK, 2 * B_DV), 0)
    scol = lax.broadcasted_iota(jnp.int32, (2 * B_DK, 2 * B_DV), 1)
    own_head = (srow // B_DK) == (scol // B_DV)
    updates = [jnp.where(own_head, _dot_tn(k_dec[sl], v[sl]), 0.0) for sl in chunks]
    yield
    state = s_ref[...]
    starts = [None] * n_chunks
    for c in (range(n_chunks) if forward else reversed(range(n_chunks))):
        starts[c] = state.astype(BF16)
        decay = jnp.broadcast_to(jnp.exp(b_last[c]), (LANES, LANES)).T
        state = state * jnp.concatenate([decay, decay], axis=1) + updates[c]
    s_ref[...] = state
    yield
    q_tb = q_t.astype(BF16)
    inter = [_dot(q_tb[sl], starts[c]) for c, sl in enumerate(chunks)]
    o_ref[0, rows, :] += jnp.concatenate(intra, axis=1) + jnp.concatenate(inter, axis=0)


GLA_STEPS_PER_ITER = 3


def _gla_kernel(q_ref, k_ref, v_ref, gf_ref, gb_ref, o_ref, sf_ref, sb_ref):
    sf_ref[...] = jnp.zeros_like(sf_ref)
    sb_ref[...] = jnp.zeros_like(sb_ref)
    o_ref[...] = jnp.zeros_like(o_ref)

    @pl.loop(0, N_TILES // GLA_STEPS_PER_ITER)
    def _(it):
        stages = []
        for sub in range(GLA_STEPS_PER_ITER):
            step = it * GLA_STEPS_PER_ITER + sub
            fwd_tile = jnp.where(step == 0, CTX_TILE, step - 1)
            bwd_tile = CTX_TILE - step
            stages.append(_gla_direction(q_ref, k_ref, v_ref, gf_ref, o_ref, sf_ref, fwd_tile, True))
            stages.append(_gla_direction(q_ref, k_ref, v_ref, gb_ref, o_ref, sb_ref, bwd_tile, False))
        while stages:
            stages = [s for s in stages if next(s, StopIteration) is not StopIteration]


def _gla(gq, gk, gv, gf, gb):
    kspec = pl.BlockSpec((1, T, LANES), lambda b, p: (b, 0, p))
    vspec = pl.BlockSpec((1, T, 2 * B_DV), lambda b, p: (b, 0, p))
    out = jax.ShapeDtypeStruct((BATCH, T, B_WIDTH), F32)
    return pl.pallas_call(
        _gla_kernel,
        out_shape=out,
        grid=(BATCH, B_HEADS // 2),
        in_specs=[kspec, kspec, vspec, kspec, kspec],
        out_specs=vspec,
        scratch_shapes=[pltpu.VMEM((2 * B_DK, 2 * B_DV), F32), pltpu.VMEM((2 * B_DK, 2 * B_DV), F32)],
        compiler_params=_params(("parallel", "parallel")),
        name="gla_scan",
    )(gq, gk, gv, gf, gb)


def _even_out_body(ins):
    a_ref, gla_ref, zb_ref, gn_ref, wa_ref, wb_ref, xl_ref, xc_ref, mod_ref = ins
    bsum = gla_ref[0]
    zb = zb_ref[0]
    parts = []
    for hh in range(B_HEADS):
        sl = slice(hh * B_DV, (hh + 1) * B_DV)
        parts.append((_rms(bsum[:, sl], B_DV) * gn_ref[:, sl] * _silu(zb[:, sl])).astype(BF16))
    y = _dot(a_ref[0], wa_ref[0]) + _dot(jnp.concatenate(parts, axis=1), wb_ref[0])
    return _stream_tile(xl_ref, xc_ref) + mod_ref[0, 2:3, :] * y


def _layer_spec(layer, shape, block=0):
    return pl.BlockSpec((1,) + shape, lambda b, t: (layer, block, 0))


def _even_out_part(a, gla, zb, gn, w_out, layer, stream, mod):
    x_lat, x_ctx, ctx_block = stream
    tok = _row_spec
    return _Part(
        arrays=(a, gla, zb, gn, w_out, w_out, x_lat, x_ctx, mod),
        in_specs=(tok(A_WIDTH), tok(B_WIDTH), tok(B_WIDTH), _const_spec((1, B_WIDTH)),
                  _layer_spec(layer, (A_WIDTH, D), 0), _layer_spec(layer, (B_WIDTH, D), 1),
                  *_stream_specs(ctx_block),
                  pl.BlockSpec((1, 3, D), _mod_index)),
        out_shapes=(), out_specs=(), body=_even_out_body, writes_stream=True)


def _odd_in_body(x, ins, outs):
    mod_ref, ng_ref, win_ref, qg_ref, kg_ref = ins
    q_ref, k_ref, v_ref, z_ref = outs
    h = _modulated_norm(x, mod_ref, ng_ref)
    u = _dot(h.astype(BF16), win_ref[0])
    ss = _group_sumsq(u[:, :2 * C_WIDTH], C_HD, C_HD)
    scale = lax.rsqrt(ss * (1.0 / C_HD) + EPS)
    for j in range(C_WIDTH // LANES):
        sl = slice(j * LANES, (j + 1) * LANES)
        q_ref[0, :, sl] = (u[:, sl] * scale[:, sl] * qg_ref[...] * (C_HD ** -0.5 * LOG2E)).astype(BF16)
        ks = slice(C_WIDTH + j * LANES, C_WIDTH + (j + 1) * LANES)
        k_ref[0, :, sl] = (u[:, ks] * scale[:, ks] * kg_ref[...]).astype(BF16)
    v_ref[0] = u[:, 2 * C_WIDTH:3 * C_WIDTH].astype(BF16)
    z_ref[0] = u[:, 3 * C_WIDTH:]


def _odd_in_part(mod, ng, win, layer, qg, kg):
    tok = _row_spec
    sd = jax.ShapeDtypeStruct
    return _Part(
        arrays=(mod, ng, win, qg, kg),
        in_specs=(pl.BlockSpec((1, 3, D), _mod_index), _const_spec((1, D)), _layer_spec(layer, (D, 4 * C_WIDTH)),
                  _const_spec((1, LANES)), _const_spec((1, LANES))),
        out_shapes=(sd((BATCH, T, C_WIDTH), BF16), sd((BATCH, T, C_WIDTH), BF16), sd((BATCH, T, C_WIDTH), BF16),
                    sd((BATCH, T, C_WIDTH), F32)),
        out_specs=(tok(C_WIDTH), tok(C_WIDTH), tok(C_WIDTH), tok(C_WIDTH)),
        body=_odd_in_body)


ROWS_PER_CHAIN = 2
ROWS_PER_STEP = 16
SCORE_LEAD = 1
UNION_ROWS = WIN_ROWS + ROWS_PER_CHAIN
UNION_KEYS = UNION_ROWS * GRID_W
BIAS_SLOTS = 2 * WIN_ROWS


def _nbr_attn_kernel(q_ref, k_ref, v_ref, z_ref, bias_ref, o_ref):
    lane = lax.broadcasted_iota(jnp.int32, (1, LANES), 1)
    first = lane < C_HD
    head_mask = (first, lane >= C_HD)
    ctx = slice(SEQ, T)

    def stack_heads(blocks):
        zero = jnp.zeros(blocks[0].shape, BF16)
        return jnp.concatenate([jnp.where(head_mask[hh], blk, zero) for blk in blocks for hh in range(2)], axis=0)

    def probabilities(s):
        m = jnp.max(s, axis=-1, keepdims=True)
        return jnp.exp2(s - m).astype(BF16)

    def finish(p, v_all, n_blocks, rows, out_rows):
        o = _normalise(_dot(p, _values_with_ones(v_all)))
        o = jnp.concatenate(
            [jnp.where(first, o[(2 * i) * rows:(2 * i + 1) * rows], o[(2 * i + 1) * rows:(2 * i + 2) * rows])
             for i in range(n_blocks)], axis=0)
        o_ref[0, out_rows, :] = (o * _silu(z_ref[0, out_rows, :])).astype(BF16)

    def chain_keys(r0):
        u0 = jnp.clip(r0 - WIN_ROWS // 2, 0, GRID_ROWS - UNION_ROWS)
        return u0, pl.ds(pl.multiple_of(u0 * GRID_W, 2 * GRID_W), UNION_KEYS)

    def query_rows(r0):
        return pl.ds(pl.multiple_of(r0 * GRID_W, ROWS_PER_CHAIN * GRID_W), ROWS_PER_CHAIN * GRID_W)

    def raw_scores(r0, k_ctx):
        q_pair = q_ref[0, query_rows(r0), :]
        q_rows = [q_pair[rr * GRID_W:(rr + 1) * GRID_W] for rr in range(ROWS_PER_CHAIN)]
        k_all = jnp.concatenate([k_ref[0, chain_keys(r0)[1], :], k_ctx], axis=0)
        return _dot_nt(stack_heads(q_rows), k_all)

    def biased_scores(r0, s):
        u0, _ = chain_keys(r0)
        blocks = []
        for rr in range(ROWS_PER_CHAIN):
            r = r0 + rr
            r_start = jnp.clip(r - WIN_ROWS // 2, 0, GRID_ROWS - WIN_ROWS)
            for hh in range(2):
                blk = s[(2 * rr + hh) * GRID_W:(2 * rr + hh + 1) * GRID_W]
                pieces = []
                for mm in range(UNION_ROWS // 2):
                    u = u0 + 2 * mm
                    ok_l = ((u >= r_start) & (u < r_start + WIN_ROWS)).astype(jnp.int32)
                    ok_r = ((u + 1 >= r_start) & (u + 1 < r_start + WIN_ROWS)).astype(jnp.int32)
                    slot = jnp.clip(u - r + WIN_ROWS, 0, BIAS_SLOTS - 1)
                    in_window = jnp.where(first, ok_l, ok_r) > 0
                    scores = blk[:, mm * LANES:(mm + 1) * LANES] + bias_ref[0, hh, slot]
                    pieces.append(jnp.where(in_window, scores, NEG))
                pieces.append(blk[:, UNION_KEYS:])
                blocks.append(jnp.concatenate(pieces, axis=1))
        return jnp.concatenate(blocks, axis=0)

    @pl.loop(0, GRID_ROWS // ROWS_PER_STEP)
    def _(t):
        k_ctx = k_ref[0, ctx, :]
        v_ctx = v_ref[0, ctx, :]
        firsts = [t * ROWS_PER_STEP + chain * ROWS_PER_CHAIN for chain in range(ROWS_PER_STEP // ROWS_PER_CHAIN)]
        raw = [raw_scores(r0, k_ctx) for r0 in firsts[:SCORE_LEAD]]
        for i, r0 in enumerate(firsts):
            p = probabilities(biased_scores(r0, raw[i]))
            if i + SCORE_LEAD < len(firsts):
                raw.append(raw_scores(firsts[i + SCORE_LEAD], k_ctx))
            v_all = jnp.concatenate([v_ref[0, chain_keys(r0)[1], :], v_ctx], axis=0)
            finish(p, v_all, ROWS_PER_CHAIN, GRID_W, query_rows(r0))

    p = probabilities(_dot_nt(stack_heads([q_ref[0, ctx, :]]), k_ref[0, ctx, :]))
    finish(p, v_ref[0, ctx, :], 1, CTX, ctx)


def _nbr_attn(q, k, v, z, bias):
    pair = pl.BlockSpec((1, T, LANES), lambda p, b: (b, 0, p))
    return pl.pallas_call(
        _nbr_attn_kernel,
        out_shape=jax.ShapeDtypeStruct((BATCH, T, C_WIDTH), BF16),
        grid=(C_HEADS // 2, BATCH),
        in_specs=[pair, pair, pair, pair,
                  pl.BlockSpec((1, 2, BIAS_SLOTS, GRID_W, LANES), lambda p, b: (p, 0, 0, 0, 0))],
        out_specs=pair,
        compiler_params=_params(("parallel", "parallel")),
        name="nbr_attn",
    )(q, k, v, z, bias)


def _odd_out_body(ins):
    o_ref, w_ref, x_ref, mod_ref = ins
    return x_ref[0] + mod_ref[0, 2:3, :] * _dot(o_ref[0], w_ref[0])


def _odd_out_part(o, w, layer, xu, mod):
    return _Part(
        arrays=(o, w, xu, mod),
        in_specs=(_row_spec(C_WIDTH), _layer_spec(layer, (C_WIDTH, D)), _row_spec(D),
                  pl.BlockSpec((1, 3, D), _mod_index)),
        out_shapes=(), out_specs=(), body=_odd_out_body, writes_stream=True)


HALF = A_ROPE // 2


def _head_slot(nope, rope):
    n_freq = HALF // 2
    block = lambda axis, half: rope[..., (2 * axis + half) * n_freq:(2 * axis + half + 1) * n_freq]
    half0 = [block(0, 0), block(1, 0)]
    half1 = [block(0, 1), block(1, 1)]
    pad = jnp.zeros(rope.shape[:-1] + (LANES - A_QK - HALF,), rope.dtype)
    return jnp.concatenate([nope] + half0 + half1 + half0 + [pad], axis=-1)


def _rope_tables():
    n = np.arange(SEQ)
    pos = np.stack([n // GRID_W, n % GRID_W], axis=1).astype(np.float32)
    inv = (ROPE_THETA ** (-np.arange(0, HALF, 2, dtype=np.float32) / HALF)).astype(np.float32)
    ang = (pos[:, :, None] * inv).astype(np.float64).reshape(SEQ, HALF)
    cos = np.ones((T, LANES), np.float32)
    sin = np.zeros((T, LANES), np.float32)
    cos[:, A_QK:] = 0.0
    cos[:SEQ, A_NOPE:A_QK] = np.tile(np.cos(ang), (1, 2))
    sin[:SEQ, A_NOPE:A_QK] = np.concatenate([-np.sin(ang), np.sin(ang)], axis=1)
    return jnp.asarray(cos), jnp.asarray(sin)


def _pad_lanes(v, width=LANES):
    return jnp.pad(v, (0, width - v.shape[0])).reshape(1, width)


def _even_weights(j, p, rope):
    w_in = p["ev_w_in"][j]
    q_l, kv_l, k_r, z_a, q_b, k_b, v_b, lr, z_b = jnp.split(
        w_in, [768, 1024, 1056, 1568, 1824, 2080, 2592, 2624], axis=1)
    zeros = jnp.zeros((D, 32), F32)
    misc = jnp.concatenate([lr, zeros, _head_slot(jnp.zeros((D, 0), F32), k_r)], axis=1)
    w_in2 = jnp.concatenate([q_l, kv_l, misc, z_a, q_b, k_b, v_b, z_b], axis=1).astype(BF16)
    w_uq = p["ev_w_uq"][j].reshape(Q_LORA, A_HEADS, A_QK)
    w_uq = _head_slot(w_uq[..., :A_NOPE], w_uq[..., A_NOPE:])
    w_ukv = p["ev_w_ukv"][j].reshape(KV_LORA, A_HEADS, A_NOPE + A_V)
    w_uk = jnp.pad(w_ukv[:, :, :A_NOPE], ((0, 0), (0, 0), (0, LANES - A_NOPE))).reshape(KV_LORA, A_HEADS * LANES)
    w_uv = w_ukv[:, :, A_NOPE:].reshape(KV_LORA, A_WIDTH)
    gate_w = p["ev_gate_w"][j]
    w_gate = jnp.zeros((LANES, 2 * B_KW), F32)
    w_gate = w_gate.at[:GATE_RANK, :B_KW].set(gate_w[0]).at[GATE_RANK:2 * GATE_RANK, B_KW:].set(gate_w[1])
    return {
        "w_in": w_in2,
        "q_norm": p["ev_q_norm"][j].reshape(1, Q_LORA),
        "w_uq": w_uq.reshape(Q_LORA, A_HEADS * LANES).astype(BF16),
        "kv_norm": p["ev_kv_norm"][j].reshape(1, KV_LORA),
        "w_ukv": jnp.concatenate([w_uk, w_uv], axis=1).astype(BF16),
        "q_gain": _head_slot(p["ev_q_gain"][j][:A_NOPE], p["ev_q_gain"][j][A_NOPE:]).reshape(1, LANES),
        "k_gain": _head_slot(p["ev_k_gain"][j][:A_NOPE], p["ev_k_gain"][j][A_NOPE:]).reshape(1, LANES),
        "w_gate": w_gate.astype(BF16),
        "gate_b": p["ev_gate_b"][j].reshape(1, 2 * B_KW),
        "cos": rope[0], "sin": rope[1],
        "gla_norm": p["ev_gla_norm"][j].reshape(1, B_WIDTH),
    }


def _nbr_bias_table(rpb):
    col = np.arange(GRID_W)
    col_start = np.clip(col - WIN_COLS // 2, 0, GRID_W - WIN_COLS)
    col_mask = (col[None, :] >= col_start[:, None]) & (col[None, :] < col_start[:, None] + WIN_COLS)
    col_idx = np.clip(col[None, :] - col[:, None] + WIN_COLS - 1, 0, 2 * WIN_COLS - 2)
    n_rel = 2 * WIN_COLS - 1
    n_rows = 2 * WIN_ROWS - 1
    pick = ((col_idx[None] == np.arange(n_rel)[:, None, None]) & col_mask[None]) * LOG2E
    select = np.zeros((2 * n_rel + 3, GRID_W, 2, GRID_W), np.float32)
    for half in range(2):
        select[half * n_rel:(half + 1) * n_rel, :, half, :] = pick
        select[2 * n_rel, :, half, :] = np.where(col_mask, 0.0, NEG)
        select[2 * n_rel + 1 + half, :, half, :] = NEG
    slots = np.arange(BIAS_SLOTS)
    outside = np.stack([slots - 1 < 0, slots >= n_rows], axis=1).astype(np.float32)
    ext = jnp.pad(rpb, ((0, 0), (1, 1), (0, 0)))
    flags = jnp.broadcast_to(jnp.asarray(np.concatenate([np.ones((BIAS_SLOTS, 1), np.float32), outside], axis=1)),
                             (C_HEADS, BIAS_SLOTS, 3))
    lhs = jnp.concatenate([ext[:, :-1], ext[:, 1:], flags], axis=-1).reshape(C_HEADS * BIAS_SLOTS, 2 * n_rel + 3)
    table = jnp.dot(lhs, jnp.asarray(select.reshape(2 * n_rel + 3, -1)), precision=lax.Precision.HIGHEST)
    return table.reshape(C_HEADS // 2, 2, BIAS_SLOTS, GRID_W, LANES)


def kernel(x, c, ctx, c_ctx, norm_g, ada_w, ada_b, ev_w_in, ev_q_norm, ev_w_uq, ev_kv_norm, ev_w_ukv, ev_q_gain,
           ev_k_gain, ev_gate_w, ev_gate_b, ev_gla_norm, ev_w_out, od_w_in, od_q_gain, od_k_gain, od_rpb, od_w_out):
    p = dict(ev_w_in=ev_w_in, ev_q_norm=ev_q_norm, ev_w_uq=ev_w_uq, ev_kv_norm=ev_kv_norm, ev_w_ukv=ev_w_ukv,
             ev_q_gain=ev_q_gain, ev_k_gain=ev_k_gain, ev_gate_w=ev_gate_w, ev_gate_b=ev_gate_b,
             ev_gla_norm=ev_gla_norm, ev_w_out=ev_w_out)
    c16 = jnp.zeros((MOD_ROWS, D), F32).at[:BATCH].set(c).at[BATCH].set(c_ctx)
    mod_all = _modulation(c16, ada_w, ada_b).reshape(DEPTH, MOD_ROWS, 3, D)
    rope = _rope_tables()
    od_w_in_b, od_w_out_b, ev_w_out_b = od_w_in.astype(BF16), od_w_out.astype(BF16), ev_w_out.astype(BF16)
    xu = None
    finish, finish_name = _source_part(x, ctx), "inputs"
    for i in range(DEPTH):
        j = i // 2
        mod = mod_all[i]
        ng = norm_g[i].reshape(1, D)
        if i % 2 == 0:
            w = _even_weights(j, p, rope)
            start, start_name = _even_in_part(mod, ng, w), "even_in"
        else:
            start = _odd_in_part(mod, ng, od_w_in_b, j, _pad_lanes(jnp.tile(od_q_gain[j], 2)),
                                 _pad_lanes(jnp.tile(od_k_gain[j], 2)))
            start_name = "odd_in"
        outs = _boundary_call(f"{finish_name}_{start_name}", finish, start)
        if finish.writes_stream:
            xu, outs = outs[0], outs[1:]
        if i % 2 == 0:
            q, k, v, za, gq, gk, gv, gf, gb, zb = outs
            a = _mla_attn(q, k, v, za)
            gla = _gla(gq, gk, gv, gf, gb)
            stream = (x, ctx, 0) if xu is None else (xu, xu, CTX_TILE)
            finish = _even_out_part(a, gla, zb, w["gla_norm"], ev_w_out_b, j, stream, mod)
            finish_name = "even_out"
        else:
            q, k, v, z = outs
            o = _nbr_attn(q, k, v, z, _nbr_bias_table(od_rpb[j]))
            finish, finish_name = _odd_out_part(o, od_w_out_b, j, xu, mod), "odd_out"
    return _boundary_call(finish_name, finish, None, n_tiles=CTX_TILE)[0]
```

```python
import functools
from typing import Callable, NamedTuple

import jax
import jax.numpy as jnp
import numpy as np
from jax import lax
from jax.experimental import pallas as pl
from jax.experimental.pallas import tpu as pltpu

D = 1024
BATCH = 8
SEQ = 2048
DEPTH = 4
GRID_W = 64
GRID_ROWS = SEQ // GRID_W
CTX = 256
T = SEQ + CTX
EPS = 1e-6
ROPE_THETA = 10000.0

A_HEADS = 8
A_NOPE = 64
A_ROPE = 32
A_V = 64
A_QK = A_NOPE + A_ROPE
KV_LORA = 256
Q_LORA = 768
A_WIDTH = A_HEADS * A_V

B_HEADS = 4
B_DK = 64
B_DV = 128
B_KW = B_HEADS * B_DK
B_WIDTH = B_HEADS * B_DV
GATE_RANK = 16
GATE_TAU = 16.0
CHUNK = 64

C_HEADS = 16
C_HD = 64
C_WIDTH = C_HEADS * C_HD
WIN_ROWS = 8
WIN_COLS = 16

LANES = 128
TILE = 256
N_TILES = T // TILE
CTX_TILE = N_TILES - 1
MOD_ROWS = 16
EVEN_COLS = 3072
EVEN_EARLY = Q_LORA + KV_LORA
NEG = -1e30
LOG2E = 1.4426950408889634
VMEM_LIMIT = 56 * 1024 * 1024

F32 = jnp.float32
BF16 = jnp.bfloat16


def _dot(a, b):
    return lax.dot_general(a, b, (((1,), (0,)), ((), ())), preferred_element_type=F32)


def _dot_nt(a, b):
    return lax.dot_general(a, b, (((1,), (1,)), ((), ())), preferred_element_type=F32)


def _dot_tn(a, b):
    return lax.dot_general(a, b, (((0,), (0,)), ((), ())), preferred_element_type=F32)


def _silu(x):
    return x / (1.0 + jnp.exp(-x))


def _rms(x, n):
    return x * lax.rsqrt(jnp.sum(x * x, axis=-1, keepdims=True) * (1.0 / n) + EPS)


def _params(sem):
    return pltpu.CompilerParams(dimension_semantics=sem, vmem_limit_bytes=VMEM_LIMIT)


def _const_spec(shape):
    nd = len(shape)
    return pl.BlockSpec(shape, lambda *_: (0,) * nd)


def _mod_index(b, t):
    return (jnp.where(t == CTX_TILE, BATCH, b), 0, 0)


def _mod_kernel(c_ref, w_ref, b_ref, o_ref):
    a = _silu(c_ref[...]).astype(BF16)
    o_ref[0] = _dot(a, w_ref[0].astype(BF16)) + b_ref[0]


def _modulation(c16, ada_w, ada_b):
    return pl.pallas_call(
        _mod_kernel,
        out_shape=jax.ShapeDtypeStruct((DEPTH, MOD_ROWS, 3 * D), F32),
        grid=(DEPTH, 3),
        in_specs=[
            pl.BlockSpec((MOD_ROWS, D), lambda i, j: (0, 0)),
            pl.BlockSpec((1, D, D), lambda i, j: (i, 0, j)),
            pl.BlockSpec((1, 1, D), lambda i, j: (i, 0, j)),
        ],
        out_specs=pl.BlockSpec((1, MOD_ROWS, D), lambda i, j: (i, 0, j)),
        compiler_params=_params(("parallel", "parallel")),
        name="modulation",
    )(c16, ada_w, ada_b.reshape(DEPTH, 1, 3 * D))


def _modulated_norm(x, mod_ref, ng_ref):
    return _rms(x, D) * ng_ref[...] * (1.0 + mod_ref[0, 1:2, :]) + mod_ref[0, 0:1, :]


def _group_sumsq(u, group, counted):
    sq = (u * u).astype(BF16)
    width = 2 * LANES
    row = lax.broadcasted_iota(jnp.int32, (width, width), 0)
    col = lax.broadcasted_iota(jnp.int32, (width, width), 1)
    ones = ((row // group == col // group) & (row % group < counted)).astype(BF16)
    cols = [slice(c, c + width) for c in range(0, u.shape[1], width)]
    return jnp.concatenate([_dot(sq[:, c], ones) for c in cols], axis=1)


def _slot_sumsq(u):
    return _group_sumsq(u, LANES, A_QK)


def _stream_specs(ctx_block):
    return [pl.BlockSpec((1, TILE, D), lambda b, t: (b, jnp.minimum(t, CTX_TILE - 1), 0)),
            pl.BlockSpec((1, TILE, D), lambda b, t: (b, ctx_block, 0))]


def _stream_tile(lat_ref, ctx_ref):
    return jnp.where(pl.program_id(1) == CTX_TILE, ctx_ref[0], lat_ref[0])


class _Part(NamedTuple):
    arrays: tuple
    in_specs: tuple
    out_shapes: tuple
    out_specs: tuple
    body: Callable
    writes_stream: bool = False


def _row_spec(width):
    return pl.BlockSpec((1, TILE, width), lambda b, t: (b, t, 0))


def _boundary_call(name, finish, start, n_tiles=N_TILES):
    parts = [p for p in (finish, start) if p is not None]
    stream_shape = (jax.ShapeDtypeStruct((BATCH, n_tiles * TILE, D), F32),) if finish.writes_stream else ()
    stream_spec = (_row_spec(D),) if finish.writes_stream else ()
    n_finish_in = len(finish.arrays)
    n_in = sum(len(p.arrays) for p in parts)

    def kernel(*refs):
        ins, outs = refs[:n_in], refs[n_in:]
        x = finish.body(ins[:n_finish_in])
        if finish.writes_stream:
            outs[0][0] = x
        if start is not None:
            start.body(x, ins[n_finish_in:], outs[len(stream_shape):])

    out_shapes = stream_shape + (start.out_shapes if start is not None else ())
    out_specs = stream_spec + (start.out_specs if start is not None else ())
    return pl.pallas_call(
        kernel,
        out_shape=out_shapes,
        grid=(BATCH, n_tiles),
        in_specs=[s for p in parts for s in p.in_specs],
        out_specs=out_specs,
        compiler_params=_params(("parallel", "parallel")),
        name=name,
    )(*[a for p in parts for a in p.arrays])


def _source_part(x, ctx):
    return _Part((x, ctx), tuple(_stream_specs(0)), (), (), lambda ins: _stream_tile(*ins))


def _even_in_body(x, ins, outs):
    (mod_ref, ng_ref, win_ref, wmisc_ref, qn_ref, wuq_ref, kvn_ref, wukv_ref, qg_ref, kg_ref, wg_ref, gbias_ref,
     cos_ref, sin_ref) = ins
    q_ref, k_ref, v_ref, za_ref, gq_ref, gk_ref, gv_ref, gf_ref, gb_ref, zb_ref = outs
    h = _modulated_norm(x, mod_ref, ng_ref).astype(BF16)
    lat = _dot(h, win_ref[:, :EVEN_EARLY])
    q_lat = lat[:, 0:Q_LORA]
    kv_lat = lat[:, Q_LORA:]
    misc = _dot(h, wmisc_ref[...])

    cos = cos_ref[...]
    sin = sin_ref[...]

    def rope(u):
        return u * cos + pltpu.roll(u, LANES - HALF, 1) * sin

    qf = _dot((_rms(q_lat, Q_LORA) * qn_ref[...]).astype(BF16), wuq_ref[...])
    kvf = _dot((_rms(kv_lat, KV_LORA) * kvn_ref[...]).astype(BF16), wukv_ref[...])

    logits = _dot(misc.astype(BF16), wg_ref[...]) + gbias_ref[...]
    g = (jnp.minimum(logits, 0.0) - jnp.log1p(jnp.exp(-jnp.abs(logits)))) * (1.0 / GATE_TAU)
    gf_ref[0] = g[:, :B_KW]
    gb_ref[0] = g[:, B_KW:]

    lane = lax.broadcasted_iota(jnp.int32, (1, LANES), 1)
    rotary = (lane >= A_NOPE) & (lane < A_QK)
    kr_roped = rope(jnp.where(lane >= A_NOPE, misc, 0.0) * kg_ref[...])
    k_rope = jnp.where(rotary, misc, 0.0)
    kr_ss = jnp.sum(k_rope * k_rope, axis=-1, keepdims=True)
    q_ss = _slot_sumsq(qf)
    k_ss = _slot_sumsq(kvf[:, :A_HEADS * LANES])

    late = _dot(h, win_ref[:, EVEN_EARLY:])
    za_ref[0] = late[:, 0:512]
    gq_ref[0] = late[:, 512:768] * (B_DK ** -0.5)
    gk_ref[0] = late[:, 768:1024]
    gv_ref[0] = late[:, 1024:1536].astype(BF16)
    zb_ref[0] = late[:, 1536:2048]

    for hh in range(A_HEADS):
        sl = slice(hh * LANES, (hh + 1) * LANES)
        q_scale = lax.rsqrt(q_ss[:, sl] * (1.0 / A_QK) + EPS) * (A_QK ** -0.5 * LOG2E)
        q_ref[0, hh] = (rope(qf[:, sl] * qg_ref[...]) * q_scale).astype(BF16)
        k_scale = lax.rsqrt((k_ss[:, sl] + kr_ss) * (1.0 / A_QK) + EPS)
        k_ref[0, hh] = ((kvf[:, sl] * kg_ref[...] + kr_roped) * k_scale).astype(BF16)
    v_ref[0] = kvf[:, A_HEADS * LANES:].astype(BF16)


def _even_in_part(mod, ng, w):
    tok = _row_spec
    head = pl.BlockSpec((1, A_HEADS, TILE, LANES), lambda b, t: (b, 0, t, 0))
    rope_spec = pl.BlockSpec((TILE, LANES), lambda b, t: (t, 0))
    sd = jax.ShapeDtypeStruct
    return _Part(
        arrays=(mod, ng, w["w_in"], w["w_misc"], w["q_norm"], w["w_uq"], w["kv_norm"], w["w_ukv"], w["q_gain"],
                w["k_gain"], w["w_gate"], w["gate_b"], w["cos"], w["sin"]),
        in_specs=(
            pl.BlockSpec((1, 3, D), _mod_index),
            _const_spec((1, D)),
            _const_spec((D, EVEN_COLS)),
            _const_spec((D, LANES)),
            _const_spec((1, Q_LORA)),
            _const_spec((Q_LORA, A_HEADS * LANES)),
            _const_spec((1, KV_LORA)),
            _const_spec((KV_LORA, A_HEADS * LANES + A_WIDTH)),
            _const_spec((1, LANES)),
            _const_spec((1, LANES)),
            _const_spec((LANES, 2 * B_KW)),
            _const_spec((1, 2 * B_KW)),
            rope_spec, rope_spec,
        ),
        out_shapes=(
            sd((BATCH, A_HEADS, T, LANES), BF16),
            sd((BATCH, A_HEADS, T, LANES), BF16),
            sd((BATCH, T, A_WIDTH), BF16),
            sd((BATCH, T, A_WIDTH), F32),
            sd((BATCH, T, B_KW), F32),
            sd((BATCH, T, B_KW), F32),
            sd((BATCH, T, B_WIDTH), BF16),
            sd((BATCH, T, B_KW), F32),
            sd((BATCH, T, B_KW), F32),
            sd((BATCH, T, B_WIDTH), F32),
        ),
        out_specs=(head, head, tok(A_WIDTH), tok(A_WIDTH), tok(B_KW), tok(B_KW), tok(B_WIDTH),
                   tok(B_KW), tok(B_KW), tok(B_WIDTH)),
        body=_even_in_body,
    )


def _values_with_ones(v):
    return jnp.concatenate([v, jnp.ones(v.shape, BF16)], axis=1)


def _normalise(acc):
    return acc[:, :LANES] / acc[:, LANES:]


MLA_TILES_PER_STEP = 4


def _mla_attn_kernel(q_ref, k_ref, v_ref, z_ref, o_ref):
    first_head = lax.broadcasted_iota(jnp.int32, (1, LANES), 1) < A_V

    def attend(row_tiles, key_groups):
        rowmax = lambda s: functools.reduce(jnp.maximum, [jnp.max(x, axis=-1, keepdims=True) for x in s])
        chains = [(rows, hh) for rows in row_tiles for hh in range(2)]
        n, groups = len(chains), range(len(key_groups))
        scores, top, probs, acc = {}, {}, {}, {}
        for step in range(n + 2):
            if step >= 1 and step - 1 < n:
                top[step - 1] = rowmax(scores[step - 1])
            for i in groups:
                g = key_groups[i]
                if step < n:
                    rows, hh = chains[step]
                    scores.setdefault(step, []).append(_dot_nt(q_ref[0, hh, rows, :], k_ref[0, hh, g, :]))
                if 0 <= step - 1 < n:
                    probs.setdefault(step - 1, []).append(
                        jnp.exp2(scores[step - 1][i] - top[step - 1]).astype(BF16))
                if 0 <= step - 2 < n:
                    acc.setdefault(step - 2, []).append(_dot(probs[step - 2][i], _values_with_ones(v_ref[0, g, :])))
            done = step - 2
            if done >= 0 and done % 2 == 1:
                rows = chains[done][0]
                o0 = _normalise(functools.reduce(jnp.add, acc[done - 1]))
                o1 = _normalise(functools.reduce(jnp.add, acc[done]))
                o_ref[0, rows, :] = (jnp.where(first_head, o0, o1) * _silu(z_ref[0, rows, :])).astype(BF16)

    @pl.loop(0, CTX_TILE // MLA_TILES_PER_STEP)
    def _(t):
        base = t * (MLA_TILES_PER_STEP * TILE)
        tiles = [pl.ds(pl.multiple_of(base + i * TILE, TILE), TILE) for i in range(MLA_TILES_PER_STEP)]
        attend(tiles, [slice(0, 768), slice(768, 1536), slice(1536, T)])

    attend([slice(SEQ, T)], [slice(SEQ, T)])


def _mla_attn(q, k, v, za):
    pair = pl.BlockSpec((1, T, LANES), lambda b, p: (b, 0, p))
    heads = pl.BlockSpec((1, 2, T, LANES), lambda b, p: (b, p, 0, 0))
    return pl.pallas_call(
        _mla_attn_kernel,
        out_shape=jax.ShapeDtypeStruct((BATCH, T, A_WIDTH), BF16),
        grid=(BATCH, A_HEADS // 2),
        in_specs=[heads, heads, pair, pair],
        out_specs=pair,
        compiler_params=_params(("parallel", "parallel")),
        name="mla_attn",
    )(q, k, v, za)


def _gla_direction(q_ref, k_ref, v_ref, g_ref, o_ref, s_ref, tile, forward):
    rows = pl.ds(pl.multiple_of(tile * TILE, TILE), TILE)
    q = q_ref[0, rows, :]
    k = k_ref[0, rows, :]
    g = g_ref[0, rows, :]
    v = v_ref[0, rows, :]
    row = lax.broadcasted_iota(jnp.int32, (TILE, TILE), 0)
    col = lax.broadcasted_iota(jnp.int32, (TILE, TILE), 1)
    in_chunk_causal = ((row // CHUNK) == (col // CHUNK)) & ((col <= row) if forward else (col >= row))
    tri = in_chunk_causal.astype(BF16)
    g_hi = g.astype(BF16)
    g_lo = (g - g_hi.astype(F32)).astype(BF16)
    b = _dot(tri, g_hi) + _dot(tri, g_lo)

    n_chunks = TILE // CHUNK
    chunks = [slice(c * CHUNK, (c + 1) * CHUNK) for c in range(n_chunks)]
    last = CHUNK - 1 if forward else 0
    b_last = [b[c * CHUNK + last:c * CHUNK + last + 1] for c in range(n_chunks)]
    b_last_rows = jnp.concatenate([jnp.broadcast_to(bl, (CHUNK, LANES)) for bl in b_last], axis=0)
    q_t = q * jnp.exp(b)
    k_t = (k * jnp.exp(-b)).astype(BF16)
    k_dec = (k * jnp.exp(b_last_rows - b)).astype(BF16)
    yield

    lane = lax.broadcasted_iota(jnp.int32, (1, LANES), 1)
    intra = []
    for hh, head_lanes in enumerate((lane < B_DK, lane >= B_DK)):
        qm = jnp.where(head_lanes, q_t, 0.0).astype(BF16)
        att = jnp.where(in_chunk_causal, _dot_nt(qm, k_t), 0.0).astype(BF16)
        intra.append(_dot(att, v[:, hh * B_DV:(hh + 1) * B_DV]))
    yield

    srow = lax.broadcasted_iota(jnp.int32, (2 * B_DK, 2 * B_DV), 0)
    scol = lax.broadcasted_iota(jnp.int32, (2 * B_DK, 2 * B_DV), 1)
    own_head = (srow // B_DK) == (scol // B_DV)
    updates = [jnp.where(own_head, _dot_tn(k_dec[sl], v[sl]), 0.0) for sl in chunks]
    yield
    state = s_ref[...]
    starts = [None] * n_chunks
    for c in (range(n_chunks) if forward else reversed(range(n_chunks))):
        starts[c] = state.astype(BF16)
        decay = jnp.broadcast_to(jnp.exp(b_last[c]), (LANES, LANES)).T
        state = state * jnp.concatenate([decay, decay], axis=1) + updates[c]
    s_ref[...] = state
    yield
    q_tb = q_t.astype(BF16)
    inter = [_dot(q_tb[sl], starts[c]) for c, sl in enumerate(chunks)]
    o_ref[0, rows, :] += jnp.concatenate(intra, axis=1) + jnp.concatenate(inter, axis=0)


GLA_STEPS_PER_ITER = 3


def _gla_kernel(q_ref, k_ref, v_ref, gf_ref, gb_ref, o_ref, sf_ref, sb_ref):
    sf_ref[...] = jnp.zeros_like(sf_ref)
    sb_ref[...] = jnp.zeros_like(sb_ref)
    o_ref[...] = jnp.zeros_like(o_ref)

    @pl.loop(0, N_TILES // GLA_STEPS_PER_ITER)
    def _(it):
        stages = []
        for sub in range(GLA_STEPS_PER_ITER):
            step = it * GLA_STEPS_PER_ITER + sub
            fwd_tile = jnp.where(step == 0, CTX_TILE, step - 1)
            bwd_tile = CTX_TILE - step
            stages.append(_gla_direction(q_ref, k_ref, v_ref, gf_ref, o_ref, sf_ref, fwd_tile, True))
            stages.append(_gla_direction(q_ref, k_ref, v_ref, gb_ref, o_ref, sb_ref, bwd_tile, False))
        while stages:
            stages = [s for s in stages if next(s, StopIteration) is not StopIteration]


def _gla(gq, gk, gv, gf, gb):
    kspec = pl.BlockSpec((1, T, LANES), lambda b, p: (b, 0, p))
    vspec = pl.BlockSpec((1, T, 2 * B_DV), lambda b, p: (b, 0, p))
    out = jax.ShapeDtypeStruct((BATCH, T, B_WIDTH), F32)
    return pl.pallas_call(
        _gla_kernel,
        out_shape=out,
        grid=(BATCH, B_HEADS // 2),
        in_specs=[kspec, kspec, vspec, kspec, kspec],
        out_specs=vspec,
        scratch_shapes=[pltpu.VMEM((2 * B_DK, 2 * B_DV), F32), pltpu.VMEM((2 * B_DK, 2 * B_DV), F32)],
        compiler_params=_params(("parallel", "parallel")),
        name="gla_scan",
    )(gq, gk, gv, gf, gb)


def _even_out_body(ins):
    a_ref, gla_ref, zb_ref, gn_ref, wa_ref, wb_ref, xl_ref, xc_ref, mod_ref = ins
    bsum = gla_ref[0]
    zb = zb_ref[0]
    parts = []
    for hh in range(B_HEADS):
        sl = slice(hh * B_DV, (hh + 1) * B_DV)
        parts.append((_rms(bsum[:, sl], B_DV) * gn_ref[:, sl] * _silu(zb[:, sl])).astype(BF16))
    y = _dot(a_ref[0], wa_ref[0]) + _dot(jnp.concatenate(parts, axis=1), wb_ref[0])
    return _stream_tile(xl_ref, xc_ref) + mod_ref[0, 2:3, :] * y


def _layer_spec(layer, shape, block=0):
    return pl.BlockSpec((1,) + shape, lambda b, t: (layer, block, 0))


def _even_out_part(a, gla, zb, gn, w_out, layer, stream, mod):
    x_lat, x_ctx, ctx_block = stream
    tok = _row_spec
    return _Part(
        arrays=(a, gla, zb, gn, w_out, w_out, x_lat, x_ctx, mod),
        in_specs=(tok(A_WIDTH), tok(B_WIDTH), tok(B_WIDTH), _const_spec((1, B_WIDTH)),
                  _layer_spec(layer, (A_WIDTH, D), 0), _layer_spec(layer, (B_WIDTH, D), 1),
                  *_stream_specs(ctx_block),
                  pl.BlockSpec((1, 3, D), _mod_index)),
        out_shapes=(), out_specs=(), body=_even_out_body, writes_stream=True)


def _odd_in_body(x, ins, outs):
    mod_ref, ng_ref, win_ref, qg_ref, kg_ref = ins
    q_ref, k_ref, v_ref, z_ref = outs
    h = _modulated_norm(x, mod_ref, ng_ref)
    u = _dot(h.astype(BF16), win_ref[0])
    ss = _group_sumsq(u[:, :2 * C_WIDTH], C_HD, C_HD)
    scale = lax.rsqrt(ss * (1.0 / C_HD) + EPS)
    for j in range(C_WIDTH // LANES):
        sl = slice(j * LANES, (j + 1) * LANES)
        q_ref[0, :, sl] = (u[:, sl] * scale[:, sl] * qg_ref[...] * (C_HD ** -0.5 * LOG2E)).astype(BF16)
        ks = slice(C_WIDTH + j * LANES, C_WIDTH + (j + 1) * LANES)
        k_ref[0, :, sl] = (u[:, ks] * scale[:, ks] * kg_ref[...]).astype(BF16)
    v_ref[0] = u[:, 2 * C_WIDTH:3 * C_WIDTH].astype(BF16)
    z_ref[0] = u[:, 3 * C_WIDTH:]


def _odd_in_part(mod, ng, win, layer, qg, kg):
    tok = _row_spec
    sd = jax.ShapeDtypeStruct
    return _Part(
        arrays=(mod, ng, win, qg, kg),
        in_specs=(pl.BlockSpec((1, 3, D), _mod_index), _const_spec((1, D)), _layer_spec(layer, (D, 4 * C_WIDTH)),
                  _const_spec((1, LANES)), _const_spec((1, LANES))),
        out_shapes=(sd((BATCH, T, C_WIDTH), BF16), sd((BATCH, T, C_WIDTH), BF16), sd((BATCH, T, C_WIDTH), BF16),
                    sd((BATCH, T, C_WIDTH), F32)),
        out_specs=(tok(C_WIDTH), tok(C_WIDTH), tok(C_WIDTH), tok(C_WIDTH)),
        body=_odd_in_body)


ROWS_PER_CHAIN = 2
ROWS_PER_STEP = 16
SCORE_LEAD = 1
UNION_ROWS = WIN_ROWS + ROWS_PER_CHAIN
UNION_KEYS = UNION_ROWS * GRID_W
BIAS_SLOTS = 2 * WIN_ROWS


def _nbr_attn_kernel(q_ref, k_ref, v_ref, z_ref, bias_ref, o_ref):
    lane = lax.broadcasted_iota(jnp.int32, (1, LANES), 1)
    first = lane < C_HD
    head_mask = (first, lane >= C_HD)
    ctx = slice(SEQ, T)

    def stack_heads(blocks):
        zero = jnp.zeros(blocks[0].shape, BF16)
        return jnp.concatenate([jnp.where(head_mask[hh], blk, zero) for blk in blocks for hh in range(2)], axis=0)

    def probabilities(s):
        m = jnp.max(s, axis=-1, keepdims=True)
        return jnp.exp2(s - m).astype(BF16)

    def finish(p, v_all, n_blocks, rows, out_rows):
        o = _normalise(_dot(p, _values_with_ones(v_all)))
        o = jnp.concatenate(
            [jnp.where(first, o[(2 * i) * rows:(2 * i + 1) * rows], o[(2 * i + 1) * rows:(2 * i + 2) * rows])
             for i in range(n_blocks)], axis=0)
        o_ref[0, out_rows, :] = (o * _silu(z_ref[0, out_rows, :])).astype(BF16)

    def chain_keys(r0):
        u0 = jnp.clip(r0 - WIN_ROWS // 2, 0, GRID_ROWS - UNION_ROWS)
        return u0, pl.ds(pl.multiple_of(u0 * GRID_W, 2 * GRID_W), UNION_KEYS)

    def query_rows(r0):
        return pl.ds(pl.multiple_of(r0 * GRID_W, ROWS_PER_CHAIN * GRID_W), ROWS_PER_CHAIN * GRID_W)

    def raw_scores(r0, k_ctx):
        q_pair = q_ref[0, query_rows(r0), :]
        q_rows = [q_pair[rr * GRID_W:(rr + 1) * GRID_W] for rr in range(ROWS_PER_CHAIN)]
        k_all = jnp.concatenate([k_ref[0, chain_keys(r0)[1], :], k_ctx], axis=0)
        return _dot_nt(stack_heads(q_rows), k_all)

    def biased_scores(r0, s):
        u0, _ = chain_keys(r0)
        blocks = []
        for rr in range(ROWS_PER_CHAIN):
            r = r0 + rr
            r_start = jnp.clip(r - WIN_ROWS // 2, 0, GRID_ROWS - WIN_ROWS)
            for hh in range(2):
                blk = s[(2 * rr + hh) * GRID_W:(2 * rr + hh + 1) * GRID_W]
                pieces = []
                for mm in range(UNION_ROWS // 2):
                    u = u0 + 2 * mm
                    ok_l = ((u >= r_start) & (u < r_start + WIN_ROWS)).astype(jnp.int32)
                    ok_r = ((u + 1 >= r_start) & (u + 1 < r_start + WIN_ROWS)).astype(jnp.int32)
                    slot = jnp.clip(u - r + WIN_ROWS, 0, BIAS_SLOTS - 1)
                    in_window = jnp.where(first, ok_l, ok_r) > 0
                    scores = blk[:, mm * LANES:(mm + 1) * LANES] + bias_ref[0, hh, slot]
                    pieces.append(jnp.where(in_window, scores, NEG))
                pieces.append(blk[:, UNION_KEYS:])
                blocks.append(jnp.concatenate(pieces, axis=1))
        return jnp.concatenate(blocks, axis=0)

    @pl.loop(0, GRID_ROWS // ROWS_PER_STEP)
    def _(t):
        k_ctx = k_ref[0, ctx, :]
        v_ctx = v_ref[0, ctx, :]
        firsts = [t * ROWS_PER_STEP + chain * ROWS_PER_CHAIN for chain in range(ROWS_PER_STEP // ROWS_PER_CHAIN)]
        raw = [raw_scores(r0, k_ctx) for r0 in firsts[:SCORE_LEAD]]
        for i, r0 in enumerate(firsts):
            p = probabilities(biased_scores(r0, raw[i]))
            if i + SCORE_LEAD < len(firsts):
                raw.append(raw_scores(firsts[i + SCORE_LEAD], k_ctx))
            v_all = jnp.concatenate([v_ref[0, chain_keys(r0)[1], :], v_ctx], axis=0)
            finish(p, v_all, ROWS_PER_CHAIN, GRID_W, query_rows(r0))

    p = probabilities(_dot_nt(stack_heads([q_ref[0, ctx, :]]), k_ref[0, ctx, :]))
    finish(p, v_ref[0, ctx, :], 1, CTX, ctx)


def _nbr_attn(q, k, v, z, bias):
    pair = pl.BlockSpec((1, T, LANES), lambda p, b: (b, 0, p))
    return pl.pallas_call(
        _nbr_attn_kernel,
        out_shape=jax.ShapeDtypeStruct((BATCH, T, C_WIDTH), BF16),
        grid=(C_HEADS // 2, BATCH),
        in_specs=[pair, pair, pair, pair,
                  pl.BlockSpec((1, 2, BIAS_SLOTS, GRID_W, LANES), lambda p, b: (p, 0, 0, 0, 0))],
        out_specs=pair,
        compiler_params=_params(("parallel", "parallel")),
        name="nbr_attn",
    )(q, k, v, z, bias)


def _odd_out_body(ins):
    o_ref, w_ref, x_ref, mod_ref = ins
    return x_ref[0] + mod_ref[0, 2:3, :] * _dot(o_ref[0], w_ref[0])


def _odd_out_part(o, w, layer, xu, mod):
    return _Part(
        arrays=(o, w, xu, mod),
        in_specs=(_row_spec(C_WIDTH), _layer_spec(layer, (C_WIDTH, D)), _row_spec(D),
                  pl.BlockSpec((1, 3, D), _mod_index)),
        out_shapes=(), out_specs=(), body=_odd_out_body, writes_stream=True)


HALF = A_ROPE // 2


def _head_slot(nope, rope):
    n_freq = HALF // 2
    block = lambda axis, half: rope[..., (2 * axis + half) * n_freq:(2 * axis + half + 1) * n_freq]
    half0 = [block(0, 0), block(1, 0)]
    half1 = [block(0, 1), block(1, 1)]
    pad = jnp.zeros(rope.shape[:-1] + (LANES - A_QK - HALF,), rope.dtype)
    return jnp.concatenate([nope] + half0 + half1 + half0 + [pad], axis=-1)


def _rope_tables():
    n = np.arange(SEQ)
    pos = np.stack([n // GRID_W, n % GRID_W], axis=1).astype(np.float32)
    inv = (ROPE_THETA ** (-np.arange(0, HALF, 2, dtype=np.float32) / HALF)).astype(np.float32)
    ang = (pos[:, :, None] * inv).astype(np.float64).reshape(SEQ, HALF)
    cos = np.ones((T, LANES), np.float32)
    sin = np.zeros((T, LANES), np.float32)
    cos[:, A_QK:] = 0.0
    cos[:SEQ, A_NOPE:A_QK] = np.tile(np.cos(ang), (1, 2))
    sin[:SEQ, A_NOPE:A_QK] = np.concatenate([-np.sin(ang), np.sin(ang)], axis=1)
    return jnp.asarray(cos), jnp.asarray(sin)


def _pad_lanes(v, width=LANES):
    return jnp.pad(v, (0, width - v.shape[0])).reshape(1, width)


EVEN_IN_RAW = 3136
EVEN_RUNS = ((0, 1024), (1056, 2592), (2624, 3136))
PREP_ROWS = 256


def _even_w_in_kernel(w_ref, o_ref):
    w = w_ref[0]
    at = 0
    for lo, hi in EVEN_RUNS:
        o_ref[:, at:at + hi - lo] = w[:, lo:hi].astype(BF16)
        at += hi - lo


def _even_w_in(ev_w_in, layer):
    return pl.pallas_call(
        _even_w_in_kernel,
        out_shape=jax.ShapeDtypeStruct((D, EVEN_COLS), BF16),
        grid=(D // PREP_ROWS,),
        in_specs=[pl.BlockSpec((1, PREP_ROWS, EVEN_IN_RAW), lambda r: (layer, r, 0))],
        out_specs=pl.BlockSpec((PREP_ROWS, EVEN_COLS), lambda r: (r, 0)),
        compiler_params=_params(("parallel",)),
        name="even_w_in",
    )(ev_w_in)


def _even_weights(j, p, rope):
    k_r = p["ev_w_in"][j, :, 1024:1056]
    lr = p["ev_w_in"][j, :, 2592:2624]
    zeros = jnp.zeros((D, 32), F32)
    misc = jnp.concatenate([lr, zeros, _head_slot(jnp.zeros((D, 0), F32), k_r)], axis=1)
    w_uq = p["ev_w_uq"][j].reshape(Q_LORA, A_HEADS, A_QK)
    w_uq = _head_slot(w_uq[..., :A_NOPE], w_uq[..., A_NOPE:])
    w_ukv = p["ev_w_ukv"][j].reshape(KV_LORA, A_HEADS, A_NOPE + A_V)
    w_uk = jnp.pad(w_ukv[:, :, :A_NOPE], ((0, 0), (0, 0), (0, LANES - A_NOPE))).reshape(KV_LORA, A_HEADS * LANES)
    w_uv = w_ukv[:, :, A_NOPE:].reshape(KV_LORA, A_WIDTH)
    gate_w = p["ev_gate_w"][j]
    w_gate = jnp.zeros((LANES, 2 * B_KW), F32)
    w_gate = w_gate.at[:GATE_RANK, :B_KW].set(gate_w[0]).at[GATE_RANK:2 * GATE_RANK, B_KW:].set(gate_w[1])
    return {
        "w_in": _even_w_in(p["ev_w_in"], j),
        "w_misc": misc.astype(BF16),
        "q_norm": p["ev_q_norm"][j].reshape(1, Q_LORA),
        "w_uq": w_uq.reshape(Q_LORA, A_HEADS * LANES).astype(BF16),
        "kv_norm": p["ev_kv_norm"][j].reshape(1, KV_LORA),
        "w_ukv": jnp.concatenate([w_uk, w_uv], axis=1).astype(BF16),
        "q_gain": _head_slot(p["ev_q_gain"][j][:A_NOPE], p["ev_q_gain"][j][A_NOPE:]).reshape(1, LANES),
        "k_gain": _head_slot(p["ev_k_gain"][j][:A_NOPE], p["ev_k_gain"][j][A_NOPE:]).reshape(1, LANES),
        "w_gate": w_gate.astype(BF16),
        "gate_b": p["ev_gate_b"][j].reshape(1, 2 * B_KW),
        "cos": rope[0], "sin": rope[1],
        "gla_norm": p["ev_gla_norm"][j].reshape(1, B_WIDTH),
    }


def _nbr_bias_table(rpb):
    col = np.arange(GRID_W)
    col_start = np.clip(col - WIN_COLS // 2, 0, GRID_W - WIN_COLS)
    col_mask = (col[None, :] >= col_start[:, None]) & (col[None, :] < col_start[:, None] + WIN_COLS)
    col_idx = np.clip(col[None, :] - col[:, None] + WIN_COLS - 1, 0, 2 * WIN_COLS - 2)
    n_rel = 2 * WIN_COLS - 1
    n_rows = 2 * WIN_ROWS - 1
    pick = ((col_idx[None] == np.arange(n_rel)[:, None, None]) & col_mask[None]) * LOG2E
    select = np.zeros((2 * n_rel + 3, GRID_W, 2, GRID_W), np.float32)
    for half in range(2):
        select[half * n_rel:(half + 1) * n_rel, :, half, :] = pick
        select[2 * n_rel, :, half, :] = np.where(col_mask, 0.0, NEG)
        select[2 * n_rel + 1 + half, :, half, :] = NEG
    slots = np.arange(BIAS_SLOTS)
    outside = np.stack([slots - 1 < 0, slots >= n_rows], axis=1).astype(np.float32)
    ext = jnp.pad(rpb, ((0, 0), (1, 1), (0, 0)))
    flags = jnp.broadcast_to(jnp.asarray(np.concatenate([np.ones((BIAS_SLOTS, 1), np.float32), outside], axis=1)),
                             (C_HEADS, BIAS_SLOTS, 3))
    lhs = jnp.concatenate([ext[:, :-1], ext[:, 1:], flags], axis=-1).reshape(C_HEADS * BIAS_SLOTS, 2 * n_rel + 3)
    table = jnp.dot(lhs, jnp.asarray(select.reshape(2 * n_rel + 3, -1)), precision=lax.Precision.HIGHEST)
    return table.reshape(C_HEADS // 2, 2, BIAS_SLOTS, GRID_W, LANES)


def kernel(x, c, ctx, c_ctx, norm_g, ada_w, ada_b, ev_w_in, ev_q_norm, ev_w_uq, ev_kv_norm, ev_w_ukv, ev_q_gain,
           ev_k_gain, ev_gate_w, ev_gate_b, ev_gla_norm, ev_w_out, od_w_in, od_q_gain, od_k_gain, od_rpb, od_w_out):
    p = dict(ev_w_in=ev_w_in, ev_q_norm=ev_q_norm, ev_w_uq=ev_w_uq, ev_kv_norm=ev_kv_norm, ev_w_ukv=ev_w_ukv,
             ev_q_gain=ev_q_gain, ev_k_gain=ev_k_gain, ev_gate_w=ev_gate_w, ev_gate_b=ev_gate_b,
             ev_gla_norm=ev_gla_norm, ev_w_out=ev_w_out)
    c16 = jnp.zeros((MOD_ROWS, D), F32).at[:BATCH].set(c).at[BATCH].set(c_ctx)
    mod_all = _modulation(c16, ada_w, ada_b).reshape(DEPTH, MOD_ROWS, 3, D)
    rope = _rope_tables()
    od_w_in_b, od_w_out_b, ev_w_out_b = od_w_in.astype(BF16), od_w_out.astype(BF16), ev_w_out.astype(BF16)
    xu = None
    finish, finish_name = _source_part(x, ctx), "inputs"
    for i in range(DEPTH):
        j = i // 2
        mod = mod_all[i]
        ng = norm_g[i].reshape(1, D)
        if i % 2 == 0:
            w = _even_weights(j, p, rope)
            start, start_name = _even_in_part(mod, ng, w), "even_in"
        else:
            start = _odd_in_part(mod, ng, od_w_in_b, j, _pad_lanes(jnp.tile(od_q_gain[j], 2)),
                                 _pad_lanes(jnp.tile(od_k_gain[j], 2)))
            start_name = "odd_in"
        outs = _boundary_call(f"{finish_name}_{start_name}", finish, start)
        if finish.writes_stream:
            xu, outs = outs[0], outs[1:]
        if i % 2 == 0:
            q, k, v, za, gq, gk, gv, gf, gb, zb = outs
            a = _mla_attn(q, k, v, za)
            gla = _gla(gq, gk, gv, gf, gb)
            stream = (x, ctx, 0) if xu is None else (xu, xu, CTX_TILE)
            finish = _even_out_part(a, gla, zb, w["gla_norm"], ev_w_out_b, j, stream, mod)
            finish_name = "even_out"
        else:
            q, k, v, z = outs
            o = _nbr_attn(q, k, v, z, _nbr_bias_table(od_rpb[j]))
            finish, finish_name = _odd_out_part(o, od_w_out_b, j, xu, mod), "odd_out"
    return _boundary_call(finish_name, finish, None, n_tiles=CTX_TILE)[0]
```

```python
import functools
from typing import Callable, NamedTuple

import jax
import jax.numpy as jnp
import numpy as np
from jax import lax
from jax.experimental import pallas as pl
from jax.experimental.pallas import tpu as pltpu

D = 1024
BATCH = 8
SEQ = 2048
DEPTH = 4
GRID_W = 64
GRID_ROWS = SEQ // GRID_W
CTX = 256
T = SEQ + CTX
EPS = 1e-6
ROPE_THETA = 10000.0

A_HEADS = 8
A_NOPE = 64
A_ROPE = 32
A_V = 64
A_QK = A_NOPE + A_ROPE
KV_LORA = 256
Q_LORA = 768
A_WIDTH = A_HEADS * A_V

B_HEADS = 4
B_DK = 64
B_DV = 128
B_KW = B_HEADS * B_DK
B_WIDTH = B_HEADS * B_DV
GATE_RANK = 16
GATE_TAU = 16.0
CHUNK = 64

C_HEADS = 16
C_HD = 64
C_WIDTH = C_HEADS * C_HD
WIN_ROWS = 8
WIN_COLS = 16

LANES = 128
TILE = 256
N_TILES = T // TILE
CTX_TILE = N_TILES - 1
MOD_ROWS = 16
EVEN_COLS = 3072
EVEN_EARLY = Q_LORA + KV_LORA
NEG = -1e30
LOG2E = 1.4426950408889634
VMEM_LIMIT = 56 * 1024 * 1024

F32 = jnp.float32
BF16 = jnp.bfloat16


def _dot(a, b):
    return lax.dot_general(a, b, (((1,), (0,)), ((), ())), preferred_element_type=F32)


def _dot_nt(a, b):
    return lax.dot_general(a, b, (((1,), (1,)), ((), ())), preferred_element_type=F32)


def _dot_tn(a, b):
    return lax.dot_general(a, b, (((0,), (0,)), ((), ())), preferred_element_type=F32)


def _silu(x):
    return x / (1.0 + jnp.exp(-x))


def _rms(x, n):
    return x * lax.rsqrt(jnp.sum(x * x, axis=-1, keepdims=True) * (1.0 / n) + EPS)


def _params(sem):
    return pltpu.CompilerParams(dimension_semantics=sem, vmem_limit_bytes=VMEM_LIMIT)


def _const_spec(shape):
    nd = len(shape)
    return pl.BlockSpec(shape, lambda *_: (0,) * nd)


def _mod_index(b, t):
    return (jnp.where(t == CTX_TILE, BATCH, b), 0, 0)


def _mod_kernel(c_ref, w_ref, b_ref, o_ref):
    a = _silu(c_ref[...]).astype(BF16)
    o_ref[0] = _dot(a, w_ref[0].astype(BF16)) + b_ref[0]


def _modulation(c16, ada_w, ada_b):
    return pl.pallas_call(
        _mod_kernel,
        out_shape=jax.ShapeDtypeStruct((DEPTH, MOD_ROWS, 3 * D), F32),
        grid=(DEPTH, 3),
        in_specs=[
            pl.BlockSpec((MOD_ROWS, D), lambda i, j: (0, 0)),
            pl.BlockSpec((1, D, D), lambda i, j: (i, 0, j)),
            pl.BlockSpec((1, 1, D), lambda i, j: (i, 0, j)),
        ],
        out_specs=pl.BlockSpec((1, MOD_ROWS, D), lambda i, j: (i, 0, j)),
        compiler_params=_params(("parallel", "parallel")),
        name="modulation",
    )(c16, ada_w, ada_b.reshape(DEPTH, 1, 3 * D))


def _modulated_norm(x, mod_ref, ng_ref):
    return _rms(x, D) * ng_ref[...] * (1.0 + mod_ref[0, 1:2, :]) + mod_ref[0, 0:1, :]


def _group_sumsq(u, group, counted):
    sq = (u * u).astype(BF16)
    width = 2 * LANES
    row = lax.broadcasted_iota(jnp.int32, (width, width), 0)
    col = lax.broadcasted_iota(jnp.int32, (width, width), 1)
    ones = ((row // group == col // group) & (row % group < counted)).astype(BF16)
    cols = [slice(c, c + width) for c in range(0, u.shape[1], width)]
    return jnp.concatenate([_dot(sq[:, c], ones) for c in cols], axis=1)


def _slot_sumsq(u):
    return _group_sumsq(u, LANES, A_QK)


def _stream_specs(ctx_block):
    return [pl.BlockSpec((1, TILE, D), lambda b, t: (b, jnp.minimum(t, CTX_TILE - 1), 0)),
            pl.BlockSpec((1, TILE, D), lambda b, t: (b, ctx_block, 0))]


def _stream_tile(lat_ref, ctx_ref):
    return jnp.where(pl.program_id(1) == CTX_TILE, ctx_ref[0], lat_ref[0])


class _Part(NamedTuple):
    arrays: tuple
    in_specs: tuple
    out_shapes: tuple
    out_specs: tuple
    body: Callable
    writes_stream: bool = False


def _row_spec(width):
    return pl.BlockSpec((1, TILE, width), lambda b, t: (b, t, 0))


def _boundary_call(name, finish, start, n_tiles=N_TILES):
    parts = [p for p in (finish, start) if p is not None]
    stream_shape = (jax.ShapeDtypeStruct((BATCH, n_tiles * TILE, D), F32),) if finish.writes_stream else ()
    stream_spec = (_row_spec(D),) if finish.writes_stream else ()
    n_finish_in = len(finish.arrays)
    n_in = sum(len(p.arrays) for p in parts)

    def kernel(*refs):
        ins, outs = refs[:n_in], refs[n_in:]
        x = finish.body(ins[:n_finish_in])
        if finish.writes_stream:
            outs[0][0] = x
        if start is not None:
            start.body(x, ins[n_finish_in:], outs[len(stream_shape):])

    out_shapes = stream_shape + (start.out_shapes if start is not None else ())
    out_specs = stream_spec + (start.out_specs if start is not None else ())
    return pl.pallas_call(
        kernel,
        out_shape=out_shapes,
        grid=(BATCH, n_tiles),
        in_specs=[s for p in parts for s in p.in_specs],
        out_specs=out_specs,
        compiler_params=_params(("parallel", "parallel")),
        name=name,
    )(*[a for p in parts for a in p.arrays])


def _source_part(x, ctx):
    return _Part((x, ctx), tuple(_stream_specs(0)), (), (), lambda ins: _stream_tile(*ins))


def _even_in_body(x, ins, outs):
    (mod_ref, ng_ref, win_ref, wmisc_ref, qn_ref, wuq_ref, kvn_ref, wukv_ref, qg_ref, kg_ref, wg_ref, gbias_ref,
     cos_ref, sin_ref) = ins
    q_ref, k_ref, v_ref, za_ref, gq_ref, gk_ref, gv_ref, gf_ref, gb_ref, zb_ref = outs
    h = _modulated_norm(x, mod_ref, ng_ref).astype(BF16)
    lat = _dot(h, win_ref[:, :EVEN_EARLY])
    q_lat = lat[:, 0:Q_LORA]
    kv_lat = lat[:, Q_LORA:]
    misc = _dot(h, wmisc_ref[...])

    cos = cos_ref[...]
    sin = sin_ref[...]

    def rope(u):
        return u * cos + pltpu.roll(u, LANES - HALF, 1) * sin

    qf = _dot((_rms(q_lat, Q_LORA) * qn_ref[...]).astype(BF16), wuq_ref[...])
    kvf = _dot((_rms(kv_lat, KV_LORA) * kvn_ref[...]).astype(BF16), wukv_ref[...])

    logits = _dot(misc.astype(BF16), wg_ref[...]) + gbias_ref[...]
    g = (jnp.minimum(logits, 0.0) - jnp.log1p(jnp.exp(-jnp.abs(logits)))) * (1.0 / GATE_TAU)
    gf_ref[0] = g[:, :B_KW]
    gb_ref[0] = g[:, B_KW:]

    lane = lax.broadcasted_iota(jnp.int32, (1, LANES), 1)
    rotary = (lane >= A_NOPE) & (lane < A_QK)
    kr_roped = rope(jnp.where(lane >= A_NOPE, misc, 0.0) * kg_ref[...])
    k_rope = jnp.where(rotary, misc, 0.0)
    kr_ss = jnp.sum(k_rope * k_rope, axis=-1, keepdims=True)
    q_ss = _slot_sumsq(qf)
    k_ss = _slot_sumsq(kvf[:, :A_HEADS * LANES])

    late = _dot(h, win_ref[:, EVEN_EARLY:])
    za_ref[0] = late[:, 0:512]
    gq_ref[0] = late[:, 512:768] * (B_DK ** -0.5)
    gk_ref[0] = late[:, 768:1024]
    gv_ref[0] = late[:, 1024:1536].astype(BF16)
    zb_ref[0] = late[:, 1536:2048]

    for hh in range(A_HEADS):
        sl = slice(hh * LANES, (hh + 1) * LANES)
        q_scale = lax.rsqrt(q_ss[:, sl] * (1.0 / A_QK) + EPS) * (A_QK ** -0.5 * LOG2E)
        q_ref[0, hh] = (rope(qf[:, sl] * qg_ref[...]) * q_scale).astype(BF16)
        k_scale = lax.rsqrt((k_ss[:, sl] + kr_ss) * (1.0 / A_QK) + EPS)
        k_ref[0, hh] = ((kvf[:, sl] * kg_ref[...] + kr_roped) * k_scale).astype(BF16)
    v_ref[0] = kvf[:, A_HEADS * LANES:].astype(BF16)


def _even_in_part(mod, ng, w):
    tok = _row_spec
    head = pl.BlockSpec((1, A_HEADS, TILE, LANES), lambda b, t: (b, 0, t, 0))
    rope_spec = pl.BlockSpec((TILE, LANES), lambda b, t: (t, 0))
    sd = jax.ShapeDtypeStruct
    return _Part(
        arrays=(mod, ng, w["w_in"], w["w_misc"], w["q_norm"], w["w_uq"], w["kv_norm"], w["w_ukv"], w["q_gain"],
                w["k_gain"], w["w_gate"], w["gate_b"], w["cos"], w["sin"]),
        in_specs=(
            pl.BlockSpec((1, 3, D), _mod_index),
            _const_spec((1, D)),
            _const_spec((D, EVEN_COLS)),
            _const_spec((D, LANES)),
            _const_spec((1, Q_LORA)),
            _const_spec((Q_LORA, A_HEADS * LANES)),
            _const_spec((1, KV_LORA)),
            _const_spec((KV_LORA, A_HEADS * LANES + A_WIDTH)),
            _const_spec((1, LANES)),
            _const_spec((1, LANES)),
            _const_spec((LANES, 2 * B_KW)),
            _const_spec((1, 2 * B_KW)),
            rope_spec, rope_spec,
        ),
        out_shapes=(
            sd((BATCH, A_HEADS, T, LANES), BF16),
            sd((BATCH, A_HEADS, T, LANES), BF16),
            sd((BATCH, T, A_WIDTH), BF16),
            sd((BATCH, T, A_WIDTH), F32),
            sd((BATCH, T, B_KW), F32),
            sd((BATCH, T, B_KW), F32),
            sd((BATCH, T, B_WIDTH), BF16),
            sd((BATCH, T, B_KW), F32),
            sd((BATCH, T, B_KW), F32),
            sd((BATCH, T, B_WIDTH), F32),
        ),
        out_specs=(head, head, tok(A_WIDTH), tok(A_WIDTH), tok(B_KW), tok(B_KW), tok(B_WIDTH),
                   tok(B_KW), tok(B_KW), tok(B_WIDTH)),
        body=_even_in_body,
    )


def _values_with_ones(v):
    return jnp.concatenate([v, jnp.ones(v.shape, BF16)], axis=1)


def _normalise(acc):
    return acc[:, :LANES] / acc[:, LANES:]


MLA_TILES_PER_STEP = 8


def _mla_attn_kernel(q_ref, k_ref, v_ref, z_ref, o_ref):
    first_head = lax.broadcasted_iota(jnp.int32, (1, LANES), 1) < A_V

    def attend(row_tiles, key_groups):
        rowmax = lambda s: functools.reduce(jnp.maximum, [jnp.max(x, axis=-1, keepdims=True) for x in s])
        chains = [(rows, hh) for rows in row_tiles for hh in range(2)]
        n, groups = len(chains), range(len(key_groups))
        scores, top, probs, acc = {}, {}, {}, {}
        for step in range(n + 2):
            if step >= 1 and step - 1 < n:
                top[step - 1] = rowmax(scores[step - 1])
            for i in groups:
                g = key_groups[i]
                if step < n:
                    rows, hh = chains[step]
                    scores.setdefault(step, []).append(_dot_nt(q_ref[0, hh, rows, :], k_ref[0, hh, g, :]))
                if 0 <= step - 1 < n:
                    probs.setdefault(step - 1, []).append(
                        jnp.exp2(scores[step - 1][i] - top[step - 1]).astype(BF16))
                if 0 <= step - 2 < n:
                    acc.setdefault(step - 2, []).append(_dot(probs[step - 2][i], _values_with_ones(v_ref[0, g, :])))
            done = step - 2
            if done >= 0 and done % 2 == 1:
                rows = chains[done][0]
                o0 = _normalise(functools.reduce(jnp.add, acc[done - 1]))
                o1 = _normalise(functools.reduce(jnp.add, acc[done]))
                o_ref[0, rows, :] = (jnp.where(first_head, o0, o1) * _silu(z_ref[0, rows, :])).astype(BF16)

    @pl.loop(0, CTX_TILE // MLA_TILES_PER_STEP)
    def _(t):
        base = t * (MLA_TILES_PER_STEP * TILE)
        tiles = [pl.ds(pl.multiple_of(base + i * TILE, TILE), TILE) for i in range(MLA_TILES_PER_STEP)]
        attend(tiles, [slice(0, 768), slice(768, 1536), slice(1536, T)])

    attend([slice(SEQ, T)], [slice(SEQ, T)])


def _mla_attn(q, k, v, za):
    pair = pl.BlockSpec((1, T, LANES), lambda b, p: (b, 0, p))
    heads = pl.BlockSpec((1, 2, T, LANES), lambda b, p: (b, p, 0, 0))
    return pl.pallas_call(
        _mla_attn_kernel,
        out_shape=jax.ShapeDtypeStruct((BATCH, T, A_WIDTH), BF16),
        grid=(BATCH, A_HEADS // 2),
        in_specs=[heads, heads, pair, pair],
        out_specs=pair,
        compiler_params=_params(("parallel", "parallel")),
        name="mla_attn",
    )(q, k, v, za)


def _gla_direction(q_ref, k_ref, v_ref, g_ref, o_ref, s_ref, tile, forward):
    rows = pl.ds(pl.multiple_of(tile * TILE, TILE), TILE)
    q = q_ref[0, rows, :]
    k = k_ref[0, rows, :]
    g = g_ref[0, rows, :]
    v = v_ref[0, rows, :]
    row = lax.broadcasted_iota(jnp.int32, (TILE, TILE), 0)
    col = lax.broadcasted_iota(jnp.int32, (TILE, TILE), 1)
    in_chunk_causal = ((row // CHUNK) == (col // CHUNK)) & ((col <= row) if forward else (col >= row))
    tri = in_chunk_causal.astype(BF16)
    g_hi = g.astype(BF16)
    g_lo = (g - g_hi.astype(F32)).astype(BF16)
    b = _dot(tri, g_hi) + _dot(tri, g_lo)

    n_chunks = TILE // CHUNK
    chunks = [slice(c * CHUNK, (c + 1) * CHUNK) for c in range(n_chunks)]
    last = CHUNK - 1 if forward else 0
    b_last = [b[c * CHUNK + last:c * CHUNK + last + 1] for c in range(n_chunks)]
    b_last_rows = jnp.concatenate([jnp.broadcast_to(bl, (CHUNK, LANES)) for bl in b_last], axis=0)
    q_t = q * jnp.exp(b)
    k_t = (k * jnp.exp(-b)).astype(BF16)
    k_dec = (k * jnp.exp(b_last_rows - b)).astype(BF16)
    yield

    lane = lax.broadcasted_iota(jnp.int32, (1, LANES), 1)
    intra = []
    for hh, head_lanes in enumerate((lane < B_DK, lane >= B_DK)):
        qm = jnp.where(head_lanes, q_t, 0.0).astype(BF16)
        att = jnp.where(in_chunk_causal, _dot_nt(qm, k_t), 0.0).astype(BF16)
        intra.append(_dot(att, v[:, hh * B_DV:(hh + 1) * B_DV]))
    yield

    srow = lax.broadcasted_iota(jnp.int32, (2 * B_DK, 2 * B_DV), 0)
    scol = lax.broadcasted_iota(jnp.int32, (2 * B_DK, 2 * B_DV), 1)
    own_head = (srow // B_DK) == (scol // B_DV)
    updates = [jnp.where(own_head, _dot_tn(k_dec[sl], v[sl]), 0.0) for sl in chunks]
    yield
    state = s_ref[...]
    starts = [None] * n_chunks
    for c in (range(n_chunks) if forward else reversed(range(n_chunks))):
        starts[c] = state.astype(BF16)
        decay = jnp.broadcast_to(jnp.exp(b_last[c]), (LANES, LANES)).T
        state = state * jnp.concatenate([decay, decay], axis=1) + updates[c]
    s_ref[...] = state
    yield
    q_tb = q_t.astype(BF16)
    inter = [_dot(q_tb[sl], starts[c]) for c, sl in enumerate(chunks)]
    o_ref[0, rows, :] += jnp.concatenate(intra, axis=1) + jnp.concatenate(inter, axis=0)


GLA_STEPS_PER_ITER = 9


def _gla_kernel(q_ref, k_ref, v_ref, gf_ref, gb_ref, o_ref, sf_ref, sb_ref):
    sf_ref[...] = jnp.zeros_like(sf_ref)
    sb_ref[...] = jnp.zeros_like(sb_ref)
    o_ref[...] = jnp.zeros_like(o_ref)

    @pl.loop(0, N_TILES // GLA_STEPS_PER_ITER)
    def _(it):
        stages = []
        for sub in range(GLA_STEPS_PER_ITER):
            step = it * GLA_STEPS_PER_ITER + sub
            fwd_tile = jnp.where(step == 0, CTX_TILE, step - 1)
            bwd_tile = CTX_TILE - step
            stages.append(_gla_direction(q_ref, k_ref, v_ref, gf_ref, o_ref, sf_ref, fwd_tile, True))
            stages.append(_gla_direction(q_ref, k_ref, v_ref, gb_ref, o_ref, sb_ref, bwd_tile, False))
        while stages:
            stages = [s for s in stages if next(s, StopIteration) is not StopIteration]


def _gla(gq, gk, gv, gf, gb):
    kspec = pl.BlockSpec((1, T, LANES), lambda b, p: (b, 0, p))
    vspec = pl.BlockSpec((1, T, 2 * B_DV), lambda b, p: (b, 0, p))
    out = jax.ShapeDtypeStruct((BATCH, T, B_WIDTH), F32)
    return pl.pallas_call(
        _gla_kernel,
        out_shape=out,
        grid=(BATCH, B_HEADS // 2),
        in_specs=[kspec, kspec, vspec, kspec, kspec],
        out_specs=vspec,
        scratch_shapes=[pltpu.VMEM((2 * B_DK, 2 * B_DV), F32), pltpu.VMEM((2 * B_DK, 2 * B_DV), F32)],
        compiler_params=_params(("parallel", "parallel")),
        name="gla_scan",
    )(gq, gk, gv, gf, gb)


def _even_out_body(ins):
    a_ref, gla_ref, zb_ref, gn_ref, wa_ref, wb_ref, xl_ref, xc_ref, mod_ref = ins
    bsum = gla_ref[0]
    zb = zb_ref[0]
    parts = []
    for hh in range(B_HEADS):
        sl = slice(hh * B_DV, (hh + 1) * B_DV)
        parts.append((_rms(bsum[:, sl], B_DV) * gn_ref[:, sl] * _silu(zb[:, sl])).astype(BF16))
    y = _dot(a_ref[0], wa_ref[0]) + _dot(jnp.concatenate(parts, axis=1), wb_ref[0])
    return _stream_tile(xl_ref, xc_ref) + mod_ref[0, 2:3, :] * y


def _layer_spec(layer, shape, block=0):
    return pl.BlockSpec((1,) + shape, lambda b, t: (layer, block, 0))


def _even_out_part(a, gla, zb, gn, w_out, layer, stream, mod):
    x_lat, x_ctx, ctx_block = stream
    tok = _row_spec
    return _Part(
        arrays=(a, gla, zb, gn, w_out, w_out, x_lat, x_ctx, mod),
        in_specs=(tok(A_WIDTH), tok(B_WIDTH), tok(B_WIDTH), _const_spec((1, B_WIDTH)),
                  _layer_spec(layer, (A_WIDTH, D), 0), _layer_spec(layer, (B_WIDTH, D), 1),
                  *_stream_specs(ctx_block),
                  pl.BlockSpec((1, 3, D), _mod_index)),
        out_shapes=(), out_specs=(), body=_even_out_body, writes_stream=True)


def _odd_in_body(x, ins, outs):
    mod_ref, ng_ref, win_ref, qg_ref, kg_ref = ins
    q_ref, k_ref, v_ref, z_ref = outs
    h = _modulated_norm(x, mod_ref, ng_ref)
    u = _dot(h.astype(BF16), win_ref[0])
    ss = _group_sumsq(u[:, :2 * C_WIDTH], C_HD, C_HD)
    scale = lax.rsqrt(ss * (1.0 / C_HD) + EPS)
    for j in range(C_WIDTH // LANES):
        sl = slice(j * LANES, (j + 1) * LANES)
        q_ref[0, :, sl] = (u[:, sl] * scale[:, sl] * qg_ref[...] * (C_HD ** -0.5 * LOG2E)).astype(BF16)
        ks = slice(C_WIDTH + j * LANES, C_WIDTH + (j + 1) * LANES)
        k_ref[0, :, sl] = (u[:, ks] * scale[:, ks] * kg_ref[...]).astype(BF16)
    v_ref[0] = u[:, 2 * C_WIDTH:3 * C_WIDTH].astype(BF16)
    z_ref[0] = u[:, 3 * C_WIDTH:]


def _odd_in_part(mod, ng, win, layer, qg, kg):
    tok = _row_spec
    sd = jax.ShapeDtypeStruct
    return _Part(
        arrays=(mod, ng, win, qg, kg),
        in_specs=(pl.BlockSpec((1, 3, D), _mod_index), _const_spec((1, D)), _layer_spec(layer, (D, 4 * C_WIDTH)),
                  _const_spec((1, LANES)), _const_spec((1, LANES))),
        out_shapes=(sd((BATCH, T, C_WIDTH), BF16), sd((BATCH, T, C_WIDTH), BF16), sd((BATCH, T, C_WIDTH), BF16),
                    sd((BATCH, T, C_WIDTH), F32)),
        out_specs=(tok(C_WIDTH), tok(C_WIDTH), tok(C_WIDTH), tok(C_WIDTH)),
        body=_odd_in_body)


ROWS_PER_CHAIN = 2
ROWS_PER_STEP = 32
SCORE_LEAD = 1
UNION_ROWS = WIN_ROWS + ROWS_PER_CHAIN
UNION_KEYS = UNION_ROWS * GRID_W
BIAS_SLOTS = 2 * WIN_ROWS


def _nbr_attn_kernel(q_ref, k_ref, v_ref, z_ref, bias_ref, o_ref):
    lane = lax.broadcasted_iota(jnp.int32, (1, LANES), 1)
    first = lane < C_HD
    head_mask = (first, lane >= C_HD)
    ctx = slice(SEQ, T)

    def stack_heads(blocks):
        zero = jnp.zeros(blocks[0].shape, BF16)
        return jnp.concatenate([jnp.where(head_mask[hh], blk, zero) for blk in blocks for hh in range(2)], axis=0)

    def probabilities(s):
        m = jnp.max(s, axis=-1, keepdims=True)
        return jnp.exp2(s - m).astype(BF16)

    def finish(p, v_all, n_blocks, rows, out_rows):
        o = _normalise(_dot(p, _values_with_ones(v_all)))
        o = jnp.concatenate(
            [jnp.where(first, o[(2 * i) * rows:(2 * i + 1) * rows], o[(2 * i + 1) * rows:(2 * i + 2) * rows])
             for i in range(n_blocks)], axis=0)
        o_ref[0, out_rows, :] = (o * _silu(z_ref[0, out_rows, :])).astype(BF16)

    def chain_keys(r0):
        u0 = jnp.clip(r0 - WIN_ROWS // 2, 0, GRID_ROWS - UNION_ROWS)
        return u0, pl.ds(pl.multiple_of(u0 * GRID_W, 2 * GRID_W), UNION_KEYS)

    def query_rows(r0):
        return pl.ds(pl.multiple_of(r0 * GRID_W, ROWS_PER_CHAIN * GRID_W), ROWS_PER_CHAIN * GRID_W)

    def raw_scores(r0, k_ctx):
        q_pair = q_ref[0, query_rows(r0), :]
        q_rows = [q_pair[rr * GRID_W:(rr + 1) * GRID_W] for rr in range(ROWS_PER_CHAIN)]
        k_all = jnp.concatenate([k_ref[0, chain_keys(r0)[1], :], k_ctx], axis=0)
        return _dot_nt(stack_heads(q_rows), k_all)

    def biased_scores(r0, s):
        u0, _ = chain_keys(r0)
        blocks = []
        for rr in range(ROWS_PER_CHAIN):
            r = r0 + rr
            r_start = jnp.clip(r - WIN_ROWS // 2, 0, GRID_ROWS - WIN_ROWS)
            for hh in range(2):
                blk = s[(2 * rr + hh) * GRID_W:(2 * rr + hh + 1) * GRID_W]
                pieces = []
                for mm in range(UNION_ROWS // 2):
                    u = u0 + 2 * mm
                    ok_l = ((u >= r_start) & (u < r_start + WIN_ROWS)).astype(jnp.int32)
                    ok_r = ((u + 1 >= r_start) & (u + 1 < r_start + WIN_ROWS)).astype(jnp.int32)
                    slot = jnp.clip(u - r + WIN_ROWS, 0, BIAS_SLOTS - 1)
                    in_window = jnp.where(first, ok_l, ok_r) > 0
                    scores = blk[:, mm * LANES:(mm + 1) * LANES] + bias_ref[0, hh, slot]
                    pieces.append(jnp.where(in_window, scores, NEG))
                pieces.append(blk[:, UNION_KEYS:])
                blocks.append(jnp.concatenate(pieces, axis=1))
        return jnp.concatenate(blocks, axis=0)

    @pl.loop(0, GRID_ROWS // ROWS_PER_STEP)
    def _(t):
        k_ctx = k_ref[0, ctx, :]
        v_ctx = v_ref[0, ctx, :]
        firsts = [t * ROWS_PER_STEP + chain * ROWS_PER_CHAIN for chain in range(ROWS_PER_STEP // ROWS_PER_CHAIN)]
        raw = [raw_scores(r0, k_ctx) for r0 in firsts[:SCORE_LEAD]]
        for i, r0 in enumerate(firsts):
            p = probabilities(biased_scores(r0, raw[i]))
            if i + SCORE_LEAD < len(firsts):
                raw.append(raw_scores(firsts[i + SCORE_LEAD], k_ctx))
            v_all = jnp.concatenate([v_ref[0, chain_keys(r0)[1], :], v_ctx], axis=0)
            finish(p, v_all, ROWS_PER_CHAIN, GRID_W, query_rows(r0))

    if o_ref.shape[1] == T:
        p = probabilities(_dot_nt(stack_heads([q_ref[0, ctx, :]]), k_ref[0, ctx, :]))
        finish(p, v_ref[0, ctx, :], 1, CTX, ctx)


def _nbr_attn(q, k, v, z, bias, update_ctx):
    pair = pl.BlockSpec((1, T, LANES), lambda p, b: (b, 0, p))
    out_rows = T if update_ctx else SEQ
    return pl.pallas_call(
        _nbr_attn_kernel,
        out_shape=jax.ShapeDtypeStruct((BATCH, out_rows, C_WIDTH), BF16),
        grid=(C_HEADS // 2, BATCH),
        in_specs=[pair, pair, pair, pair,
                  pl.BlockSpec((1, 2, BIAS_SLOTS, GRID_W, LANES), lambda p, b: (p, 0, 0, 0, 0))],
        out_specs=pl.BlockSpec((1, out_rows, LANES), lambda p, b: (b, 0, p)),
        compiler_params=_params(("parallel", "parallel")),
        name="nbr_attn",
    )(q, k, v, z, bias)


def _odd_out_body(ins):
    o_ref, w_ref, x_ref, mod_ref = ins
    return x_ref[0] + mod_ref[0, 2:3, :] * _dot(o_ref[0], w_ref[0])


def _odd_out_part(o, w, layer, xu, mod):
    return _Part(
        arrays=(o, w, xu, mod),
        in_specs=(_row_spec(C_WIDTH), _layer_spec(layer, (C_WIDTH, D)), _row_spec(D),
                  pl.BlockSpec((1, 3, D), _mod_index)),
        out_shapes=(), out_specs=(), body=_odd_out_body, writes_stream=True)


HALF = A_ROPE // 2


def _head_slot(nope, rope):
    n_freq = HALF // 2
    block = lambda axis, half: rope[..., (2 * axis + half) * n_freq:(2 * axis + half + 1) * n_freq]
    half0 = [block(0, 0), block(1, 0)]
    half1 = [block(0, 1), block(1, 1)]
    pad = jnp.zeros(rope.shape[:-1] + (LANES - A_QK - HALF,), rope.dtype)
    return jnp.concatenate([nope] + half0 + half1 + half0 + [pad], axis=-1)


def _rope_tables():
    n = np.arange(SEQ)
    pos = np.stack([n // GRID_W, n % GRID_W], axis=1).astype(np.float32)
    inv = (ROPE_THETA ** (-np.arange(0, HALF, 2, dtype=np.float32) / HALF)).astype(np.float32)
    ang = (pos[:, :, None] * inv).astype(np.float64).reshape(SEQ, HALF)
    cos = np.ones((T, LANES), np.float32)
    sin = np.zeros((T, LANES), np.float32)
    cos[:, A_QK:] = 0.0
    cos[:SEQ, A_NOPE:A_QK] = np.tile(np.cos(ang), (1, 2))
    sin[:SEQ, A_NOPE:A_QK] = np.concatenate([-np.sin(ang), np.sin(ang)], axis=1)
    return jnp.asarray(cos), jnp.asarray(sin)


def _pad_lanes(v, width=LANES):
    return jnp.pad(v, (0, width - v.shape[0])).reshape(1, width)


EVEN_IN_RAW = 3136
EVEN_RUNS = ((0, 1024), (1056, 2592), (2624, 3136))
PREP_ROWS = 256


def _even_w_in_kernel(w_ref, o_ref):
    w = w_ref[0]
    at = 0
    for lo, hi in EVEN_RUNS:
        o_ref[:, at:at + hi - lo] = w[:, lo:hi].astype(BF16)
        at += hi - lo


def _even_w_in(ev_w_in, layer):
    return pl.pallas_call(
        _even_w_in_kernel,
        out_shape=jax.ShapeDtypeStruct((D, EVEN_COLS), BF16),
        grid=(D // PREP_ROWS,),
        in_specs=[pl.BlockSpec((1, PREP_ROWS, EVEN_IN_RAW), lambda r: (layer, r, 0))],
        out_specs=pl.BlockSpec((PREP_ROWS, EVEN_COLS), lambda r: (r, 0)),
        compiler_params=_params(("parallel",)),
        name="even_w_in",
    )(ev_w_in)


def _even_weights(j, p, rope):
    k_r = p["ev_w_in"][j, :, 1024:1056]
    lr = p["ev_w_in"][j, :, 2592:2624]
    zeros = jnp.zeros((D, 32), F32)
    misc = jnp.concatenate([lr, zeros, _head_slot(jnp.zeros((D, 0), F32), k_r)], axis=1)
    w_uq = p["ev_w_uq"][j].reshape(Q_LORA, A_HEADS, A_QK)
    w_uq = _head_slot(w_uq[..., :A_NOPE], w_uq[..., A_NOPE:])
    w_ukv = p["ev_w_ukv"][j].reshape(KV_LORA, A_HEADS, A_NOPE + A_V)
    w_uk = jnp.pad(w_ukv[:, :, :A_NOPE], ((0, 0), (0, 0), (0, LANES - A_NOPE))).reshape(KV_LORA, A_HEADS * LANES)
    w_uv = w_ukv[:, :, A_NOPE:].reshape(KV_LORA, A_WIDTH)
    gate_w = p["ev_gate_w"][j]
    w_gate = jnp.zeros((LANES, 2 * B_KW), F32)
    w_gate = w_gate.at[:GATE_RANK, :B_KW].set(gate_w[0]).at[GATE_RANK:2 * GATE_RANK, B_KW:].set(gate_w[1])
    return {
        "w_in": _even_w_in(p["ev_w_in"], j),
        "w_misc": misc.astype(BF16),
        "q_norm": p["ev_q_norm"][j].reshape(1, Q_LORA),
        "w_uq": w_uq.reshape(Q_LORA, A_HEADS * LANES).astype(BF16),
        "kv_norm": p["ev_kv_norm"][j].reshape(1, KV_LORA),
        "w_ukv": jnp.concatenate([w_uk, w_uv], axis=1).astype(BF16),
        "q_gain": _head_slot(p["ev_q_gain"][j][:A_NOPE], p["ev_q_gain"][j][A_NOPE:]).reshape(1, LANES),
        "k_gain": _head_slot(p["ev_k_gain"][j][:A_NOPE], p["ev_k_gain"][j][A_NOPE:]).reshape(1, LANES),
        "w_gate": w_gate.astype(BF16),
        "gate_b": p["ev_gate_b"][j].reshape(1, 2 * B_KW),
        "cos": rope[0], "sin": rope[1],
        "gla_norm": p["ev_gla_norm"][j].reshape(1, B_WIDTH),
    }


def _nbr_bias_table(rpb):
    col = np.arange(GRID_W)
    col_start = np.clip(col - WIN_COLS // 2, 0, GRID_W - WIN_COLS)
    col_mask = (col[None, :] >= col_start[:, None]) & (col[None, :] < col_start[:, None] + WIN_COLS)
    col_idx = np.clip(col[None, :] - col[:, None] + WIN_COLS - 1, 0, 2 * WIN_COLS - 2)
    n_rel = 2 * WIN_COLS - 1
    n_rows = 2 * WIN_ROWS - 1
    pick = ((col_idx[None] == np.arange(n_rel)[:, None, None]) & col_mask[None]) * LOG2E
    select = np.zeros((2 * n_rel + 3, GRID_W, 2, GRID_W), np.float32)
    for half in range(2):
        select[half * n_rel:(half + 1) * n_rel, :, half, :] = pick
        select[2 * n_rel, :, half, :] = np.where(col_mask, 0.0, NEG)
        select[2 * n_rel + 1 + half, :, half, :] = NEG
    slots = np.arange(BIAS_SLOTS)
    outside = np.stack([slots - 1 < 0, slots >= n_rows], axis=1).astype(np.float32)
    ext = jnp.pad(rpb, ((0, 0), (1, 1), (0, 0)))
    flags = jnp.broadcast_to(jnp.asarray(np.concatenate([np.ones((BIAS_SLOTS, 1), np.float32), outside], axis=1)),
                             (C_HEADS, BIAS_SLOTS, 3))
    lhs = jnp.concatenate([ext[:, :-1], ext[:, 1:], flags], axis=-1).reshape(C_HEADS * BIAS_SLOTS, 2 * n_rel + 3)
    table = jnp.dot(lhs, jnp.asarray(select.reshape(2 * n_rel + 3, -1)), precision=lax.Precision.HIGHEST)
    return table.reshape(C_HEADS // 2, 2, BIAS_SLOTS, GRID_W, LANES)


def kernel(x, c, ctx, c_ctx, norm_g, ada_w, ada_b, ev_w_in, ev_q_norm, ev_w_uq, ev_kv_norm, ev_w_ukv, ev_q_gain,
           ev_k_gain, ev_gate_w, ev_gate_b, ev_gla_norm, ev_w_out, od_w_in, od_q_gain, od_k_gain, od_rpb, od_w_out):
    p = dict(ev_w_in=ev_w_in, ev_q_norm=ev_q_norm, ev_w_uq=ev_w_uq, ev_kv_norm=ev_kv_norm, ev_w_ukv=ev_w_ukv,
             ev_q_gain=ev_q_gain, ev_k_gain=ev_k_gain, ev_gate_w=ev_gate_w, ev_gate_b=ev_gate_b,
             ev_gla_norm=ev_gla_norm, ev_w_out=ev_w_out)
    c16 = jnp.zeros((MOD_ROWS, D), F32).at[:BATCH].set(c).at[BATCH].set(c_ctx)
    mod_all = _modulation(c16, ada_w, ada_b).reshape(DEPTH, MOD_ROWS, 3, D)
    rope = _rope_tables()
    od_w_in_b, od_w_out_b, ev_w_out_b = od_w_in.astype(BF16), od_w_out.astype(BF16), ev_w_out.astype(BF16)
    xu = None
    finish, finish_name = _source_part(x, ctx), "inputs"
    for i in range(DEPTH):
        j = i // 2
        mod = mod_all[i]
        ng = norm_g[i].reshape(1, D)
        if i % 2 == 0:
            w = _even_weights(j, p, rope)
            start, start_name = _even_in_part(mod, ng, w), "even_in"
        else:
            start = _odd_in_part(mod, ng, od_w_in_b, j, _pad_lanes(jnp.tile(od_q_gain[j], 2)),
                                 _pad_lanes(jnp.tile(od_k_gain[j], 2)))
            start_name = "odd_in"
        outs = _boundary_call(f"{finish_name}_{start_name}", finish, start)
        if finish.writes_stream:
            xu, outs = outs[0], outs[1:]
        if i % 2 == 0:
            q, k, v, za, gq, gk, gv, gf, gb, zb = outs
            a = _mla_attn(q, k, v, za)
            gla = _gla(gq, gk, gv, gf, gb)
            stream = (x, ctx, 0) if xu is None else (xu, xu, CTX_TILE)
            finish = _even_out_part(a, gla, zb, w["gla_norm"], ev_w_out_b, j, stream, mod)
            finish_name = "even_out"
        else:
            q, k, v, z = outs
            o = _nbr_attn(q, k, v, z, _nbr_bias_table(od_rpb[j]), update_ctx=i < DEPTH - 1)
            finish, finish_name = _odd_out_part(o, od_w_out_b, j, xu, mod), "odd_out"
    return _boundary_call(finish_name, finish, None, n_tiles=CTX_TILE)[0]
```

```python
import functools
from typing import Callable, NamedTuple

import jax
import jax.numpy as jnp
import numpy as np
from jax import lax
from jax.experimental import pallas as pl
from jax.experimental.pallas import tpu as pltpu

D = 1024
BATCH = 8
SEQ = 2048
DEPTH = 4
GRID_W = 64
GRID_ROWS = SEQ // GRID_W
CTX = 256
T = SEQ + CTX
EPS = 1e-6
ROPE_THETA = 10000.0

A_HEADS = 8
A_NOPE = 64
A_ROPE = 32
A_V = 64
A_QK = A_NOPE + A_ROPE
KV_LORA = 256
Q_LORA = 768
A_WIDTH = A_HEADS * A_V

B_HEADS = 4
B_DK = 64
B_DV = 128
B_KW = B_HEADS * B_DK
B_WIDTH = B_HEADS * B_DV
GATE_RANK = 16
GATE_TAU = 16.0
CHUNK = 64

C_HEADS = 16
C_HD = 64
C_WIDTH = C_HEADS * C_HD
WIN_ROWS = 8
WIN_COLS = 16

LANES = 128
TILE = 256
N_TILES = T // TILE
CTX_TILE = N_TILES - 1
LAST_ROWS = 512
MOD_ROWS = 16
EVEN_COLS = 3072
EVEN_EARLY = Q_LORA + KV_LORA
NEG = -1e30
LOG2E = 1.4426950408889634
VMEM_LIMIT = 56 * 1024 * 1024

F32 = jnp.float32
BF16 = jnp.bfloat16


def _dot(a, b):
    return lax.dot_general(a, b, (((1,), (0,)), ((), ())), preferred_element_type=F32)


def _dot_nt(a, b):
    return lax.dot_general(a, b, (((1,), (1,)), ((), ())), preferred_element_type=F32)


def _dot_tn(a, b):
    return lax.dot_general(a, b, (((0,), (0,)), ((), ())), preferred_element_type=F32)


def _silu(x):
    return x / (1.0 + jnp.exp(-x))


def _rms(x, n):
    return x * lax.rsqrt(jnp.sum(x * x, axis=-1, keepdims=True) * (1.0 / n) + EPS)


def _params(sem):
    return pltpu.CompilerParams(dimension_semantics=sem, vmem_limit_bytes=VMEM_LIMIT)


def _const_spec(shape):
    nd = len(shape)
    return pl.BlockSpec(shape, lambda *_: (0,) * nd)


def _mod_index(b, t):
    return (jnp.where(t == CTX_TILE, BATCH, b), 0, 0)


def _mod_kernel(c_ref, w_ref, b_ref, o_ref):
    a = _silu(c_ref[...]).astype(BF16)
    o_ref[0] = _dot(a, w_ref[0].astype(BF16)) + b_ref[0]


def _modulation(c16, ada_w, ada_b):
    return pl.pallas_call(
        _mod_kernel,
        out_shape=jax.ShapeDtypeStruct((DEPTH, MOD_ROWS, 3 * D), F32),
        grid=(DEPTH, 3),
        in_specs=[
            pl.BlockSpec((MOD_ROWS, D), lambda i, j: (0, 0)),
            pl.BlockSpec((1, D, D), lambda i, j: (i, 0, j)),
            pl.BlockSpec((1, 1, D), lambda i, j: (i, 0, j)),
        ],
        out_specs=pl.BlockSpec((1, MOD_ROWS, D), lambda i, j: (i, 0, j)),
        compiler_params=_params(("parallel", "parallel")),
        name="modulation",
    )(c16, ada_w, ada_b.reshape(DEPTH, 1, 3 * D))


def _modulated_norm(x, mod_ref, ng_ref):
    return _rms(x, D) * ng_ref[...] * (1.0 + mod_ref[0, 1:2, :]) + mod_ref[0, 0:1, :]


def _group_sumsq(u, group, counted):
    sq = (u * u).astype(BF16)
    width = 2 * LANES
    row = lax.broadcasted_iota(jnp.int32, (width, width), 0)
    col = lax.broadcasted_iota(jnp.int32, (width, width), 1)
    ones = ((row // group == col // group) & (row % group < counted)).astype(BF16)
    cols = [slice(c, c + width) for c in range(0, u.shape[1], width)]
    return jnp.concatenate([_dot(sq[:, c], ones) for c in cols], axis=1)


def _slot_sumsq(u):
    return _group_sumsq(u, LANES, A_QK)


def _stream_specs(ctx_block):
    return [pl.BlockSpec((1, TILE, D), lambda b, t: (b, jnp.minimum(t, CTX_TILE - 1), 0)),
            pl.BlockSpec((1, TILE, D), lambda b, t: (b, ctx_block, 0))]


def _stream_tile(lat_ref, ctx_ref):
    return jnp.where(pl.program_id(1) == CTX_TILE, ctx_ref[0], lat_ref[0])


class _Part(NamedTuple):
    arrays: tuple
    in_specs: tuple
    out_shapes: tuple
    out_specs: tuple
    body: Callable
    writes_stream: bool = False


def _row_spec(width, rows=TILE):
    return pl.BlockSpec((1, rows, width), lambda b, t: (b, t, 0))


def _boundary_call(name, finish, start, n_tiles=N_TILES, rows=TILE):
    parts = [p for p in (finish, start) if p is not None]
    stream_shape = (jax.ShapeDtypeStruct((BATCH, n_tiles * rows, D), F32),) if finish.writes_stream else ()
    stream_spec = (_row_spec(D, rows),) if finish.writes_stream else ()
    n_finish_in = len(finish.arrays)
    n_in = sum(len(p.arrays) for p in parts)

    def kernel(*refs):
        ins, outs = refs[:n_in], refs[n_in:]
        x = finish.body(ins[:n_finish_in])
        if finish.writes_stream:
            outs[0][0] = x
        if start is not None:
            start.body(x, ins[n_finish_in:], outs[len(stream_shape):])

    out_shapes = stream_shape + (start.out_shapes if start is not None else ())
    out_specs = stream_spec + (start.out_specs if start is not None else ())
    return pl.pallas_call(
        kernel,
        out_shape=out_shapes,
        grid=(BATCH, n_tiles),
        in_specs=[s for p in parts for s in p.in_specs],
        out_specs=out_specs,
        compiler_params=_params(("parallel", "parallel")),
        name=name,
    )(*[a for p in parts for a in p.arrays])


def _source_part(x, ctx):
    return _Part((x, ctx), tuple(_stream_specs(0)), (), (), lambda ins: _stream_tile(*ins))


def _even_in_body(x, ins, outs):
    (mod_ref, ng_ref, win_ref, wmisc_ref, qn_ref, wuq_ref, kvn_ref, wukv_ref, qg_ref, kg_ref, wg_ref, gbias_ref,
     cos_ref, sin_ref) = ins
    q_ref, k_ref, v_ref, za_ref, gq_ref, gk_ref, gv_ref, gf_ref, gb_ref, zb_ref = outs
    h = _modulated_norm(x, mod_ref, ng_ref).astype(BF16)
    lat = _dot(h, win_ref[:, :EVEN_EARLY])
    q_lat = lat[:, 0:Q_LORA]
    kv_lat = lat[:, Q_LORA:]
    misc = _dot(h, wmisc_ref[...])

    cos = cos_ref[...]
    sin = sin_ref[...]

    def rope(u):
        return u * cos + pltpu.roll(u, LANES - HALF, 1) * sin

    qf = _dot((_rms(q_lat, Q_LORA) * qn_ref[...]).astype(BF16), wuq_ref[...])
    kvf = _dot((_rms(kv_lat, KV_LORA) * kvn_ref[...]).astype(BF16), wukv_ref[...])

    logits = _dot(misc.astype(BF16), wg_ref[...]) + gbias_ref[...]
    g = (jnp.minimum(logits, 0.0) - jnp.log1p(jnp.exp(-jnp.abs(logits)))) * (1.0 / GATE_TAU)
    gf_ref[0] = g[:, :B_KW]
    gb_ref[0] = g[:, B_KW:]

    lane = lax.broadcasted_iota(jnp.int32, (1, LANES), 1)
    rotary = (lane >= A_NOPE) & (lane < A_QK)
    kr_roped = rope(jnp.where(lane >= A_NOPE, misc, 0.0) * kg_ref[...])
    k_rope = jnp.where(rotary, misc, 0.0)
    kr_ss = jnp.sum(k_rope * k_rope, axis=-1, keepdims=True)
    q_ss = _slot_sumsq(qf)
    k_ss = _slot_sumsq(kvf[:, :A_HEADS * LANES])

    late = _dot(h, win_ref[:, EVEN_EARLY:])
    za_ref[0] = late[:, 0:512]
    gq_ref[0] = late[:, 512:768] * (B_DK ** -0.5)
    gk_ref[0] = late[:, 768:1024]
    gv_ref[0] = late[:, 1024:1536].astype(BF16)
    zb_ref[0] = late[:, 1536:2048]

    for hh in range(A_HEADS):
        sl = slice(hh * LANES, (hh + 1) * LANES)
        q_scale = lax.rsqrt(q_ss[:, sl] * (1.0 / A_QK) + EPS) * (A_QK ** -0.5 * LOG2E)
        q_ref[0, hh] = (rope(qf[:, sl] * qg_ref[...]) * q_scale).astype(BF16)
        k_scale = lax.rsqrt((k_ss[:, sl] + kr_ss) * (1.0 / A_QK) + EPS)
        k_ref[0, hh] = ((kvf[:, sl] * kg_ref[...] + kr_roped) * k_scale).astype(BF16)
    v_ref[0] = kvf[:, A_HEADS * LANES:].astype(BF16)


def _even_in_part(mod, ng, w):
    tok = _row_spec
    head = pl.BlockSpec((1, A_HEADS, TILE, LANES), lambda b, t: (b, 0, t, 0))
    rope_spec = pl.BlockSpec((TILE, LANES), lambda b, t: (t, 0))
    sd = jax.ShapeDtypeStruct
    return _Part(
        arrays=(mod, ng, w["w_in"], w["w_misc"], w["q_norm"], w["w_uq"], w["kv_norm"], w["w_ukv"], w["q_gain"],
                w["k_gain"], w["w_gate"], w["gate_b"], w["cos"], w["sin"]),
        in_specs=(
            pl.BlockSpec((1, 3, D), _mod_index),
            _const_spec((1, D)),
            _const_spec((D, EVEN_COLS)),
            _const_spec((D, LANES)),
            _const_spec((1, Q_LORA)),
            _const_spec((Q_LORA, A_HEADS * LANES)),
            _const_spec((1, KV_LORA)),
            _const_spec((KV_LORA, A_HEADS * LANES + A_WIDTH)),
            _const_spec((1, LANES)),
            _const_spec((1, LANES)),
            _const_spec((LANES, 2 * B_KW)),
            _const_spec((1, 2 * B_KW)),
            rope_spec, rope_spec,
        ),
        out_shapes=(
            sd((BATCH, A_HEADS, T, LANES), BF16),
            sd((BATCH, A_HEADS, T, LANES), BF16),
            sd((BATCH, T, A_WIDTH), BF16),
            sd((BATCH, T, A_WIDTH), F32),
            sd((BATCH, T, B_KW), F32),
            sd((BATCH, T, B_KW), F32),
            sd((BATCH, T, B_WIDTH), BF16),
            sd((BATCH, T, B_KW), F32),
            sd((BATCH, T, B_KW), F32),
            sd((BATCH, T, B_WIDTH), F32),
        ),
        out_specs=(head, head, tok(A_WIDTH), tok(A_WIDTH), tok(B_KW), tok(B_KW), tok(B_WIDTH),
                   tok(B_KW), tok(B_KW), tok(B_WIDTH)),
        body=_even_in_body,
    )


def _values_with_ones(v):
    return jnp.concatenate([v, jnp.ones(v.shape, BF16)], axis=1)


def _normalise(acc):
    return acc[:, :LANES] / acc[:, LANES:]


MLA_TILES_PER_STEP = 8


def _mla_attn_kernel(q_ref, k_ref, v_ref, z_ref, o_ref):
    first_head = lax.broadcasted_iota(jnp.int32, (1, LANES), 1) < A_V

    def attend(row_tiles, key_groups):
        rowmax = lambda s: functools.reduce(jnp.maximum, [jnp.max(x, axis=-1, keepdims=True) for x in s])
        chains = [(rows, hh) for rows in row_tiles for hh in range(2)]
        n, groups = len(chains), range(len(key_groups))
        scores, top, probs, acc = {}, {}, {}, {}
        for step in range(n + 2):
            if step >= 1 and step - 1 < n:
                top[step - 1] = rowmax(scores[step - 1])
            for i in groups:
                g = key_groups[i]
                if step < n:
                    rows, hh = chains[step]
                    scores.setdefault(step, []).append(_dot_nt(q_ref[0, hh, rows, :], k_ref[0, hh, g, :]))
                if 0 <= step - 1 < n:
                    probs.setdefault(step - 1, []).append(
                        jnp.exp2(scores[step - 1][i] - top[step - 1]).astype(BF16))
                if 0 <= step - 2 < n:
                    acc.setdefault(step - 2, []).append(_dot(probs[step - 2][i], _values_with_ones(v_ref[0, g, :])))
            done = step - 2
            if done >= 0 and done % 2 == 1:
                rows = chains[done][0]
                o0 = _normalise(functools.reduce(jnp.add, acc[done - 1]))
                o1 = _normalise(functools.reduce(jnp.add, acc[done]))
                o_ref[0, rows, :] = (jnp.where(first_head, o0, o1) * _silu(z_ref[0, rows, :])).astype(BF16)

    @pl.loop(0, CTX_TILE // MLA_TILES_PER_STEP)
    def _(t):
        base = t * (MLA_TILES_PER_STEP * TILE)
        tiles = [pl.ds(pl.multiple_of(base + i * TILE, TILE), TILE) for i in range(MLA_TILES_PER_STEP)]
        attend(tiles, [slice(0, 768), slice(768, 1536), slice(1536, T)])

    attend([slice(SEQ, T)], [slice(SEQ, T)])


def _mla_attn(q, k, v, za):
    pair = pl.BlockSpec((1, T, LANES), lambda b, p: (b, 0, p))
    heads = pl.BlockSpec((1, 2, T, LANES), lambda b, p: (b, p, 0, 0))
    return pl.pallas_call(
        _mla_attn_kernel,
        out_shape=jax.ShapeDtypeStruct((BATCH, T, A_WIDTH), BF16),
        grid=(BATCH, A_HEADS // 2),
        in_specs=[heads, heads, pair, pair],
        out_specs=pair,
        compiler_params=_params(("parallel", "parallel")),
        name="mla_attn",
    )(q, k, v, za)


def _gla_direction(q_ref, k_ref, v_ref, g_ref, o_ref, s_ref, tile, forward):
    rows = pl.ds(pl.multiple_of(tile * TILE, TILE), TILE)
    q = q_ref[0, rows, :]
    k = k_ref[0, rows, :]
    g = g_ref[0, rows, :]
    v = v_ref[0, rows, :]
    row = lax.broadcasted_iota(jnp.int32, (TILE, TILE), 0)
    col = lax.broadcasted_iota(jnp.int32, (TILE, TILE), 1)
    in_chunk_causal = ((row // CHUNK) == (col // CHUNK)) & ((col <= row) if forward else (col >= row))
    tri = in_chunk_causal.astype(BF16)
    g_hi = g.astype(BF16)
    g_lo = (g - g_hi.astype(F32)).astype(BF16)
    b = _dot(tri, g_hi) + _dot(tri, g_lo)

    n_chunks = TILE // CHUNK
    chunks = [slice(c * CHUNK, (c + 1) * CHUNK) for c in range(n_chunks)]
    last = CHUNK - 1 if forward else 0
    b_last = [b[c * CHUNK + last:c * CHUNK + last + 1] for c in range(n_chunks)]
    b_last_rows = jnp.concatenate([jnp.broadcast_to(bl, (CHUNK, LANES)) for bl in b_last], axis=0)
    q_t = q * jnp.exp(b)
    k_t = (k * jnp.exp(-b)).astype(BF16)
    k_dec = (k * jnp.exp(b_last_rows - b)).astype(BF16)
    yield

    lane = lax.broadcasted_iota(jnp.int32, (1, LANES), 1)
    intra = []
    for hh, head_lanes in enumerate((lane < B_DK, lane >= B_DK)):
        qm = jnp.where(head_lanes, q_t, 0.0).astype(BF16)
        att = jnp.where(in_chunk_causal, _dot_nt(qm, k_t), 0.0).astype(BF16)
        intra.append(_dot(att, v[:, hh * B_DV:(hh + 1) * B_DV]))
    yield

    srow = lax.broadcasted_iota(jnp.int32, (2 * B_DK, 2 * B_DV), 0)
    scol = lax.broadcasted_iota(jnp.int32, (2 * B_DK, 2 * B_DV), 1)
    own_head = (srow // B_DK) == (scol // B_DV)
    updates = [jnp.where(own_head, _dot_tn(k_dec[sl], v[sl]), 0.0) for sl in chunks]
    yield
    state = s_ref[...]
    starts = [None] * n_chunks
    for c in (range(n_chunks) if forward else reversed(range(n_chunks))):
        starts[c] = state.astype(BF16)
        decay = jnp.broadcast_to(jnp.exp(b_last[c]), (LANES, LANES)).T
        state = state * jnp.concatenate([decay, decay], axis=1) + updates[c]
    s_ref[...] = state
    yield
    q_tb = q_t.astype(BF16)
    inter = [_dot(q_tb[sl], starts[c]) for c, sl in enumerate(chunks)]
    o_ref[0, rows, :] += jnp.concatenate(intra, axis=1) + jnp.concatenate(inter, axis=0)


GLA_STEPS_PER_ITER = 9


def _gla_kernel(q_ref, k_ref, v_ref, gf_ref, gb_ref, o_ref, sf_ref, sb_ref):
    sf_ref[...] = jnp.zeros_like(sf_ref)
    sb_ref[...] = jnp.zeros_like(sb_ref)
    o_ref[...] = jnp.zeros_like(o_ref)

    @pl.loop(0, N_TILES // GLA_STEPS_PER_ITER)
    def _(it):
        stages = []
        for sub in range(GLA_STEPS_PER_ITER):
            step = it * GLA_STEPS_PER_ITER + sub
            fwd_tile = jnp.where(step == 0, CTX_TILE, step - 1)
            bwd_tile = CTX_TILE - step
            stages.append(_gla_direction(q_ref, k_ref, v_ref, gf_ref, o_ref, sf_ref, fwd_tile, True))
            stages.append(_gla_direction(q_ref, k_ref, v_ref, gb_ref, o_ref, sb_ref, bwd_tile, False))
        while stages:
            stages = [s for s in stages if next(s, StopIteration) is not StopIteration]


def _gla(gq, gk, gv, gf, gb):
    kspec = pl.BlockSpec((1, T, LANES), lambda b, p: (b, 0, p))
    vspec = pl.BlockSpec((1, T, 2 * B_DV), lambda b, p: (b, 0, p))
    out = jax.ShapeDtypeStruct((BATCH, T, B_WIDTH), F32)
    return pl.pallas_call(
        _gla_kernel,
        out_shape=out,
        grid=(BATCH, B_HEADS // 2),
        in_specs=[kspec, kspec, vspec, kspec, kspec],
        out_specs=vspec,
        scratch_shapes=[pltpu.VMEM((2 * B_DK, 2 * B_DV), F32), pltpu.VMEM((2 * B_DK, 2 * B_DV), F32)],
        compiler_params=_params(("parallel", "parallel")),
        name="gla_scan",
    )(gq, gk, gv, gf, gb)


def _even_out_body(ins):
    a_ref, gla_ref, zb_ref, gn_ref, wa_ref, wb_ref, xl_ref, xc_ref, mod_ref = ins
    bsum = gla_ref[0]
    zb = zb_ref[0]
    parts = []
    for hh in range(B_HEADS):
        sl = slice(hh * B_DV, (hh + 1) * B_DV)
        parts.append((_rms(bsum[:, sl], B_DV) * gn_ref[:, sl] * _silu(zb[:, sl])).astype(BF16))
    y = _dot(a_ref[0], wa_ref[0]) + _dot(jnp.concatenate(parts, axis=1), wb_ref[0])
    return _stream_tile(xl_ref, xc_ref) + mod_ref[0, 2:3, :] * y


def _layer_spec(layer, shape, block=0):
    return pl.BlockSpec((1,) + shape, lambda b, t: (layer, block, 0))


def _even_out_part(a, gla, zb, gn, w_out, layer, stream, mod):
    x_lat, x_ctx, ctx_block = stream
    tok = _row_spec
    return _Part(
        arrays=(a, gla, zb, gn, w_out, w_out, x_lat, x_ctx, mod),
        in_specs=(tok(A_WIDTH), tok(B_WIDTH), tok(B_WIDTH), _const_spec((1, B_WIDTH)),
                  _layer_spec(layer, (A_WIDTH, D), 0), _layer_spec(layer, (B_WIDTH, D), 1),
                  *_stream_specs(ctx_block),
                  pl.BlockSpec((1, 3, D), _mod_index)),
        out_shapes=(), out_specs=(), body=_even_out_body, writes_stream=True)


def _odd_in_body(x, ins, outs):
    mod_ref, ng_ref, win_ref, qg_ref, kg_ref = ins
    q_ref, k_ref, v_ref, z_ref = outs
    h = _modulated_norm(x, mod_ref, ng_ref)
    u = _dot(h.astype(BF16), win_ref[0])
    ss = _group_sumsq(u[:, :2 * C_WIDTH], C_HD, C_HD)
    scale = lax.rsqrt(ss * (1.0 / C_HD) + EPS)
    for j in range(C_WIDTH // LANES):
        sl = slice(j * LANES, (j + 1) * LANES)
        q_ref[0, :, sl] = (u[:, sl] * scale[:, sl] * qg_ref[...] * (C_HD ** -0.5 * LOG2E)).astype(BF16)
        ks = slice(C_WIDTH + j * LANES, C_WIDTH + (j + 1) * LANES)
        k_ref[0, :, sl] = (u[:, ks] * scale[:, ks] * kg_ref[...]).astype(BF16)
    v_ref[0] = u[:, 2 * C_WIDTH:3 * C_WIDTH].astype(BF16)
    z_ref[0] = u[:, 3 * C_WIDTH:]


def _odd_in_part(mod, ng, win, layer, qg, kg):
    tok = _row_spec
    sd = jax.ShapeDtypeStruct
    return _Part(
        arrays=(mod, ng, win, qg, kg),
        in_specs=(pl.BlockSpec((1, 3, D), _mod_index), _const_spec((1, D)), _layer_spec(layer, (D, 4 * C_WIDTH)),
                  _const_spec((1, LANES)), _const_spec((1, LANES))),
        out_shapes=(sd((BATCH, T, C_WIDTH), BF16), sd((BATCH, T, C_WIDTH), BF16), sd((BATCH, T, C_WIDTH), BF16),
                    sd((BATCH, T, C_WIDTH), F32)),
        out_specs=(tok(C_WIDTH), tok(C_WIDTH), tok(C_WIDTH), tok(C_WIDTH)),
        body=_odd_in_body)


ROWS_PER_CHAIN = 2
ROWS_PER_STEP = 32
SCORE_LEAD = 2
UNION_ROWS = WIN_ROWS + ROWS_PER_CHAIN
UNION_KEYS = UNION_ROWS * GRID_W
BIAS_SLOTS = 2 * WIN_ROWS


def _nbr_attn_kernel(q_ref, k_ref, v_ref, z_ref, bias_ref, o_ref):
    lane = lax.broadcasted_iota(jnp.int32, (1, LANES), 1)
    first = lane < C_HD
    head_mask = (first, lane >= C_HD)
    ctx = slice(SEQ, T)

    def stack_heads(blocks):
        zero = jnp.zeros(blocks[0].shape, BF16)
        return jnp.concatenate([jnp.where(head_mask[hh], blk, zero) for blk in blocks for hh in range(2)], axis=0)

    def probabilities(s):
        m = jnp.max(s, axis=-1, keepdims=True)
        return jnp.exp2(s - m).astype(BF16)

    def finish(p, v_all, n_blocks, rows, out_rows):
        o = _normalise(_dot(p, _values_with_ones(v_all)))
        o = jnp.concatenate(
            [jnp.where(first, o[(2 * i) * rows:(2 * i + 1) * rows], o[(2 * i + 1) * rows:(2 * i + 2) * rows])
             for i in range(n_blocks)], axis=0)
        o_ref[0, out_rows, :] = (o * _silu(z_ref[0, out_rows, :])).astype(BF16)

    def chain_keys(r0):
        u0 = jnp.clip(r0 - WIN_ROWS // 2, 0, GRID_ROWS - UNION_ROWS)
        return u0, pl.ds(pl.multiple_of(u0 * GRID_W, 2 * GRID_W), UNION_KEYS)

    def query_rows(r0):
        return pl.ds(pl.multiple_of(r0 * GRID_W, ROWS_PER_CHAIN * GRID_W), ROWS_PER_CHAIN * GRID_W)

    def raw_scores(r0, k_ctx):
        q_pair = q_ref[0, query_rows(r0), :]
        q_rows = [q_pair[rr * GRID_W:(rr + 1) * GRID_W] for rr in range(ROWS_PER_CHAIN)]
        k_all = jnp.concatenate([k_ref[0, chain_keys(r0)[1], :], k_ctx], axis=0)
        return _dot_nt(stack_heads(q_rows), k_all)

    def biased_scores(r0, s):
        u0, _ = chain_keys(r0)
        blocks = []
        for rr in range(ROWS_PER_CHAIN):
            r = r0 + rr
            r_start = jnp.clip(r - WIN_ROWS // 2, 0, GRID_ROWS - WIN_ROWS)
            for hh in range(2):
                blk = s[(2 * rr + hh) * GRID_W:(2 * rr + hh + 1) * GRID_W]
                pieces = []
                for mm in range(UNION_ROWS // 2):
                    u = u0 + 2 * mm
                    ok_l = ((u >= r_start) & (u < r_start + WIN_ROWS)).astype(jnp.int32)
                    ok_r = ((u + 1 >= r_start) & (u + 1 < r_start + WIN_ROWS)).astype(jnp.int32)
                    slot = jnp.clip(u - r + WIN_ROWS, 0, BIAS_SLOTS - 1)
                    in_window = jnp.where(first, ok_l, ok_r) > 0
                    scores = blk[:, mm * LANES:(mm + 1) * LANES] + bias_ref[0, hh, slot]
                    pieces.append(jnp.where(in_window, scores, NEG))
                pieces.append(blk[:, UNION_KEYS:])
                blocks.append(jnp.concatenate(pieces, axis=1))
        return jnp.concatenate(blocks, axis=0)

    @pl.loop(0, GRID_ROWS // ROWS_PER_STEP)
    def _(t):
        k_ctx = k_ref[0, ctx, :]
        v_ctx = v_ref[0, ctx, :]
        firsts = [t * ROWS_PER_STEP + chain * ROWS_PER_CHAIN for chain in range(ROWS_PER_STEP // ROWS_PER_CHAIN)]
        raw = [raw_scores(r0, k_ctx) for r0 in firsts[:SCORE_LEAD]]
        for i, r0 in enumerate(firsts):
            p = probabilities(biased_scores(r0, raw[i]))
            if i + SCORE_LEAD < len(firsts):
                raw.append(raw_scores(firsts[i + SCORE_LEAD], k_ctx))
            v_all = jnp.concatenate([v_ref[0, chain_keys(r0)[1], :], v_ctx], axis=0)
            finish(p, v_all, ROWS_PER_CHAIN, GRID_W, query_rows(r0))

    if o_ref.shape[1] == T:
        p = probabilities(_dot_nt(stack_heads([q_ref[0, ctx, :]]), k_ref[0, ctx, :]))
        finish(p, v_ref[0, ctx, :], 1, CTX, ctx)


def _nbr_attn(q, k, v, z, bias, update_ctx):
    pair = pl.BlockSpec((1, T, LANES), lambda p, b: (b, 0, p))
    out_rows = T if update_ctx else SEQ
    return pl.pallas_call(
        _nbr_attn_kernel,
        out_shape=jax.ShapeDtypeStruct((BATCH, out_rows, C_WIDTH), BF16),
        grid=(C_HEADS // 2, BATCH),
        in_specs=[pair, pair, pair, pair,
                  pl.BlockSpec((1, 2, BIAS_SLOTS, GRID_W, LANES), lambda p, b: (p, 0, 0, 0, 0))],
        out_specs=pl.BlockSpec((1, out_rows, LANES), lambda p, b: (b, 0, p)),
        compiler_params=_params(("parallel", "parallel")),
        name="nbr_attn",
    )(q, k, v, z, bias)


def _odd_out_body(ins):
    o_ref, w_ref, x_ref, mod_ref = ins
    return x_ref[0] + mod_ref[0, 2:3, :] * _dot(o_ref[0], w_ref[0])


def _odd_out_part(o, w, layer, xu, mod, rows=TILE):
    return _Part(
        arrays=(o, w, xu, mod),
        in_specs=(_row_spec(C_WIDTH, rows), _layer_spec(layer, (C_WIDTH, D)), _row_spec(D, rows),
                  pl.BlockSpec((1, 3, D), _mod_index)),
        out_shapes=(), out_specs=(), body=_odd_out_body, writes_stream=True)


HALF = A_ROPE // 2


def _head_slot(nope, rope):
    n_freq = HALF // 2
    block = lambda axis, half: rope[..., (2 * axis + half) * n_freq:(2 * axis + half + 1) * n_freq]
    half0 = [block(0, 0), block(1, 0)]
    half1 = [block(0, 1), block(1, 1)]
    pad = jnp.zeros(rope.shape[:-1] + (LANES - A_QK - HALF,), rope.dtype)
    return jnp.concatenate([nope] + half0 + half1 + half0 + [pad], axis=-1)


def _rope_tables():
    n = np.arange(SEQ)
    pos = np.stack([n // GRID_W, n % GRID_W], axis=1).astype(np.float32)
    inv = (ROPE_THETA ** (-np.arange(0, HALF, 2, dtype=np.float32) / HALF)).astype(np.float32)
    ang = (pos[:, :, None] * inv).astype(np.float64).reshape(SEQ, HALF)
    cos = np.ones((T, LANES), np.float32)
    sin = np.zeros((T, LANES), np.float32)
    cos[:, A_QK:] = 0.0
    cos[:SEQ, A_NOPE:A_QK] = np.tile(np.cos(ang), (1, 2))
    sin[:SEQ, A_NOPE:A_QK] = np.concatenate([-np.sin(ang), np.sin(ang)], axis=1)
    return jnp.asarray(cos), jnp.asarray(sin)


def _pad_lanes(v, width=LANES):
    return jnp.pad(v, (0, width - v.shape[0])).reshape(1, width)


EVEN_IN_RAW = 3136
EVEN_RUNS = ((0, 1024), (1056, 2592), (2624, 3136))
PREP_ROWS = 256


def _even_w_in_kernel(w_ref, o_ref):
    w = w_ref[0]
    at = 0
    for lo, hi in EVEN_RUNS:
        o_ref[:, at:at + hi - lo] = w[:, lo:hi].astype(BF16)
        at += hi - lo


def _even_w_in(ev_w_in, layer):
    return pl.pallas_call(
        _even_w_in_kernel,
        out_shape=jax.ShapeDtypeStruct((D, EVEN_COLS), BF16),
        grid=(D // PREP_ROWS,),
        in_specs=[pl.BlockSpec((1, PREP_ROWS, EVEN_IN_RAW), lambda r: (layer, r, 0))],
        out_specs=pl.BlockSpec((PREP_ROWS, EVEN_COLS), lambda r: (r, 0)),
        compiler_params=_params(("parallel",)),
        name="even_w_in",
    )(ev_w_in)


def _even_weights(j, p, rope):
    k_r = p["ev_w_in"][j, :, 1024:1056]
    lr = p["ev_w_in"][j, :, 2592:2624]
    zeros = jnp.zeros((D, 32), F32)
    misc = jnp.concatenate([lr, zeros, _head_slot(jnp.zeros((D, 0), F32), k_r)], axis=1)
    w_uq = p["ev_w_uq"][j].reshape(Q_LORA, A_HEADS, A_QK)
    w_uq = _head_slot(w_uq[..., :A_NOPE], w_uq[..., A_NOPE:])
    w_ukv = p["ev_w_ukv"][j].reshape(KV_LORA, A_HEADS, A_NOPE + A_V)
    w_uk = jnp.pad(w_ukv[:, :, :A_NOPE], ((0, 0), (0, 0), (0, LANES - A_NOPE))).reshape(KV_LORA, A_HEADS * LANES)
    w_uv = w_ukv[:, :, A_NOPE:].reshape(KV_LORA, A_WIDTH)
    gate_w = p["ev_gate_w"][j]
    w_gate = jnp.zeros((LANES, 2 * B_KW), F32)
    w_gate = w_gate.at[:GATE_RANK, :B_KW].set(gate_w[0]).at[GATE_RANK:2 * GATE_RANK, B_KW:].set(gate_w[1])
    return {
        "w_in": _even_w_in(p["ev_w_in"], j),
        "w_misc": misc.astype(BF16),
        "q_norm": p["ev_q_norm"][j].reshape(1, Q_LORA),
        "w_uq": w_uq.reshape(Q_LORA, A_HEADS * LANES).astype(BF16),
        "kv_norm": p["ev_kv_norm"][j].reshape(1, KV_LORA),
        "w_ukv": jnp.concatenate([w_uk, w_uv], axis=1).astype(BF16),
        "q_gain": _head_slot(p["ev_q_gain"][j][:A_NOPE], p["ev_q_gain"][j][A_NOPE:]).reshape(1, LANES),
        "k_gain": _head_slot(p["ev_k_gain"][j][:A_NOPE], p["ev_k_gain"][j][A_NOPE:]).reshape(1, LANES),
        "w_gate": w_gate.astype(BF16),
        "gate_b": p["ev_gate_b"][j].reshape(1, 2 * B_KW),
        "cos": rope[0], "sin": rope[1],
        "gla_norm": p["ev_gla_norm"][j].reshape(1, B_WIDTH),
    }


def _nbr_bias_table(rpb):
    col = np.arange(GRID_W)
    col_start = np.clip(col - WIN_COLS // 2, 0, GRID_W - WIN_COLS)
    col_mask = (col[None, :] >= col_start[:, None]) & (col[None, :] < col_start[:, None] + WIN_COLS)
    col_idx = np.clip(col[None, :] - col[:, None] + WIN_COLS - 1, 0, 2 * WIN_COLS - 2)
    n_rel = 2 * WIN_COLS - 1
    n_rows = 2 * WIN_ROWS - 1
    pick = ((col_idx[None] == np.arange(n_rel)[:, None, None]) & col_mask[None]) * LOG2E
    select = np.zeros((2 * n_rel + 3, GRID_W, 2, GRID_W), np.float32)
    for half in range(2):
        select[half * n_rel:(half + 1) * n_rel, :, half, :] = pick
        select[2 * n_rel, :, half, :] = np.where(col_mask, 0.0, NEG)
        select[2 * n_rel + 1 + half, :, half, :] = NEG
    slots = np.arange(BIAS_SLOTS)
    outside = np.stack([slots - 1 < 0, slots >= n_rows], axis=1).astype(np.float32)
    ext = jnp.pad(rpb, ((0, 0), (1, 1), (0, 0)))
    flags = jnp.broadcast_to(jnp.asarray(np.concatenate([np.ones((BIAS_SLOTS, 1), np.float32), outside], axis=1)),
                             (C_HEADS, BIAS_SLOTS, 3))
    lhs = jnp.concatenate([ext[:, :-1], ext[:, 1:], flags], axis=-1).reshape(C_HEADS * BIAS_SLOTS, 2 * n_rel + 3)
    table = jnp.dot(lhs, jnp.asarray(select.reshape(2 * n_rel + 3, -1)), precision=lax.Precision.HIGHEST)
    return table.reshape(C_HEADS // 2, 2, BIAS_SLOTS, GRID_W, LANES)


def kernel(x, c, ctx, c_ctx, norm_g, ada_w, ada_b, ev_w_in, ev_q_norm, ev_w_uq, ev_kv_norm, ev_w_ukv, ev_q_gain,
           ev_k_gain, ev_gate_w, ev_gate_b, ev_gla_norm, ev_w_out, od_w_in, od_q_gain, od_k_gain, od_rpb, od_w_out):
    p = dict(ev_w_in=ev_w_in, ev_q_norm=ev_q_norm, ev_w_uq=ev_w_uq, ev_kv_norm=ev_kv_norm, ev_w_ukv=ev_w_ukv,
             ev_q_gain=ev_q_gain, ev_k_gain=ev_k_gain, ev_gate_w=ev_gate_w, ev_gate_b=ev_gate_b,
             ev_gla_norm=ev_gla_norm, ev_w_out=ev_w_out)
    c16 = jnp.zeros((MOD_ROWS, D), F32).at[:BATCH].set(c).at[BATCH].set(c_ctx)
    mod_all = _modulation(c16, ada_w, ada_b).reshape(DEPTH, MOD_ROWS, 3, D)
    rope = _rope_tables()
    od_w_in_b, od_w_out_b, ev_w_out_b = od_w_in.astype(BF16), od_w_out.astype(BF16), ev_w_out.astype(BF16)
    xu = None
    finish, finish_name = _source_part(x, ctx), "inputs"
    for i in range(DEPTH):
        j = i // 2
        mod = mod_all[i]
        ng = norm_g[i].reshape(1, D)
        if i % 2 == 0:
            w = _even_weights(j, p, rope)
            start, start_name = _even_in_part(mod, ng, w), "even_in"
        else:
            start = _odd_in_part(mod, ng, od_w_in_b, j, _pad_lanes(jnp.tile(od_q_gain[j], 2)),
                                 _pad_lanes(jnp.tile(od_k_gain[j], 2)))
            start_name = "odd_in"
        outs = _boundary_call(f"{finish_name}_{start_name}", finish, start)
        if finish.writes_stream:
            xu, outs = outs[0], outs[1:]
        if i % 2 == 0:
            q, k, v, za, gq, gk, gv, gf, gb, zb = outs
            a = _mla_attn(q, k, v, za)
            gla = _gla(gq, gk, gv, gf, gb)
            stream = (x, ctx, 0) if xu is None else (xu, xu, CTX_TILE)
            finish = _even_out_part(a, gla, zb, w["gla_norm"], ev_w_out_b, j, stream, mod)
            finish_name = "even_out"
        else:
            q, k, v, z = outs
            o = _nbr_attn(q, k, v, z, _nbr_bias_table(od_rpb[j]), update_ctx=i < DEPTH - 1)
            rows = TILE if i < DEPTH - 1 else LAST_ROWS
            finish, finish_name = _odd_out_part(o, od_w_out_b, j, xu, mod, rows), "odd_out"
    return _boundary_call(finish_name, finish, None, n_tiles=SEQ // LAST_ROWS, rows=LAST_ROWS)[0]
```

```python
import functools
from typing import Callable, NamedTuple

import jax
import jax.numpy as jnp
import numpy as np
from jax import lax
from jax.experimental import pallas as pl
from jax.experimental.pallas import tpu as pltpu

D = 1024
BATCH = 8
SEQ = 2048
DEPTH = 4
GRID_W = 64
GRID_ROWS = SEQ // GRID_W
CTX = 256
T = SEQ + CTX
EPS = 1e-6
ROPE_THETA = 10000.0

A_HEADS = 8
A_NOPE = 64
A_ROPE = 32
A_V = 64
A_QK = A_NOPE + A_ROPE
KV_LORA = 256
Q_LORA = 768
A_WIDTH = A_HEADS * A_V

B_HEADS = 4
B_DK = 64
B_DV = 128
B_KW = B_HEADS * B_DK
B_WIDTH = B_HEADS * B_DV
GATE_RANK = 16
GATE_TAU = 16.0
CHUNK = 64

C_HEADS = 16
C_HD = 64
C_WIDTH = C_HEADS * C_HD
WIN_ROWS = 8
WIN_COLS = 16

LANES = 128
TILE = 256
N_TILES = T // TILE
CTX_TILE = N_TILES - 1
LAST_ROWS = 512
MOD_ROWS = 16
EVEN_COLS = 3072
EVEN_EARLY = Q_LORA + KV_LORA
NEG = -1e30
LOG2E = 1.4426950408889634
VMEM_LIMIT = 56 * 1024 * 1024

F32 = jnp.float32
BF16 = jnp.bfloat16


def _dot(a, b):
    return lax.dot_general(a, b, (((1,), (0,)), ((), ())), preferred_element_type=F32)


def _dot_nt(a, b):
    return lax.dot_general(a, b, (((1,), (1,)), ((), ())), preferred_element_type=F32)


def _dot_tn(a, b):
    return lax.dot_general(a, b, (((0,), (0,)), ((), ())), preferred_element_type=F32)


def _silu(x):
    return x / (1.0 + jnp.exp(-x))


def _rms(x, n):
    return x * lax.rsqrt(jnp.sum(x * x, axis=-1, keepdims=True) * (1.0 / n) + EPS)


def _params(sem):
    return pltpu.CompilerParams(dimension_semantics=sem, vmem_limit_bytes=VMEM_LIMIT)


def _const_spec(shape):
    nd = len(shape)
    return pl.BlockSpec(shape, lambda *_: (0,) * nd)


def _mod_index(b, t):
    return (jnp.where(t == CTX_TILE, BATCH, b), 0, 0)


def _mod_kernel(c_ref, w_ref, b_ref, o_ref):
    a = _silu(c_ref[...]).astype(BF16)
    o_ref[0] = _dot(a, w_ref[0].astype(BF16)) + b_ref[0]


def _modulation(c16, ada_w, ada_b):
    return pl.pallas_call(
        _mod_kernel,
        out_shape=jax.ShapeDtypeStruct((DEPTH, MOD_ROWS, 3 * D), F32),
        grid=(DEPTH, 3),
        in_specs=[
            pl.BlockSpec((MOD_ROWS, D), lambda i, j: (0, 0)),
            pl.BlockSpec((1, D, D), lambda i, j: (i, 0, j)),
            pl.BlockSpec((1, 1, D), lambda i, j: (i, 0, j)),
        ],
        out_specs=pl.BlockSpec((1, MOD_ROWS, D), lambda i, j: (i, 0, j)),
        compiler_params=_params(("parallel", "parallel")),
        name="modulation",
    )(c16, ada_w, ada_b.reshape(DEPTH, 1, 3 * D))


def _modulated_norm(x, mod_ref, ng_ref):
    return _rms(x, D) * ng_ref[...] * (1.0 + mod_ref[0, 1:2, :]) + mod_ref[0, 0:1, :]


def _group_sumsq(u, group, counted):
    sq = (u * u).astype(BF16)
    width = 2 * LANES
    row = lax.broadcasted_iota(jnp.int32, (width, width), 0)
    col = lax.broadcasted_iota(jnp.int32, (width, width), 1)
    ones = ((row // group == col // group) & (row % group < counted)).astype(BF16)
    cols = [slice(c, c + width) for c in range(0, u.shape[1], width)]
    return jnp.concatenate([_dot(sq[:, c], ones) for c in cols], axis=1)


def _slot_sumsq(u):
    return _group_sumsq(u, LANES, A_QK)


def _stream_specs(ctx_block):
    return [pl.BlockSpec((1, TILE, D), lambda b, t: (b, jnp.minimum(t, CTX_TILE - 1), 0)),
            pl.BlockSpec((1, TILE, D), lambda b, t: (b, ctx_block, 0))]


def _stream_tile(lat_ref, ctx_ref):
    return jnp.where(pl.program_id(1) == CTX_TILE, ctx_ref[0], lat_ref[0])


class _Part(NamedTuple):
    arrays: tuple
    in_specs: tuple
    out_shapes: tuple
    out_specs: tuple
    body: Callable
    writes_stream: bool = False


def _row_spec(width, rows=TILE):
    return pl.BlockSpec((1, rows, width), lambda b, t: (b, t, 0))


def _boundary_call(name, finish, start, n_tiles=N_TILES, rows=TILE):
    parts = [p for p in (finish, start) if p is not None]
    stream_shape = (jax.ShapeDtypeStruct((BATCH, n_tiles * rows, D), F32),) if finish.writes_stream else ()
    stream_spec = (_row_spec(D, rows),) if finish.writes_stream else ()
    n_finish_in = len(finish.arrays)
    n_in = sum(len(p.arrays) for p in parts)

    def kernel(*refs):
        ins, outs = refs[:n_in], refs[n_in:]
        x = finish.body(ins[:n_finish_in])
        if finish.writes_stream:
            outs[0][0] = x
        if start is not None:
            start.body(x, ins[n_finish_in:], outs[len(stream_shape):])

    out_shapes = stream_shape + (start.out_shapes if start is not None else ())
    out_specs = stream_spec + (start.out_specs if start is not None else ())
    return pl.pallas_call(
        kernel,
        out_shape=out_shapes,
        grid=(BATCH, n_tiles),
        in_specs=[s for p in parts for s in p.in_specs],
        out_specs=out_specs,
        compiler_params=_params(("parallel", "parallel")),
        name=name,
    )(*[a for p in parts for a in p.arrays])


def _source_part(x, ctx):
    return _Part((x, ctx), tuple(_stream_specs(0)), (), (), lambda ins: _stream_tile(*ins))


def _even_in_body(x, ins, outs):
    (mod_ref, ng_ref, win_ref, wmisc_ref, qn_ref, wuq_ref, kvn_ref, wukv_ref, qg_ref, kg_ref, wg_ref, gbias_ref,
     cos_ref, sin_ref) = ins
    q_ref, k_ref, v_ref, za_ref, gq_ref, gk_ref, gv_ref, gf_ref, gb_ref, zb_ref = outs
    h = _modulated_norm(x, mod_ref, ng_ref).astype(BF16)
    lat = _dot(h, win_ref[:, :EVEN_EARLY])
    q_lat = lat[:, 0:Q_LORA]
    kv_lat = lat[:, Q_LORA:]
    misc = _dot(h, wmisc_ref[...])

    cos = cos_ref[...]
    sin = sin_ref[...]

    def rope(u):
        return u * cos + pltpu.roll(u, LANES - HALF, 1) * sin

    qn = (_rms(q_lat, Q_LORA) * qn_ref[...]).astype(BF16)
    kn = (_rms(kv_lat, KV_LORA) * kvn_ref[...]).astype(BF16)

    logits = _dot(misc.astype(BF16), wg_ref[...]) + gbias_ref[...]
    g = (jnp.minimum(logits, 0.0) - jnp.log1p(jnp.exp(-jnp.abs(logits)))) * (1.0 / GATE_TAU)
    gf_ref[0] = g[:, :B_KW]
    gb_ref[0] = g[:, B_KW:]

    lane = lax.broadcasted_iota(jnp.int32, (1, LANES), 1)
    rotary = (lane >= A_NOPE) & (lane < A_QK)
    kr_roped = rope(jnp.where(lane >= A_NOPE, misc, 0.0) * kg_ref[...])
    k_rope = jnp.where(rotary, misc, 0.0)
    kr_ss = jnp.sum(k_rope * k_rope, axis=-1, keepdims=True)

    late = _dot(h, win_ref[:, EVEN_EARLY:])
    za_ref[0] = late[:, 0:512]
    gq_ref[0] = late[:, 512:768] * (B_DK ** -0.5)
    gk_ref[0] = late[:, 768:1024]
    gv_ref[0] = late[:, 1024:1536].astype(BF16)
    zb_ref[0] = late[:, 1536:2048]

    for pair in range(A_HEADS // 2):
        cols = slice(pair * 2 * LANES, (pair + 1) * 2 * LANES)
        qf = _dot(qn, wuq_ref[:, cols])
        kf = _dot(kn, wukv_ref[:, cols])
        q_ss = _slot_sumsq(qf)
        k_ss = _slot_sumsq(kf)
        for hh in range(2):
            sl = slice(hh * LANES, (hh + 1) * LANES)
            q_scale = lax.rsqrt(q_ss[:, sl] * (1.0 / A_QK) + EPS) * (A_QK ** -0.5 * LOG2E)
            q_ref[0, 2 * pair + hh] = (rope(qf[:, sl] * qg_ref[...]) * q_scale).astype(BF16)
            k_scale = lax.rsqrt((k_ss[:, sl] + kr_ss) * (1.0 / A_QK) + EPS)
            k_ref[0, 2 * pair + hh] = ((kf[:, sl] * kg_ref[...] + kr_roped) * k_scale).astype(BF16)
    v_ref[0] = _dot(kn, wukv_ref[:, A_HEADS * LANES:]).astype(BF16)


def _even_in_part(mod, ng, w):
    tok = _row_spec
    head = pl.BlockSpec((1, A_HEADS, TILE, LANES), lambda b, t: (b, 0, t, 0))
    rope_spec = pl.BlockSpec((TILE, LANES), lambda b, t: (t, 0))
    sd = jax.ShapeDtypeStruct
    return _Part(
        arrays=(mod, ng, w["w_in"], w["w_misc"], w["q_norm"], w["w_uq"], w["kv_norm"], w["w_ukv"], w["q_gain"],
                w["k_gain"], w["w_gate"], w["gate_b"], w["cos"], w["sin"]),
        in_specs=(
            pl.BlockSpec((1, 3, D), _mod_index),
            _const_spec((1, D)),
            _const_spec((D, EVEN_COLS)),
            _const_spec((D, LANES)),
            _const_spec((1, Q_LORA)),
            _const_spec((Q_LORA, A_HEADS * LANES)),
            _const_spec((1, KV_LORA)),
            _const_spec((KV_LORA, A_HEADS * LANES + A_WIDTH)),
            _const_spec((1, LANES)),
            _const_spec((1, LANES)),
            _const_spec((LANES, 2 * B_KW)),
            _const_spec((1, 2 * B_KW)),
            rope_spec, rope_spec,
        ),
        out_shapes=(
            sd((BATCH, A_HEADS, T, LANES), BF16),
            sd((BATCH, A_HEADS, T, LANES), BF16),
            sd((BATCH, T, A_WIDTH), BF16),
            sd((BATCH, T, A_WIDTH), F32),
            sd((BATCH, T, B_KW), F32),
            sd((BATCH, T, B_KW), F32),
            sd((BATCH, T, B_WIDTH), BF16),
            sd((BATCH, T, B_KW), F32),
            sd((BATCH, T, B_KW), F32),
            sd((BATCH, T, B_WIDTH), F32),
        ),
        out_specs=(head, head, tok(A_WIDTH), tok(A_WIDTH), tok(B_KW), tok(B_KW), tok(B_WIDTH),
                   tok(B_KW), tok(B_KW), tok(B_WIDTH)),
        body=_even_in_body,
    )


def _values_with_ones(v):
    return jnp.concatenate([v, jnp.ones(v.shape, BF16)], axis=1)


def _normalise(acc):
    return acc[:, :LANES] / acc[:, LANES:]


MLA_TILES_PER_STEP = 8


def _mla_attn_kernel(q_ref, k_ref, v_ref, z_ref, o_ref):
    first_head = lax.broadcasted_iota(jnp.int32, (1, LANES), 1) < A_V

    def attend(row_tiles, key_groups):
        rowmax = lambda s: functools.reduce(jnp.maximum, [jnp.max(x, axis=-1, keepdims=True) for x in s])
        chains = [(rows, hh) for rows in row_tiles for hh in range(2)]
        n, groups = len(chains), range(len(key_groups))
        scores, top, probs, acc = {}, {}, {}, {}
        for step in range(n + 2):
            if step >= 1 and step - 1 < n:
                top[step - 1] = rowmax(scores[step - 1])
            for i in groups:
                g = key_groups[i]
                if step < n:
                    rows, hh = chains[step]
                    scores.setdefault(step, []).append(_dot_nt(q_ref[0, hh, rows, :], k_ref[0, hh, g, :]))
                if 0 <= step - 1 < n:
                    probs.setdefault(step - 1, []).append(
                        jnp.exp2(scores[step - 1][i] - top[step - 1]).astype(BF16))
                if 0 <= step - 2 < n:
                    acc.setdefault(step - 2, []).append(_dot(probs[step - 2][i], _values_with_ones(v_ref[0, g, :])))
            done = step - 2
            if done >= 0 and done % 2 == 1:
                rows = chains[done][0]
                o0 = _normalise(functools.reduce(jnp.add, acc[done - 1]))
                o1 = _normalise(functools.reduce(jnp.add, acc[done]))
                o_ref[0, rows, :] = (jnp.where(first_head, o0, o1) * _silu(z_ref[0, rows, :])).astype(BF16)

    @pl.loop(0, CTX_TILE // MLA_TILES_PER_STEP)
    def _(t):
        base = t * (MLA_TILES_PER_STEP * TILE)
        tiles = [pl.ds(pl.multiple_of(base + i * TILE, TILE), TILE) for i in range(MLA_TILES_PER_STEP)]
        attend(tiles, [slice(0, 768), slice(768, 1536), slice(1536, T)])

    attend([slice(SEQ, T)], [slice(SEQ, T)])


def _mla_attn(q, k, v, za):
    pair = pl.BlockSpec((1, T, LANES), lambda b, p: (b, 0, p))
    heads = pl.BlockSpec((1, 2, T, LANES), lambda b, p: (b, p, 0, 0))
    return pl.pallas_call(
        _mla_attn_kernel,
        out_shape=jax.ShapeDtypeStruct((BATCH, T, A_WIDTH), BF16),
        grid=(BATCH, A_HEADS // 2),
        in_specs=[heads, heads, pair, pair],
        out_specs=pair,
        compiler_params=_params(("parallel", "parallel")),
        name="mla_attn",
    )(q, k, v, za)


def _gla_direction(q_ref, k_ref, v_ref, g_ref, o_ref, s_ref, tile, forward):
    rows = pl.ds(pl.multiple_of(tile * TILE, TILE), TILE)
    q = q_ref[0, rows, :]
    k = k_ref[0, rows, :]
    g = g_ref[0, rows, :]
    v = v_ref[0, rows, :]
    row = lax.broadcasted_iota(jnp.int32, (TILE, TILE), 0)
    col = lax.broadcasted_iota(jnp.int32, (TILE, TILE), 1)
    in_chunk_causal = ((row // CHUNK) == (col // CHUNK)) & ((col <= row) if forward else (col >= row))
    tri = in_chunk_causal.astype(BF16)
    g_hi = g.astype(BF16)
    g_lo = (g - g_hi.astype(F32)).astype(BF16)
    b = _dot(tri, g_hi) + _dot(tri, g_lo)

    n_chunks = TILE // CHUNK
    chunks = [slice(c * CHUNK, (c + 1) * CHUNK) for c in range(n_chunks)]
    last = CHUNK - 1 if forward else 0
    b_last = [b[c * CHUNK + last:c * CHUNK + last + 1] for c in range(n_chunks)]
    b_last_rows = jnp.concatenate([jnp.broadcast_to(bl, (CHUNK, LANES)) for bl in b_last], axis=0)
    q_t = q * jnp.exp(b)
    k_t = (k * jnp.exp(-b)).astype(BF16)
    k_dec = (k * jnp.exp(b_last_rows - b)).astype(BF16)
    yield

    lane = lax.broadcasted_iota(jnp.int32, (1, LANES), 1)
    intra = []
    for hh, head_lanes in enumerate((lane < B_DK, lane >= B_DK)):
        qm = jnp.where(head_lanes, q_t, 0.0).astype(BF16)
        att = jnp.where(in_chunk_causal, _dot_nt(qm, k_t), 0.0).astype(BF16)
        intra.append(_dot(att, v[:, hh * B_DV:(hh + 1) * B_DV]))
    yield

    srow = lax.broadcasted_iota(jnp.int32, (2 * B_DK, 2 * B_DV), 0)
    scol = lax.broadcasted_iota(jnp.int32, (2 * B_DK, 2 * B_DV), 1)
    own_head = (srow // B_DK) == (scol // B_DV)
    updates = [jnp.where(own_head, _dot_tn(k_dec[sl], v[sl]), 0.0) for sl in chunks]
    yield
    state = s_ref[...]
    starts = [None] * n_chunks
    for c in (range(n_chunks) if forward else reversed(range(n_chunks))):
        starts[c] = state.astype(BF16)
        decay = jnp.broadcast_to(jnp.exp(b_last[c]), (LANES, LANES)).T
        state = state * jnp.concatenate([decay, decay], axis=1) + updates[c]
    s_ref[...] = state
    yield
    q_tb = q_t.astype(BF16)
    inter = [_dot(q_tb[sl], starts[c]) for c, sl in enumerate(chunks)]
    o_ref[0, rows, :] += jnp.concatenate(intra, axis=1) + jnp.concatenate(inter, axis=0)


GLA_STEPS_PER_ITER = 9


def _gla_kernel(q_ref, k_ref, v_ref, gf_ref, gb_ref, o_ref, sf_ref, sb_ref):
    sf_ref[...] = jnp.zeros_like(sf_ref)
    sb_ref[...] = jnp.zeros_like(sb_ref)
    o_ref[...] = jnp.zeros_like(o_ref)

    @pl.loop(0, N_TILES // GLA_STEPS_PER_ITER)
    def _(it):
        stages = []
        for sub in range(GLA_STEPS_PER_ITER):
            step = it * GLA_STEPS_PER_ITER + sub
            fwd_tile = jnp.where(step == 0, CTX_TILE, step - 1)
            bwd_tile = CTX_TILE - step
            stages.append(_gla_direction(q_ref, k_ref, v_ref, gf_ref, o_ref, sf_ref, fwd_tile, True))
            stages.append(_gla_direction(q_ref, k_ref, v_ref, gb_ref, o_ref, sb_ref, bwd_tile, False))
        while stages:
            stages = [s for s in stages if next(s, StopIteration) is not StopIteration]


def _gla(gq, gk, gv, gf, gb):
    kspec = pl.BlockSpec((1, T, LANES), lambda b, p: (b, 0, p))
    vspec = pl.BlockSpec((1, T, 2 * B_DV), lambda b, p: (b, 0, p))
    out = jax.ShapeDtypeStruct((BATCH, T, B_WIDTH), F32)
    return pl.pallas_call(
        _gla_kernel,
        out_shape=out,
        grid=(BATCH, B_HEADS // 2),
        in_specs=[kspec, kspec, vspec, kspec, kspec],
        out_specs=vspec,
        scratch_shapes=[pltpu.VMEM((2 * B_DK, 2 * B_DV), F32), pltpu.VMEM((2 * B_DK, 2 * B_DV), F32)],
        compiler_params=_params(("parallel", "parallel")),
        name="gla_scan",
    )(gq, gk, gv, gf, gb)


def _even_out_body(ins):
    a_ref, gla_ref, zb_ref, gn_ref, wa_ref, wb_ref, xl_ref, xc_ref, mod_ref = ins
    bsum = gla_ref[0]
    zb = zb_ref[0]
    parts = []
    for hh in range(B_HEADS):
        sl = slice(hh * B_DV, (hh + 1) * B_DV)
        parts.append((_rms(bsum[:, sl], B_DV) * gn_ref[:, sl] * _silu(zb[:, sl])).astype(BF16))
    y = _dot(a_ref[0], wa_ref[0]) + _dot(jnp.concatenate(parts, axis=1), wb_ref[0])
    return _stream_tile(xl_ref, xc_ref) + mod_ref[0, 2:3, :] * y


def _layer_spec(layer, shape, block=0):
    return pl.BlockSpec((1,) + shape, lambda b, t: (layer, block, 0))


def _even_out_part(a, gla, zb, gn, w_out, layer, stream, mod):
    x_lat, x_ctx, ctx_block = stream
    tok = _row_spec
    return _Part(
        arrays=(a, gla, zb, gn, w_out, w_out, x_lat, x_ctx, mod),
        in_specs=(tok(A_WIDTH), tok(B_WIDTH), tok(B_WIDTH), _const_spec((1, B_WIDTH)),
                  _layer_spec(layer, (A_WIDTH, D), 0), _layer_spec(layer, (B_WIDTH, D), 1),
                  *_stream_specs(ctx_block),
                  pl.BlockSpec((1, 3, D), _mod_index)),
        out_shapes=(), out_specs=(), body=_even_out_body, writes_stream=True)


def _odd_in_body(x, ins, outs):
    mod_ref, ng_ref, win_ref, qg_ref, kg_ref = ins
    q_ref, k_ref, v_ref, z_ref = outs
    h = _modulated_norm(x, mod_ref, ng_ref).astype(BF16)
    project = lambda part: _dot(h, win_ref[0, :, part * C_WIDTH:(part + 1) * C_WIDTH])

    def normalised(u, gain_ref, factor):
        scale = lax.rsqrt(_group_sumsq(u, C_HD, C_HD) * (1.0 / C_HD) + EPS)
        return (u * scale * jnp.tile(gain_ref[...], (1, C_WIDTH // LANES)) * factor).astype(BF16)

    uq = project(0)
    uk = project(1)
    q_ref[0] = normalised(uq, qg_ref, C_HD ** -0.5 * LOG2E)
    v_ref[0] = project(2).astype(BF16)
    k_ref[0] = normalised(uk, kg_ref, 1.0)
    z_ref[0] = project(3)


def _odd_in_part(mod, ng, win, layer, qg, kg):
    tok = _row_spec
    sd = jax.ShapeDtypeStruct
    return _Part(
        arrays=(mod, ng, win, qg, kg),
        in_specs=(pl.BlockSpec((1, 3, D), _mod_index), _const_spec((1, D)), _layer_spec(layer, (D, 4 * C_WIDTH)),
                  _const_spec((1, LANES)), _const_spec((1, LANES))),
        out_shapes=(sd((BATCH, T, C_WIDTH), BF16), sd((BATCH, T, C_WIDTH), BF16), sd((BATCH, T, C_WIDTH), BF16),
                    sd((BATCH, T, C_WIDTH), F32)),
        out_specs=(tok(C_WIDTH), tok(C_WIDTH), tok(C_WIDTH), tok(C_WIDTH)),
        body=_odd_in_body)


ROWS_PER_CHAIN = 2
ROWS_PER_STEP = 32
SCORE_LEAD = 2
UNION_ROWS = WIN_ROWS + ROWS_PER_CHAIN
UNION_KEYS = UNION_ROWS * GRID_W
BIAS_SLOTS = 2 * WIN_ROWS


def _nbr_attn_kernel(q_ref, k_ref, v_ref, z_ref, bias_ref, o_ref):
    lane = lax.broadcasted_iota(jnp.int32, (1, LANES), 1)
    first = lane < C_HD
    head_mask = (first, lane >= C_HD)
    ctx = slice(SEQ, T)

    def stack_heads(blocks):
        zero = jnp.zeros(blocks[0].shape, BF16)
        return jnp.concatenate([jnp.where(head_mask[hh], blk, zero) for blk in blocks for hh in range(2)], axis=0)

    def probabilities(s):
        m = jnp.max(s, axis=-1, keepdims=True)
        return jnp.exp2(s - m).astype(BF16)

    def finish(p, v_all, n_blocks, rows, out_rows):
        o = _normalise(_dot(p, _values_with_ones(v_all)))
        o = jnp.concatenate(
            [jnp.where(first, o[(2 * i) * rows:(2 * i + 1) * rows], o[(2 * i + 1) * rows:(2 * i + 2) * rows])
             for i in range(n_blocks)], axis=0)
        o_ref[0, out_rows, :] = (o * _silu(z_ref[0, out_rows, :])).astype(BF16)

    def chain_keys(r0):
        u0 = jnp.clip(r0 - WIN_ROWS // 2, 0, GRID_ROWS - UNION_ROWS)
        return u0, pl.ds(pl.multiple_of(u0 * GRID_W, 2 * GRID_W), UNION_KEYS)

    def query_rows(r0):
        return pl.ds(pl.multiple_of(r0 * GRID_W, ROWS_PER_CHAIN * GRID_W), ROWS_PER_CHAIN * GRID_W)

    def raw_scores(r0, k_ctx):
        q_pair = q_ref[0, query_rows(r0), :]
        q_rows = [q_pair[rr * GRID_W:(rr + 1) * GRID_W] for rr in range(ROWS_PER_CHAIN)]
        k_all = jnp.concatenate([k_ref[0, chain_keys(r0)[1], :], k_ctx], axis=0)
        return _dot_nt(stack_heads(q_rows), k_all)

    def biased_scores(r0, s):
        u0, _ = chain_keys(r0)
        blocks = []
        for rr in range(ROWS_PER_CHAIN):
            r = r0 + rr
            r_start = jnp.clip(r - WIN_ROWS // 2, 0, GRID_ROWS - WIN_ROWS)
            for hh in range(2):
                blk = s[(2 * rr + hh) * GRID_W:(2 * rr + hh + 1) * GRID_W]
                pieces = []
                for mm in range(UNION_ROWS // 2):
                    u = u0 + 2 * mm
                    ok_l = ((u >= r_start) & (u < r_start + WIN_ROWS)).astype(jnp.int32)
                    ok_r = ((u + 1 >= r_start) & (u + 1 < r_start + WIN_ROWS)).astype(jnp.int32)
                    slot = jnp.clip(u - r + WIN_ROWS, 0, BIAS_SLOTS - 1)
                    in_window = jnp.where(first, ok_l, ok_r) > 0
                    scores = blk[:, mm * LANES:(mm + 1) * LANES] + bias_ref[0, hh, slot]
                    pieces.append(jnp.where(in_window, scores, NEG))
                pieces.append(blk[:, UNION_KEYS:])
                blocks.append(jnp.concatenate(pieces, axis=1))
        return jnp.concatenate(blocks, axis=0)

    @pl.loop(0, GRID_ROWS // ROWS_PER_STEP)
    def _(t):
        k_ctx = k_ref[0, ctx, :]
        v_ctx = v_ref[0, ctx, :]
        firsts = [t * ROWS_PER_STEP + chain * ROWS_PER_CHAIN for chain in range(ROWS_PER_STEP // ROWS_PER_CHAIN)]
        raw = [raw_scores(r0, k_ctx) for r0 in firsts[:SCORE_LEAD]]
        for i, r0 in enumerate(firsts):
            p = probabilities(biased_scores(r0, raw[i]))
            if i + SCORE_LEAD < len(firsts):
                raw.append(raw_scores(firsts[i + SCORE_LEAD], k_ctx))
            v_all = jnp.concatenate([v_ref[0, chain_keys(r0)[1], :], v_ctx], axis=0)
            finish(p, v_all, ROWS_PER_CHAIN, GRID_W, query_rows(r0))

    if o_ref.shape[1] == T:
        p = probabilities(_dot_nt(stack_heads([q_ref[0, ctx, :]]), k_ref[0, ctx, :]))
        finish(p, v_ref[0, ctx, :], 1, CTX, ctx)


def _nbr_attn(q, k, v, z, bias, update_ctx):
    pair = pl.BlockSpec((1, T, LANES), lambda p, b: (b, 0, p))
    out_rows = T if update_ctx else SEQ
    return pl.pallas_call(
        _nbr_attn_kernel,
        out_shape=jax.ShapeDtypeStruct((BATCH, out_rows, C_WIDTH), BF16),
        grid=(C_HEADS // 2, BATCH),
        in_specs=[pair, pair, pair, pair,
                  pl.BlockSpec((1, 2, BIAS_SLOTS, GRID_W, LANES), lambda p, b: (p, 0, 0, 0, 0))],
        out_specs=pl.BlockSpec((1, out_rows, LANES), lambda p, b: (b, 0, p)),
        compiler_params=_params(("parallel", "parallel")),
        name="nbr_attn",
    )(q, k, v, z, bias)


def _odd_out_body(ins):
    o_ref, w_ref, x_ref, mod_ref = ins
    return x_ref[0] + mod_ref[0, 2:3, :] * _dot(o_ref[0], w_ref[0])


def _odd_out_part(o, w, layer, xu, mod, rows=TILE):
    return _Part(
        arrays=(o, w, xu, mod),
        in_specs=(_row_spec(C_WIDTH, rows), _layer_spec(layer, (C_WIDTH, D)), _row_spec(D, rows),
                  pl.BlockSpec((1, 3, D), _mod_index)),
        out_shapes=(), out_specs=(), body=_odd_out_body, writes_stream=True)


HALF = A_ROPE // 2


def _head_slot(nope, rope):
    n_freq = HALF // 2
    block = lambda axis, half: rope[..., (2 * axis + half) * n_freq:(2 * axis + half + 1) * n_freq]
    half0 = [block(0, 0), block(1, 0)]
    half1 = [block(0, 1), block(1, 1)]
    pad = jnp.zeros(rope.shape[:-1] + (LANES - A_QK - HALF,), rope.dtype)
    return jnp.concatenate([nope] + half0 + half1 + half0 + [pad], axis=-1)


def _rope_tables():
    n = np.arange(SEQ)
    pos = np.stack([n // GRID_W, n % GRID_W], axis=1).astype(np.float32)
    inv = (ROPE_THETA ** (-np.arange(0, HALF, 2, dtype=np.float32) / HALF)).astype(np.float32)
    ang = (pos[:, :, None] * inv).astype(np.float64).reshape(SEQ, HALF)
    cos = np.ones((T, LANES), np.float32)
    sin = np.zeros((T, LANES), np.float32)
    cos[:, A_QK:] = 0.0
    cos[:SEQ, A_NOPE:A_QK] = np.tile(np.cos(ang), (1, 2))
    sin[:SEQ, A_NOPE:A_QK] = np.concatenate([-np.sin(ang), np.sin(ang)], axis=1)
    return jnp.asarray(cos), jnp.asarray(sin)


def _pad_lanes(v, width=LANES):
    return jnp.pad(v, (0, width - v.shape[0])).reshape(1, width)


EVEN_IN_RAW = 3136
EVEN_RUNS = ((0, 1024), (1056, 2592), (2624, 3136))
PREP_ROWS = 256


def _even_w_in_kernel(w_ref, o_ref):
    w = w_ref[0]
    at = 0
    for lo, hi in EVEN_RUNS:
        o_ref[:, at:at + hi - lo] = w[:, lo:hi].astype(BF16)
        at += hi - lo


def _even_w_in(ev_w_in, layer):
    return pl.pallas_call(
        _even_w_in_kernel,
        out_shape=jax.ShapeDtypeStruct((D, EVEN_COLS), BF16),
        grid=(D // PREP_ROWS,),
        in_specs=[pl.BlockSpec((1, PREP_ROWS, EVEN_IN_RAW), lambda r: (layer, r, 0))],
        out_specs=pl.BlockSpec((PREP_ROWS, EVEN_COLS), lambda r: (r, 0)),
        compiler_params=_params(("parallel",)),
        name="even_w_in",
    )(ev_w_in)


def _even_weights(j, p, rope):
    k_r = p["ev_w_in"][j, :, 1024:1056]
    lr = p["ev_w_in"][j, :, 2592:2624]
    zeros = jnp.zeros((D, 32), F32)
    misc = jnp.concatenate([lr, zeros, _head_slot(jnp.zeros((D, 0), F32), k_r)], axis=1)
    w_uq = p["ev_w_uq"][j].reshape(Q_LORA, A_HEADS, A_QK)
    w_uq = _head_slot(w_uq[..., :A_NOPE], w_uq[..., A_NOPE:])
    w_ukv = p["ev_w_ukv"][j].reshape(KV_LORA, A_HEADS, A_NOPE + A_V)
    w_uk = jnp.pad(w_ukv[:, :, :A_NOPE], ((0, 0), (0, 0), (0, LANES - A_NOPE))).reshape(KV_LORA, A_HEADS * LANES)
    w_uv = w_ukv[:, :, A_NOPE:].reshape(KV_LORA, A_WIDTH)
    gate_w = p["ev_gate_w"][j]
    w_gate = jnp.zeros((LANES, 2 * B_KW), F32)
    w_gate = w_gate.at[:GATE_RANK, :B_KW].set(gate_w[0]).at[GATE_RANK:2 * GATE_RANK, B_KW:].set(gate_w[1])
    return {
        "w_in": _even_w_in(p["ev_w_in"], j),
        "w_misc": misc.astype(BF16),
        "q_norm": p["ev_q_norm"][j].reshape(1, Q_LORA),
        "w_uq": w_uq.reshape(Q_LORA, A_HEADS * LANES).astype(BF16),
        "kv_norm": p["ev_kv_norm"][j].reshape(1, KV_LORA),
        "w_ukv": jnp.concatenate([w_uk, w_uv], axis=1).astype(BF16),
        "q_gain": _head_slot(p["ev_q_gain"][j][:A_NOPE], p["ev_q_gain"][j][A_NOPE:]).reshape(1, LANES),
        "k_gain": _head_slot(p["ev_k_gain"][j][:A_NOPE], p["ev_k_gain"][j][A_NOPE:]).reshape(1, LANES),
        "w_gate": w_gate.astype(BF16),
        "gate_b": p["ev_gate_b"][j].reshape(1, 2 * B_KW),
        "cos": rope[0], "sin": rope[1],
        "gla_norm": p["ev_gla_norm"][j].reshape(1, B_WIDTH),
    }


def _nbr_bias_table(rpb):
    col = np.arange(GRID_W)
    col_start = np.clip(col - WIN_COLS // 2, 0, GRID_W - WIN_COLS)
    col_mask = (col[None, :] >= col_start[:, None]) & (col[None, :] < col_start[:, None] + WIN_COLS)
    col_idx = np.clip(col[None, :] - col[:, None] + WIN_COLS - 1, 0, 2 * WIN_COLS - 2)
    n_rel = 2 * WIN_COLS - 1
    n_rows = 2 * WIN_ROWS - 1
    pick = ((col_idx[None] == np.arange(n_rel)[:, None, None]) & col_mask[None]) * LOG2E
    select = np.zeros((2 * n_rel + 3, GRID_W, 2, GRID_W), np.float32)
    for half in range(2):
        select[half * n_rel:(half + 1) * n_rel, :, half, :] = pick
        select[2 * n_rel, :, half, :] = np.where(col_mask, 0.0, NEG)
        select[2 * n_rel + 1 + half, :, half, :] = NEG
    slots = np.arange(BIAS_SLOTS)
    outside = np.stack([slots - 1 < 0, slots >= n_rows], axis=1).astype(np.float32)
    ext = jnp.pad(rpb, ((0, 0), (1, 1), (0, 0)))
    flags = jnp.broadcast_to(jnp.asarray(np.concatenate([np.ones((BIAS_SLOTS, 1), np.float32), outside], axis=1)),
                             (C_HEADS, BIAS_SLOTS, 3))
    lhs = jnp.concatenate([ext[:, :-1], ext[:, 1:], flags], axis=-1).reshape(C_HEADS * BIAS_SLOTS, 2 * n_rel + 3)
    table = jnp.dot(lhs, jnp.asarray(select.reshape(2 * n_rel + 3, -1)), precision=lax.Precision.HIGHEST)
    return table.reshape(C_HEADS // 2, 2, BIAS_SLOTS, GRID_W, LANES)


def kernel(x, c, ctx, c_ctx, norm_g, ada_w, ada_b, ev_w_in, ev_q_norm, ev_w_uq, ev_kv_norm, ev_w_ukv, ev_q_gain,
           ev_k_gain, ev_gate_w, ev_gate_b, ev_gla_norm, ev_w_out, od_w_in, od_q_gain, od_k_gain, od_rpb, od_w_out):
    p = dict(ev_w_in=ev_w_in, ev_q_norm=ev_q_norm, ev_w_uq=ev_w_uq, ev_kv_norm=ev_kv_norm, ev_w_ukv=ev_w_ukv,
             ev_q_gain=ev_q_gain, ev_k_gain=ev_k_gain, ev_gate_w=ev_gate_w, ev_gate_b=ev_gate_b,
             ev_gla_norm=ev_gla_norm, ev_w_out=ev_w_out)
    c16 = jnp.zeros((MOD_ROWS, D), F32).at[:BATCH].set(c).at[BATCH].set(c_ctx)
    mod_all = _modulation(c16, ada_w, ada_b).reshape(DEPTH, MOD_ROWS, 3, D)
    rope = _rope_tables()
    od_w_in_b, od_w_out_b, ev_w_out_b = od_w_in.astype(BF16), od_w_out.astype(BF16), ev_w_out.astype(BF16)
    xu = None
    finish, finish_name = _source_part(x, ctx), "inputs"
    for i in range(DEPTH):
        j = i // 2
        mod = mod_all[i]
        ng = norm_g[i].reshape(1, D)
        if i % 2 == 0:
            w = _even_weights(j, p, rope)
            start, start_name = _even_in_part(mod, ng, w), "even_in"
        else:
            start = _odd_in_part(mod, ng, od_w_in_b, j, _pad_lanes(jnp.tile(od_q_gain[j], 2)),
                                 _pad_lanes(jnp.tile(od_k_gain[j], 2)))
            start_name = "odd_in"
        outs = _boundary_call(f"{finish_name}_{start_name}", finish, start)
        if finish.writes_stream:
            xu, outs = outs[0], outs[1:]
        if i % 2 == 0:
            q, k, v, za, gq, gk, gv, gf, gb, zb = outs
            a = _mla_attn(q, k, v, za)
            gla = _gla(gq, gk, gv, gf, gb)
            stream = (x, ctx, 0) if xu is None else (xu, xu, CTX_TILE)
            finish = _even_out_part(a, gla, zb, w["gla_norm"], ev_w_out_b, j, stream, mod)
            finish_name = "even_out"
        else:
            q, k, v, z = outs
            o = _nbr_attn(q, k, v, z, _nbr_bias_table(od_rpb[j]), update_ctx=i < DEPTH - 1)
            rows = TILE if i < DEPTH - 1 else LAST_ROWS
            finish, finish_name = _odd_out_part(o, od_w_out_b, j, xu, mod, rows), "odd_out"
    return _boundary_call(finish_name, finish, None, n_tiles=SEQ // LAST_ROWS, rows=LAST_ROWS)[0]
```
